```python
import math
import jax, jax.numpy as jnp
from jax import lax
import numpy as np

D_MODEL = 1024
BATCH = 2
SEQ = 8192
DEPTH = 1
DEC_BATCH = 32
DEC_SEQ = 4
PAST_LEN = 16384
PAGE_SIZE = 128

N_HEADS_A = 8
HEAD_DIM_A = 64
WIDTH_A = N_HEADS_A * HEAD_DIM_A
MOBA_BLOCK = 256
MOBA_TOPK = 3
Q_CHUNK = 64
ROT_DIM = HEAD_DIM_A // 4
ROPE_THETA = 500000.0
N_HEADS_B = 8
HEAD_DIM_B = 64
WIDTH_B = N_HEADS_B * HEAD_DIM_B
DECAY_LORA = 64
AAA_LORA = 64
GATE_LORA = 128
RWKV_COLS = 3 * WIDTH_B + DECAY_LORA + AAA_LORA + GATE_LORA
DECAY_SCALE = math.exp(-0.5)
LN_X_EPS = 64e-5
N_MEM = 256
N_HEADS_M = 4
HEAD_DIM_M = 128
WIDTH_M = N_HEADS_M * HEAD_DIM_M
N_BRANCH = 3
BRANCH_WIDTH = 512
IN_WIDTH = 3 * WIDTH_A + RWKV_COLS + WIDTH_M + N_BRANCH * D_MODEL
IN_SPLITS = [WIDTH_A, 2 * WIDTH_A, 3 * WIDTH_A, 3 * WIDTH_A + RWKV_COLS, 3 * WIDTH_A + RWKV_COLS + WIDTH_M]
RWKV_SPLITS = [WIDTH_B, 2 * WIDTH_B, 3 * WIDTH_B, 3 * WIDTH_B + DECAY_LORA, 3 * WIDTH_B + DECAY_LORA + AAA_LORA]
N_EXPERTS = 32
TOP_K = 4
D_FF = D_MODEL
SWIGLU_ALPHA = 1.702
SWIGLU_LIMIT = 7.0
MOE_BLOCK = 128
NORM_EPS = 1e-6
F32 = jnp.float32

kernel_name = 'moba_rwkv7_memxattn_moe_step'


def rms_norm(x, g):
    xf = x.astype(F32)
    y = xf * lax.rsqrt(jnp.mean(xf * xf, axis=-1, keepdims=True) + NORM_EPS)
    return (y * g.astype(F32)).astype(x.dtype)


def partial_rope(x, pos):
    half = ROT_DIM // 2
    inv_freq = 1.0 / (ROPE_THETA ** (jnp.arange(0, ROT_DIM, 2, dtype=F32) / ROT_DIM))
    ang = pos.astype(F32)[:, None] * inv_freq[None, :]
    cos, sin = jnp.cos(ang)[:, None, :], jnp.sin(ang)[:, None, :]
    xf = x.astype(F32)
    x1, x2 = xf[..., :half], xf[..., half:ROT_DIM]
    rot = jnp.concatenate([x1 * cos - x2 * sin, x2 * cos + x1 * sin], axis=-1).astype(x.dtype)
    return jnp.concatenate([rot, x[..., ROT_DIM:]], axis=-1)


def mixer_projections(x, p):
    h = rms_norm(x, p['norm_mix'])
    return jnp.split(h @ p['w_in'], IN_SPLITS, axis=-1)


def moba_heads(q, k, v, pos, p):
    B, S, _ = q.shape
    heads = lambda t: t.reshape(B, S, N_HEADS_A, HEAD_DIM_A)
    q = partial_rope(rms_norm(heads(q), p['q_norm_a']), pos)
    k = partial_rope(rms_norm(heads(k), p['k_norm_a']), pos)
    return q, k, heads(v)


def moba_prompt(q, k, v):
    B, S, H, Dh = q.shape
    nb = -(-S // MOBA_BLOCK)
    pad = nb * MOBA_BLOCK - S

    def to_blocks(t):
        t = jnp.pad(t, ((0, 0), (0, pad), (0, 0), (0, 0)))
        return t.reshape(B, nb, MOBA_BLOCK, H, Dh).transpose(0, 3, 1, 2, 4)

    kb, vb = to_blocks(k), to_blocks(v)
    k_mean = jnp.mean(kb.astype(F32), axis=3)
    k_sel = min(MOBA_TOPK, nb - 1)
    scale = Dh ** -0.5
    n_chunks = S // Q_CHUNK
    q_chunks = q.reshape(B, n_chunks, Q_CHUNK, H, Dh).transpose(1, 0, 2, 3, 4)
    b_idx = jnp.arange(B)[:, None, None, None]
    h_idx = jnp.arange(H)[None, None, :, None]
    blk_pos = jnp.arange(MOBA_BLOCK)

    def one_chunk(args):
        qc, c = args
        qf = qc.astype(F32) * scale
        q_pos = c * Q_CHUNK + jnp.arange(Q_CHUNK)
        own = (c * Q_CHUNK) // MOBA_BLOCK
        k_own = lax.dynamic_index_in_dim(kb, own, axis=2, keepdims=False).astype(F32)
        v_own = lax.dynamic_index_in_dim(vb, own, axis=2, keepdims=False).astype(F32)
        s_own = jnp.einsum('bqhd,bhkd->bqhk', qf, k_own)
        causal = (own * MOBA_BLOCK + blk_pos)[None, :] <= q_pos[:, None]
        s_own = jnp.where(causal[None, :, None, :], s_own, -jnp.inf)
        if k_sel == 0:
            return jnp.einsum('bqhk,bhkd->bqhd', jax.nn.softmax(s_own, axis=-1), v_own)
        gate = jnp.einsum('bqhd,bhnd->bqhn', qf, k_mean)
        gate = jnp.where(jnp.arange(nb) < own, gate, -jnp.inf)
        _, idx = lax.top_k(gate, k_sel)
        k_g = kb[b_idx, h_idx, idx].astype(F32)
        v_g = vb[b_idx, h_idx, idx].astype(F32)
        s_sel = jnp.einsum('bqhd,bqhjkd->bqhjk', qf, k_g)
        s_sel = jnp.where((idx < own)[..., None], s_sel, -jnp.inf)
        n_sel = k_sel * MOBA_BLOCK
        s_all = jnp.concatenate([s_sel.reshape(B, Q_CHUNK, H, n_sel), s_own], axis=-1)
        pr = jax.nn.softmax(s_all, axis=-1)
        p_sel = pr[..., :n_sel].reshape(B, Q_CHUNK, H, k_sel, MOBA_BLOCK)
        return (jnp.einsum('bqhjk,bqhjkd->bqhd', p_sel, v_g)
                + jnp.einsum('bqhk,bhkd->bqhd', pr[..., n_sel:], v_own))

    out = lax.map(one_chunk, (q_chunks, jnp.arange(n_chunks)))
    return out.transpose(1, 0, 2, 3, 4).reshape(B, S, H, Dh).astype(q.dtype)


def moba_sample(q, k, v, cache_k, cache_v, page_table, layer):
    DB, T, H, Dh = q.shape
    ppb = MOBA_BLOCK // PAGE_SIZE
    n_full = PAST_LEN // MOBA_BLOCK
    own_start = n_full * MOBA_BLOCK
    own_page0 = own_start // PAGE_SIZE
    n_own_pages = PAST_LEN // PAGE_SIZE - own_page0
    k_sel = min(MOBA_TOPK, n_full)
    scale = Dh ** -0.5
    qf = q.astype(F32) * scale
    q_pos = PAST_LEN + jnp.arange(T)
    own_pages = page_table[:, own_page0:own_page0 + n_own_pages]

    def own_rows(cache):
        rows = cache[layer, own_pages]
        return rows.transpose(0, 2, 1, 3, 4).reshape(DB, H, n_own_pages * PAGE_SIZE, Dh)

    k_own = jnp.concatenate([own_rows(cache_k), k.transpose(0, 2, 1, 3)], axis=2).astype(F32)
    v_own = jnp.concatenate([own_rows(cache_v), v.transpose(0, 2, 1, 3)], axis=2).astype(F32)
    k_pos = own_start + jnp.arange(k_own.shape[2])
    s_own = jnp.einsum('bqhd,bhkd->bqhk', qf, k_own)
    s_own = jnp.where((k_pos[None, :] <= q_pos[:, None])[None, :, None, :], s_own, -jnp.inf)
    if k_sel == 0:
        return jnp.einsum('bqhk,bhkd->bqhd', jax.nn.softmax(s_own, axis=-1), v_own).astype(q.dtype)
    k_full = cache_k[layer, page_table[:, :n_full * ppb]].astype(F32)
    k_mean = k_full.reshape(DB, n_full, ppb, H, PAGE_SIZE, Dh).mean(axis=(2, 4))
    gate = jnp.einsum('bqhd,bnhd->bqhn', qf, k_mean)
    _, idx = lax.top_k(gate, k_sel)
    logical = idx[..., None] * ppb + jnp.arange(ppb)
    phys = page_table[jnp.arange(DB)[:, None, None, None, None], logical]
    h_idx = jnp.arange(H)[None, None, :, None, None]
    k_g = cache_k[layer, phys, h_idx].astype(F32).reshape(DB, T, H, k_sel, MOBA_BLOCK, Dh)
    v_g = cache_v[layer, phys, h_idx].astype(F32).reshape(DB, T, H, k_sel, MOBA_BLOCK, Dh)
    s_sel = jnp.einsum('bqhd,bqhjkd->bqhjk', qf, k_g)
    n_sel = k_sel * MOBA_BLOCK
    s_all = jnp.concatenate([s_sel.reshape(DB, T, H, n_sel), s_own], axis=-1)
    pr = jax.nn.softmax(s_all, axis=-1)
    p_sel = pr[..., :n_sel].reshape(DB, T, H, k_sel, MOBA_BLOCK)
    out = (jnp.einsum('bqhjk,bqhjkd->bqhd', p_sel, v_g)
           + jnp.einsum('bqhk,bhkd->bqhd', pr[..., n_sel:], v_own))
    return out.astype(q.dtype)


def rwkv_features(p_cur, p_prev, p):
    B, S, _ = p_cur.shape
    xs = (p_cur + (p_prev - p_cur) * p['rw_mu']).astype(F32)
    r, k, v, dw, da, dg = jnp.split(xs, RWKV_SPLITS, axis=-1)
    heads = lambda t: t.reshape(B, S, N_HEADS_B, HEAD_DIM_B)
    w = jnp.exp(-DECAY_SCALE * jax.nn.sigmoid(p['rw_w0'] + jnp.tanh(dw) @ p['rw_decay_up']))
    a = jax.nn.sigmoid(p['rw_a0'] + da @ p['rw_a_up'])
    g = jax.nn.sigmoid(dg) @ p['rw_g_up']
    kk = heads(k * p['rw_k_k'])
    kk = kk / jnp.maximum(jnp.sqrt(jnp.sum(kk * kk, axis=-1, keepdims=True)), 1e-12)
    k = k * (1.0 + (a - 1.0) * p['rw_k_a'])
    return heads(r), heads(w), heads(k), heads(v), kk, heads(a), g


def rwkv_mix(state0, p_cur, p_prev, p):
    r, w, k, v, kk, a, g = rwkv_features(p_cur, p_prev, p)
    B, S = p_cur.shape[:2]

    def step(s, inp):
        r_t, w_t, k_t, v_t, kk_t, a_t = inp
        s_kk = jnp.einsum('bhvk,bhk->bhv', s, kk_t)
        s = (s * w_t[:, :, None, :] - s_kk[..., None] * (kk_t * a_t)[:, :, None, :]
             + v_t[..., None] * k_t[:, :, None, :])
        return s, jnp.einsum('bhvk,bhk->bhv', s, r_t)

    seq_first = lambda t: jnp.swapaxes(t, 0, 1)
    state, y = lax.scan(step, state0, tuple(seq_first(t) for t in (r, w, k, v, kk, a)))
    y = seq_first(y)
    mu = jnp.mean(y, axis=-1, keepdims=True)
    var = jnp.mean(jnp.square(y - mu), axis=-1, keepdims=True)
    yn = ((y - mu) * lax.rsqrt(var + LN_X_EPS)).reshape(B, S, WIDTH_B) * p['ln_x_w'] + p['ln_x_b']
    bonus = (jnp.sum(r * k * p['rw_r_k'], axis=-1, keepdims=True) * v).reshape(B, S, WIDTH_B)
    return (yn + bonus) * g, state


def memory_kv(mem, p):
    B, M, _ = mem.shape
    kv = rms_norm(mem, p['norm_mem']) @ p['w_mem_kv']
    mk, mv = jnp.split(kv, 2, axis=-1)
    mk = rms_norm(mk.reshape(B, M, N_HEADS_M, HEAD_DIM_M), p['k_norm_m'])
    return mk, mv.reshape(B, M, N_HEADS_M, HEAD_DIM_M)


def memory_attend(q_m, mk, mv, p):
    B, S, _ = q_m.shape
    q = rms_norm(q_m.reshape(B, S, N_HEADS_M, HEAD_DIM_M), p['q_norm_m']).astype(F32) * HEAD_DIM_M ** -0.5
    s = jnp.einsum('bqhd,bmhd->bhqm', q, mk.astype(F32))
    return jnp.einsum('bhqm,bmhd->bqhd', jax.nn.softmax(s, axis=-1), mv.astype(F32))


def merge_branches(x, o_a, o_b, o_m, gate_logits, p):
    B, S, _ = x.shape
    o = jnp.stack([o_a.reshape(B, S, BRANCH_WIDTH).astype(x.dtype), o_b.reshape(B, S, BRANCH_WIDTH).astype(x.dtype),
                   o_m.reshape(B, S, BRANCH_WIDTH).astype(x.dtype)], axis=2)
    y_br = jnp.einsum('bsnc,ncd->bsnd', o, p['w_branch'])
    gates = jax.nn.sigmoid(gate_logits.reshape(B, S, N_BRANCH, D_MODEL).astype(F32))
    merged = jnp.sum(gates * y_br.astype(F32), axis=2).astype(x.dtype)
    return x + (merged @ p['w_out']).astype(x.dtype)


def clamped_swiglu(u):
    u_glu = jnp.minimum(u[..., :D_FF], SWIGLU_LIMIT)
    u_lin = jnp.clip(u[..., D_FF:], -SWIGLU_LIMIT, SWIGLU_LIMIT)
    return u_glu * jax.nn.sigmoid(SWIGLU_ALPHA * u_glu) * (u_lin + 1.0)


def moe_ffn(h, p):
    N, D = h.shape
    logits = (h @ p['router_w']).astype(F32) + p['router_b'].astype(F32)
    top_logit, top_e = lax.top_k(logits, TOP_K)
    gates = jax.nn.softmax(top_logit, axis=-1)
    e_flat = top_e.reshape(-1)
    tok_flat = jnp.repeat(jnp.arange(N, dtype=jnp.int32), TOP_K)
    g_flat = gates.reshape(-1)
    onehot = jax.nn.one_hot(e_flat, N_EXPERTS, dtype=jnp.int32)
    counts = jnp.sum(onehot, axis=0)
    rank = jnp.sum((jnp.cumsum(onehot, axis=0) - onehot) * onehot, axis=1)
    padded = (counts + MOE_BLOCK - 1) // MOE_BLOCK * MOE_BLOCK
    p_end = jnp.cumsum(padded)
    row = (p_end - padded)[e_flat] + rank
    n_blocks = -(-(N * TOP_K + N_EXPERTS * (MOE_BLOCK - 1)) // MOE_BLOCK)
    n_rows = n_blocks * MOE_BLOCK
    row_tok = jnp.full((n_rows,), N, jnp.int32).at[row].set(tok_flat)
    h_pad = jnp.concatenate([h, jnp.zeros((1, D), h.dtype)], axis=0)
    xb = h_pad[row_tok].reshape(n_blocks, MOE_BLOCK, D)
    blk_e = jnp.minimum(jnp.searchsorted(p_end, jnp.arange(n_blocks, dtype=jnp.int32) * MOE_BLOCK, side='right'),
                        N_EXPERTS - 1)
    w1, b1, w2, b2 = p['moe_w1'], p['moe_b1'], p['moe_w2'], p['moe_b2']

    def expert_block(args):
        xblk, e = args
        return clamped_swiglu(xblk @ w1[e] + b1[e]) @ w2[e] + b2[e]

    yb = lax.map(expert_block, (xb, blk_e)).reshape(n_rows, D)
    contrib = yb[row].astype(F32) * g_flat[:, None]
    return jax.ops.segment_sum(contrib, tok_flat, num_segments=N)


def ffn_block(x, p):
    B, S, D = x.shape
    h = rms_norm(x, p['norm_ffn']).reshape(B * S, D)
    return x + moe_ffn(h, p).reshape(B, S, D).astype(x.dtype)


def layer_prompt(x, mem, p):
    B, S, _ = x.shape
    pos = jnp.arange(S, dtype=jnp.int32)
    q_a, k_a, v_a, p_rw, q_m, gate_logits = mixer_projections(x, p)
    q_a, k_a, v_a = moba_heads(q_a, k_a, v_a, pos, p)
    o_a = moba_prompt(q_a, k_a, v_a)
    p_prev = jnp.concatenate([jnp.zeros_like(p_rw[:, :1]), p_rw[:, :-1]], axis=1)
    o_b, wkv = rwkv_mix(jnp.zeros((B, N_HEADS_B, HEAD_DIM_B, HEAD_DIM_B), F32), p_rw, p_prev, p)
    mk, mv = memory_kv(mem, p)
    o_m = memory_attend(q_m, mk, mv, p)
    x = ffn_block(merge_branches(x, o_a, o_b, o_m, gate_logits, p), p)
    return x, k_a.transpose(0, 2, 1, 3), v_a.transpose(0, 2, 1, 3), wkv, p_rw[:, -1], mk, mv


def layer_sample(x, cache_k, cache_v, mem_k, mem_v, wkv0, shift0, page_table, layer, p):
    DB, T, _ = x.shape
    pos = PAST_LEN + jnp.arange(T, dtype=jnp.int32)
    q_a, k_a, v_a, p_rw, q_m, gate_logits = mixer_projections(x, p)
    q_a, k_a, v_a = moba_heads(q_a, k_a, v_a, pos, p)
    o_a = moba_sample(q_a, k_a, v_a, cache_k, cache_v, page_table, layer)
    p_prev = jnp.concatenate([shift0[:, None, :].astype(p_rw.dtype), p_rw[:, :-1]], axis=1)
    o_b, wkv = rwkv_mix(wkv0.astype(F32), p_rw, p_prev, p)
    o_m = memory_attend(q_m, mem_k, mem_v, p)
    x = ffn_block(merge_branches(x, o_a, o_b, o_m, gate_logits, p), p)
    return x, k_a.transpose(0, 2, 1, 3), v_a.transpose(0, 2, 1, 3), wkv, p_rw[:, -1]


def setup_inputs(seed: int = 0) -> dict:
    key = jax.random.key(seed)
    keys = iter(jax.random.split(key, 64))
    L = DEPTH

    def nrm(shape, scale):
        return jax.random.normal(next(keys), shape, F32) * scale

    def gain(shape):
        return 1.0 + nrm(shape, 0.05)

    def unif(shape, lo, hi):
        return jax.random.uniform(next(keys), shape, F32, lo, hi)

    n_pages = PAST_LEN // PAGE_SIZE
    n_used = DEC_BATCH * n_pages
    n_pool = n_used + n_used // 4
    page_table = jax.random.permutation(next(keys), n_pool)[:n_used].reshape(DEC_BATCH, n_pages).astype(jnp.int32)
    return {
        'x_prompt': nrm((BATCH, SEQ, D_MODEL), 1.0),
        'x_sample': nrm((DEC_BATCH, DEC_SEQ, D_MODEL), 1.0),
        'mem_prompt': nrm((BATCH, N_MEM, D_MODEL), 1.0),
        'cache_k': nrm((L, n_pool, N_HEADS_A, PAGE_SIZE, HEAD_DIM_A), 1.0),
        'cache_v': nrm((L, n_pool, N_HEADS_A, PAGE_SIZE, HEAD_DIM_A), 1.0),
        'cache_mem_k': nrm((L, DEC_BATCH, N_MEM, N_HEADS_M, HEAD_DIM_M), 1.0),
        'cache_mem_v': nrm((L, DEC_BATCH, N_MEM, N_HEADS_M, HEAD_DIM_M), 1.0),
        'state_wkv': nrm((L, DEC_BATCH, N_HEADS_B, HEAD_DIM_B, HEAD_DIM_B), 0.3),
        'state_shift': nrm((L, DEC_BATCH, RWKV_COLS), 1.0),
        'page_table': page_table,
        'norm_mix': gain((L, D_MODEL)),
        'norm_mem': gain((L, D_MODEL)),
        'norm_ffn': gain((L, D_MODEL)),
        'w_in': nrm((L, D_MODEL, IN_WIDTH), D_MODEL ** -0.5),
        'q_norm_a': gain((L, HEAD_DIM_A)),
        'k_norm_a': gain((L, HEAD_DIM_A)),
        'q_norm_m': gain((L, HEAD_DIM_M)),
        'k_norm_m': gain((L, HEAD_DIM_M)),
        'w_mem_kv': nrm((L, D_MODEL, 2 * WIDTH_M), D_MODEL ** -0.5),
        'rw_mu': unif((L, RWKV_COLS), 0.0, 1.0),
        'rw_w0': unif((L, WIDTH_B), -3.0, 1.0),
        'rw_decay_up': nrm((L, DECAY_LORA, WIDTH_B), 0.5 * DECAY_LORA ** -0.5),
        'rw_a0': nrm((L, WIDTH_B), 0.5),
        'rw_a_up': nrm((L, AAA_LORA, WIDTH_B), 0.5 * AAA_LORA ** -0.5),
        'rw_g_up': nrm((L, GATE_LORA, WIDTH_B), GATE_LORA ** -0.5),
        'rw_k_k': 0.85 + nrm((L, WIDTH_B), 0.05),
        'rw_k_a': gain((L, WIDTH_B)),
        'rw_r_k': nrm((L, N_HEADS_B, HEAD_DIM_B), 0.1),
        'ln_x_w': gain((L, WIDTH_B)),
        'ln_x_b': nrm((L, WIDTH_B), 0.02),
        'w_branch': nrm((L, N_BRANCH, BRANCH_WIDTH, D_MODEL), BRANCH_WIDTH ** -0.5),
        'w_out': nrm((L, D_MODEL, D_MODEL), D_MODEL ** -0.5),
        'router_w': nrm((L, D_MODEL, N_EXPERTS), D_MODEL ** -0.5),
        'router_b': nrm((L, N_EXPERTS), 0.01),
        'moe_w1': nrm((L, N_EXPERTS, D_MODEL, 2 * D_FF), D_MODEL ** -0.5),
        'moe_b1': nrm((L, N_EXPERTS, 2 * D_FF), 0.01),
        'moe_w2': nrm((L, N_EXPERTS, D_FF, D_MODEL), D_FF ** -0.5),
        'moe_b2': nrm((L, N_EXPERTS, D_MODEL), 0.01),
    }


def reference(x_prompt, x_sample, mem_prompt, cache_k, cache_v, cache_mem_k, cache_mem_v,
              state_wkv, state_shift, page_table, norm_mix, norm_mem, norm_ffn, w_in,
              q_norm_a, k_norm_a, q_norm_m, k_norm_m, w_mem_kv, rw_mu, rw_w0, rw_decay_up,
              rw_a0, rw_a_up, rw_g_up, rw_k_k, rw_k_a, rw_r_k, ln_x_w, ln_x_b, w_branch, w_out,
              router_w, router_b, moe_w1, moe_b1, moe_w2, moe_b2):
    xp, xs = x_prompt, x_sample
    kp, vp, wkvp, shp, mkp, mvp = [], [], [], [], [], []
    ksm, vsm, wkvs, shs = [], [], [], []
    for l in range(DEPTH):
        p = dict(norm_mix=norm_mix[l], norm_mem=norm_mem[l], norm_ffn=norm_ffn[l], w_in=w_in[l],
                 q_norm_a=q_norm_a[l], k_norm_a=k_norm_a[l], q_norm_m=q_norm_m[l], k_norm_m=k_norm_m[l],
                 w_mem_kv=w_mem_kv[l], rw_mu=rw_mu[l], rw_w0=rw_w0[l], rw_decay_up=rw_decay_up[l],
                 rw_a0=rw_a0[l], rw_a_up=rw_a_up[l], rw_g_up=rw_g_up[l], rw_k_k=rw_k_k[l], rw_k_a=rw_k_a[l],
                 rw_r_k=rw_r_k[l], ln_x_w=ln_x_w[l], ln_x_b=ln_x_b[l], w_branch=w_branch[l], w_out=w_out[l],
                 router_w=router_w[l], router_b=router_b[l], moe_w1=moe_w1[l], moe_b1=moe_b1[l],
                 moe_w2=moe_w2[l], moe_b2=moe_b2[l])
        xp, k_r, v_r, wkv_r, sh_r, mk_r, mv_r = layer_prompt(xp, mem_prompt, p)
        kp.append(k_r); vp.append(v_r); wkvp.append(wkv_r); shp.append(sh_r); mkp.append(mk_r); mvp.append(mv_r)
        xs, k_s, v_s, wkv_s, sh_s = layer_sample(xs, cache_k, cache_v, cache_mem_k[l], cache_mem_v[l],
                                                 state_wkv[l], state_shift[l], page_table, l, p)
        ksm.append(k_s); vsm.append(v_s); wkvs.append(wkv_s); shs.append(sh_s)
    y_prompt, y_sample = xp, xs
    k_prompt, v_prompt = jnp.stack(kp), jnp.stack(vp)
    wkv_prompt, shift_prompt = jnp.stack(wkvp), jnp.stack(shp)
    mem_k_prompt, mem_v_prompt = jnp.stack(mkp), jnp.stack(mvp)
    k_sample, v_sample = jnp.stack(ksm), jnp.stack(vsm)
    wkv_sample, shift_sample = jnp.stack(wkvs), jnp.stack(shs)
    return (y_prompt, y_sample, k_prompt, v_prompt, wkv_prompt, shift_prompt, mem_k_prompt, mem_v_prompt,
            k_sample, v_sample, wkv_sample, shift_sample)
```

```python
import functools
import math

import jax
import jax.numpy as jnp
from jax import lax
from jax.experimental import pallas as pl
from jax.experimental.pallas import tpu as pltpu

F32 = jnp.float32
BF16 = jnp.bfloat16
I32 = jnp.int32

N_HEADS_A = 8
HEAD_DIM_A = 64
WIDTH_A = 512
MOBA_BLOCK = 256
MOBA_TOPK = 3
ROT_DIM = 16
ROPE_THETA = 500000.0
PAGE_SIZE = 128
N_HEADS_B = 8
HEAD_DIM_B = 64
WIDTH_B = 512
DECAY_LORA = 64
AAA_LORA = 64
GATE_LORA = 128
RWKV_COLS = 1792
DECAY_SCALE = math.exp(-0.5)
LN_X_EPS = 64e-5
N_HEADS_M = 4
HEAD_DIM_M = 128
WIDTH_M = 512
N_EXPERTS = 32
TOP_K = 4
SWIGLU_ALPHA = 1.702
SWIGLU_LIMIT = 7.0
NORM_EPS = 1e-6

NEG_BIG = -1e30
SAMPLE_T_PAD = 8
RWKV_CHUNK = 64
MOE_ROWS = 256
VMEM_LIMIT = 56 * 1024 * 1024


def _cparams(sem, vmem=None):
    return pltpu.CompilerParams(dimension_semantics=sem, vmem_limit_bytes=vmem or VMEM_LIMIT)


def _dg(a, b, ca, cb):
    return lax.dot_general(a, b, (((ca,), (cb,)), ((), ())), preferred_element_type=F32)


def _split(x):
    hi = x.astype(BF16)
    lo = (x - hi.astype(F32)).astype(BF16)
    return hi, lo


def _dot3(a, b, ca=1, cb=0):
    ah, al = _split(a)
    bh, bl = _split(b)
    return _dg(ah, bh, ca, cb) + (_dg(ah, bl, ca, cb) + _dg(al, bh, ca, cb))


def _dot2_exact_rhs(a, b_bf16):
    ah, al = _split(a)
    return _dg(ah, b_bf16, 1, 0) + _dg(al, b_bf16, 1, 0)


def _dotb(a, b, ca=1, cb=0):
    return _dg(a.astype(BF16), b.astype(BF16), ca, cb)


def _rms(x, gain_row):
    ms = jnp.mean(x * x, axis=-1, keepdims=True)
    return x * lax.rsqrt(ms + NORM_EPS) * gain_row


def _seg_ones(width, seg):
    r = lax.broadcasted_iota(I32, (width, width), 0) // seg
    c = lax.broadcasted_iota(I32, (width, width), 1) // seg
    return jnp.where(r == c, 1.0, 0.0).astype(BF16)


def _proj_kernel(x_ref, g_ref, w1, w2, w3, w4, o1, o2, o3, o4):
    h = _rms(x_ref[...], g_ref[...]).astype(BF16)
    o1[...] = _dg(h, w1[...], 1, 0)
    o2[...] = _dg(h, w2[...], 1, 0)
    o3[...] = _dg(h, w3[...], 1, 0)
    o4[...] = _dg(h, w4[...], 1, 0)


def _in_projection(x2, gain, ws, tm):
    n, d = x2.shape
    widths = [w.shape[1] for w in ws]
    const = lambda i: (0, 0)
    return pl.pallas_call(
        _proj_kernel,
        grid=(n // tm,),
        in_specs=[pl.BlockSpec((tm, d), lambda i: (i, 0)), pl.BlockSpec((1, d), const)]
        + [pl.BlockSpec((d, wd), const) for wd in widths],
        out_specs=[pl.BlockSpec((tm, wd), lambda i: (i, 0)) for wd in widths],
        out_shape=[jax.ShapeDtypeStruct((n, wd), F32) for wd in widths],
        compiler_params=_cparams(("parallel",)),
        name="in_projection",
    )(x2, gain, *ws)


def _rope_tables(pos):
    half = ROT_DIM // 2
    inv_freq = 1.0 / (ROPE_THETA ** (jnp.arange(0, ROT_DIM, 2, dtype=F32) / ROT_DIM))
    ang = pos.astype(F32)[:, None] * inv_freq[None, :]
    cos, sin = jnp.cos(ang), jnp.sin(ang)
    n = pos.shape[0]
    rest = HEAD_DIM_A - ROT_DIM
    c = jnp.concatenate([cos, cos, jnp.ones((n, rest), F32)], axis=1)
    s_up = jnp.concatenate([-sin, jnp.zeros((n, half + rest), F32)], axis=1)
    s_dn = jnp.concatenate([jnp.zeros((n, half), F32), sin, jnp.zeros((n, rest), F32)], axis=1)
    two = lambda t: jnp.concatenate([t, t], axis=1)
    return two(c), two(s_up), two(s_dn)


def _norm_rope(x, seg, gain, c, s_up, s_dn):
    ss = _dot2_exact_rhs(x * x, seg)
    y = x * lax.rsqrt(ss * (1.0 / HEAD_DIM_A) + NORM_EPS) * gain
    half = ROT_DIM // 2
    up = pltpu.roll(y, WIDTH_A - half, 1)
    dn = pltpu.roll(y, half, 1)
    return y * c + up * s_up + dn * s_dn


def _moba_prep_kernel(with_blocks, qkv_ref, seg_ref, qg_ref, kg_ref, c_ref, su_ref, sd_ref, *outs):
    if with_blocks:
        qs_ref, k_ref, v_ref, kaug_ref, vb_ref, kmean_ref = outs
    else:
        qs_ref, k_ref, v_ref = outs
    x = qkv_ref[0]
    tm = x.shape[0]
    rep = lambda r: jnp.concatenate([r[...]] * (WIDTH_A // 128), axis=1)
    c, su, sd = rep(c_ref), rep(su_ref), rep(sd_ref)
    seg = seg_ref[...]
    q = _norm_rope(x[:, :WIDTH_A], seg, qg_ref[...], c, su, sd) * (HEAD_DIM_A ** -0.5)
    k = _norm_rope(x[:, WIDTH_A:2 * WIDTH_A], seg, kg_ref[...], c, su, sd)
    v = x[:, 2 * WIDTH_A:]
    if with_blocks:
        blk = pl.program_id(1)
        lane = lax.broadcasted_iota(I32, (tm, HEAD_DIM_A), 1)
        onehot = jnp.where(lane == blk, 1.0, 0.0).astype(BF16)
    for h in range(N_HEADS_A):
        sl = slice(h * HEAD_DIM_A, (h + 1) * HEAD_DIM_A)
        qs_ref[0, h] = q[:, sl]
        k_ref[0, h] = k[:, sl]
        v_ref[0, h] = v[:, sl]
        if with_blocks:
            kaug_ref[0, h] = jnp.concatenate([k[:, sl].astype(BF16), onehot], axis=1)
            vb_ref[0, h] = v[:, sl].astype(BF16)
            kmean_ref[0, h, pl.ds(blk, 1), :] = jnp.mean(k[:, sl], axis=0, keepdims=True)


def _moba_prep(qkv, pos, q_gain, k_gain, tm, with_blocks):
    bq, s, _ = qkv.shape
    nb = s // tm
    c, su, sd = _rope_tables(pos)
    seg = _seg_ones(WIDTH_A, HEAD_DIM_A)
    tile8 = lambda g: jnp.tile(g.astype(F32), N_HEADS_A)[None, :]
    hm = lambda dt: jax.ShapeDtypeStruct((bq, N_HEADS_A, s, HEAD_DIM_A), dt)
    hm_spec = pl.BlockSpec((1, N_HEADS_A, tm, HEAD_DIM_A), lambda b, j: (b, 0, j, 0))
    out_shape = [hm(F32), hm(F32), hm(F32)]
    out_specs = [hm_spec, hm_spec, hm_spec]
    if with_blocks:
        out_shape += [jax.ShapeDtypeStruct((bq, N_HEADS_A, s, 128), BF16), hm(BF16),
                      jax.ShapeDtypeStruct((bq, N_HEADS_A, nb, HEAD_DIM_A), F32)]
        out_specs += [pl.BlockSpec((1, N_HEADS_A, tm, 128), lambda b, j: (b, 0, j, 0)), hm_spec,
                      pl.BlockSpec((1, N_HEADS_A, nb, HEAD_DIM_A), lambda b, j: (b, 0, 0, 0))]
    const = lambda b, j: (0, 0)
    tab = pl.BlockSpec((tm, 128), lambda b, j: (j, 0))
    return pl.pallas_call(
        functools.partial(_moba_prep_kernel, with_blocks),
        grid=(bq, nb),
        in_specs=[pl.BlockSpec((1, tm, 3 * WIDTH_A), lambda b, j: (b, j, 0)),
                  pl.BlockSpec((WIDTH_A, WIDTH_A), const),
                  pl.BlockSpec((1, WIDTH_A), const), pl.BlockSpec((1, WIDTH_A), const), tab, tab, tab],
        out_specs=out_specs,
        out_shape=out_shape,
        compiler_params=_cparams(("parallel", "arbitrary")),
        name="moba_prep",
    )(qkv, seg, tile8(q_gain), tile8(k_gain), c, su, sd)


def _top_select(g, n_idx, n_total, k_sel):
    sel = jnp.zeros(g.shape, jnp.bool_)
    for _ in range(k_sel):
        mx = jnp.max(g, axis=1, keepdims=True)
        cand = (g == mx) & (mx > -jnp.inf)
        first = jnp.min(jnp.where(cand, n_idx, n_total), axis=1, keepdims=True)
        pick = n_idx == first
        sel = sel | pick
        g = jnp.where(pick, -jnp.inf, g)
    return sel


def _moba_flash_kernel(q_ref, kaug_ref, v_ref, kmean_ref, o_ref, m_sc, l_sc, acc_sc):
    i = pl.program_id(1)
    h = pl.program_id(2)
    tq = q_ref.shape[2]
    nb = kmean_ref.shape[2]
    q = q_ref[0, 0]
    gate = _dot3(q, kmean_ref[0, 0], 1, 1)
    n_idx = lax.broadcasted_iota(I32, (tq, nb), 1)
    sel = _top_select(jnp.where(n_idx < i, gate, -jnp.inf), n_idx, nb, MOBA_TOPK)
    bias = jnp.where(sel | (n_idx == i), 0.0, NEG_BIG).astype(BF16)
    pad = jnp.zeros((tq, HEAD_DIM_A - nb), BF16)
    qaug = jnp.concatenate([q.astype(BF16), bias, pad], axis=1)

    start = pl.multiple_of(i * tq, tq)
    s = _dg(qaug, kaug_ref[0, 0, pl.ds(start, tq), :], 1, 1)
    row = lax.broadcasted_iota(I32, (tq, tq), 0)
    col = lax.broadcasted_iota(I32, (tq, tq), 1)
    s = jnp.where(col <= row, s, -jnp.inf)
    m0 = jnp.max(s, axis=1, keepdims=True)
    p = jnp.exp(s - m0)
    m_sc[...] = m0
    l_sc[...] = jnp.sum(p, axis=1, keepdims=True)
    acc_sc[...] = _dg(p.astype(BF16), v_ref[0, 0, pl.ds(start, tq), :], 1, 0)

    def body(j, carry):
        off = pl.multiple_of(j * tq, tq)
        sj = _dg(qaug, kaug_ref[0, 0, pl.ds(off, tq), :], 1, 1)
        m_old = m_sc[...]
        m_new = jnp.maximum(m_old, jnp.max(sj, axis=1, keepdims=True))
        alpha = jnp.exp(m_old - m_new)
        pj = jnp.exp(sj - m_new)
        l_sc[...] = alpha * l_sc[...] + jnp.sum(pj, axis=1, keepdims=True)
        acc_sc[...] = alpha * acc_sc[...] + _dg(pj.astype(BF16), v_ref[0, 0, pl.ds(off, tq), :], 1, 0)
        m_sc[...] = m_new
        return carry

    lax.fori_loop(0, i, body, 0)
    out = (acc_sc[...] / l_sc[...]).astype(o_ref.dtype)
    for hh in range(N_HEADS_A):
        @pl.when(h == hh)
        def _():
            o_ref[0, :, hh * HEAD_DIM_A:(hh + 1) * HEAD_DIM_A] = out


def _moba_flash(q_s, kaug, vb, kmean):
    b, nh, s, dh = q_s.shape
    tq = MOBA_BLOCK
    nb = s // tq
    assert nb <= HEAD_DIM_A
    return pl.pallas_call(
        _moba_flash_kernel,
        grid=(b, nb, nh),
        in_specs=[pl.BlockSpec((1, 1, tq, dh), lambda bi, i, h: (bi, h, i, 0)),
                  pl.BlockSpec((1, 1, s, 128), lambda bi, i, h: (bi, h, 0, 0)),
                  pl.BlockSpec((1, 1, s, dh), lambda bi, i, h: (bi, h, 0, 0)),
                  pl.BlockSpec((1, 1, nb, dh), lambda bi, i, h: (bi, h, 0, 0))],
        out_specs=pl.BlockSpec((1, tq, nh * dh), lambda bi, i, h: (bi, i, 0)),
        out_shape=jax.ShapeDtypeStruct((b, s, nh * dh), BF16),
        scratch_shapes=[pltpu.VMEM((tq, 1), F32), pltpu.VMEM((tq, 1), F32), pltpu.VMEM((tq, dh), F32)],
        compiler_params=_cparams(("parallel", "parallel", "arbitrary")),
        name="moba_flash",
    )(q_s, kaug, vb, kmean)


PAGES_PER_STEP = 16


def _page_mean_kernel(pt_ref, *refs):
    pages, out_ref = refs[:PAGES_PER_STEP], refs[PAGES_PER_STEP]
    ppb = MOBA_BLOCK // PAGE_SIZE
    for h in range(N_HEADS_A):
        rows = []
        for j in range(PAGES_PER_STEP // ppb):
            tot = jnp.sum(pages[ppb * j][h], axis=0, keepdims=True)
            for u in range(1, ppb):
                tot = tot + jnp.sum(pages[ppb * j + u][h], axis=0, keepdims=True)
            rows.append(tot)
        out_ref[0, h] = jnp.concatenate(rows, axis=0) * (1.0 / MOBA_BLOCK)


def _page_means(cache_k, page_table, layer):
    db, n_pages = page_table.shape
    ppb = MOBA_BLOCK // PAGE_SIZE
    n_full = n_pages // ppb
    bps = PAGES_PER_STEP // ppb
    steps = n_full * ppb // PAGES_PER_STEP
    _, _, nh, pg, dh = cache_k.shape

    def page_spec(u):
        return pl.BlockSpec((None, None, nh, pg, dh),
                            lambda b, s, pt: (layer, pt[b, s * PAGES_PER_STEP + u], 0, 0, 0))

    return pl.pallas_call(
        _page_mean_kernel,
        grid_spec=pltpu.PrefetchScalarGridSpec(
            num_scalar_prefetch=1,
            grid=(db, steps),
            in_specs=[page_spec(u) for u in range(PAGES_PER_STEP)],
            out_specs=pl.BlockSpec((1, nh, bps, dh), lambda b, s, pt: (b, 0, s, 0)),
        ),
        out_shape=jax.ShapeDtypeStruct((db, nh, n_full, dh), F32),
        compiler_params=_cparams(("parallel", "arbitrary")),
        name="page_means",
    )(page_table, *([cache_k] * PAGES_PER_STEP))


def _sample_select_kernel(q_ref, km_ref, idx_ref):
    tp = q_ref.shape[2]
    nb = km_ref.shape[2]
    n_idx = lax.broadcasted_iota(I32, (tp, nb), 1)
    for h in range(N_HEADS_A):
        g = _dot3(q_ref[0, h], km_ref[0, h], 1, 1)
        cols = []
        for _ in range(MOBA_TOPK):
            mx = jnp.max(g, axis=1, keepdims=True)
            first = jnp.min(jnp.where(g == mx, n_idx, nb), axis=1, keepdims=True)
            cols.append(first)
            g = jnp.where(n_idx == first, -jnp.inf, g)
        idx_ref[0, h] = jnp.concatenate(cols, axis=1)


def _sample_select(q_s, kmean_s):
    db, nh, tp, dh = q_s.shape
    nb = kmean_s.shape[2]
    return pl.pallas_call(
        _sample_select_kernel,
        grid=(db,),
        in_specs=[pl.BlockSpec((1, nh, tp, dh), lambda b: (b, 0, 0, 0)),
                  pl.BlockSpec((1, nh, nb, dh), lambda b: (b, 0, 0, 0))],
        out_specs=pl.BlockSpec((1, nh, tp, MOBA_TOPK), lambda b: (b, 0, 0, 0)),
        out_shape=jax.ShapeDtypeStruct((db, nh, tp, MOBA_TOPK), I32),
        compiler_params=_cparams(("parallel",)),
        name="sample_select",
    )(q_s, kmean_s)


def _sample_attend_kernel(t_valid, layer, phys_ref, q_ref, kn_ref, vn_ref, ck_ref, cv_ref, o_ref,
                          kbuf, vbuf, sem):
    b = pl.program_id(0)
    t = pl.program_id(1)
    ppb = MOBA_BLOCK // PAGE_SIZE
    n_slab = MOBA_TOPK * ppb
    tp = q_ref.shape[2]

    def copies(h, u):
        page = phys_ref[((b * t_valid + t) * N_HEADS_A + h) * n_slab + u]
        return (pltpu.make_async_copy(ck_ref.at[layer, page, h], kbuf.at[h, u], sem.at[0]),
                pltpu.make_async_copy(cv_ref.at[layer, page, h], vbuf.at[h, u], sem.at[1]))

    for h in range(N_HEADS_A):
        for u in range(n_slab):
            ck, cv = copies(h, u)
            ck.start()
            cv.start()
    for h in range(N_HEADS_A):
        for u in range(n_slab):
            ck, cv = copies(h, u)
            ck.wait()
            cv.wait()

    row = lax.broadcasted_iota(I32, (tp, 1), 0)
    key = lax.broadcasted_iota(I32, (tp, tp), 1)
    for h in range(N_HEADS_A):
        qh = q_ref[0, h].astype(BF16)
        s_own = _dg(qh, kn_ref[0, h].astype(BF16), 1, 1)
        s_own = jnp.where((key <= t) & (key < t_valid), s_own, -jnp.inf)
        s_sel = [_dg(qh, kbuf[h, u].astype(BF16), 1, 1) for u in range(n_slab)]
        m = jnp.max(s_own, axis=1, keepdims=True)
        for sj in s_sel:
            m = jnp.maximum(m, jnp.max(sj, axis=1, keepdims=True))
        p_own = jnp.exp(s_own - m)
        l = jnp.sum(p_own, axis=1, keepdims=True)
        acc = _dg(p_own.astype(BF16), vn_ref[0, h].astype(BF16), 1, 0)
        for u, sj in enumerate(s_sel):
            pj = jnp.exp(sj - m)
            l = l + jnp.sum(pj, axis=1, keepdims=True)
            acc = acc + _dg(pj.astype(BF16), vbuf[h, u].astype(BF16), 1, 0)
        out = acc / l
        o_ref[0, 0, h] = jnp.sum(jnp.where(row == t, out, 0.0), axis=0, keepdims=True)


def _sample_attend(q_s, k_new, v_new, cache_k, cache_v, phys, t_valid, layer):
    db, nh, tp, dh = q_s.shape
    n_slab = MOBA_TOPK * (MOBA_BLOCK // PAGE_SIZE)
    hm = pl.BlockSpec((1, nh, tp, dh), lambda b, t, ph: (b, 0, 0, 0))
    return pl.pallas_call(
        functools.partial(_sample_attend_kernel, t_valid, layer),
        grid_spec=pltpu.PrefetchScalarGridSpec(
            num_scalar_prefetch=1,
            grid=(db, t_valid),
            in_specs=[hm, hm, hm, pl.BlockSpec(memory_space=pl.ANY), pl.BlockSpec(memory_space=pl.ANY)],
            out_specs=pl.BlockSpec((1, 1, nh, 1, dh), lambda b, t, ph: (b, t, 0, 0, 0)),
            scratch_shapes=[pltpu.VMEM((nh, n_slab, PAGE_SIZE, dh), F32),
                            pltpu.VMEM((nh, n_slab, PAGE_SIZE, dh), F32),
                            pltpu.SemaphoreType.DMA((2,))],
        ),
        out_shape=jax.ShapeDtypeStruct((db, t_valid, nh, 1, dh), F32),
        compiler_params=_cparams(("arbitrary", "arbitrary")),
        name="sample_attend",
    )(phys, q_s, k_new, v_new, cache_k, cache_v)


def _rwkv_feat_kernel(period, t_valid, cur_ref, prev_ref, mu_ref, w0_ref, dup_ref, a0_ref, aup_ref,
                      gup_ref, kk_ref, ka_ref, rk_ref, seg_ref,
                      r_o, lw_o, k_o, v_o, kk_o, b_o, g_o, bonus_o):
    cur = cur_ref[...]
    xs = cur + (prev_ref[...] - cur) * mu_ref[...]
    w = WIDTH_B
    r, k, v = xs[:, :w], xs[:, w:2 * w], xs[:, 2 * w:3 * w]
    dw = xs[:, 3 * w:3 * w + DECAY_LORA]
    da = xs[:, 3 * w + DECAY_LORA:3 * w + DECAY_LORA + AAA_LORA]
    dg = xs[:, 3 * w + DECAY_LORA + AAA_LORA:]
    lw = -DECAY_SCALE * jax.nn.sigmoid(w0_ref[...] + _dot3(jnp.tanh(dw), dup_ref[...]))
    a = jax.nn.sigmoid(a0_ref[...] + _dot3(da, aup_ref[...]))
    g = _dot3(jax.nn.sigmoid(dg), gup_ref[...])
    seg = seg_ref[...]
    kkr = k * kk_ref[...]
    kk = kkr / jnp.maximum(jnp.sqrt(_dot2_exact_rhs(kkr * kkr, seg)), 1e-12)
    k2 = k * (1.0 + (a - 1.0) * ka_ref[...])
    bonus = _dot2_exact_rhs(r * k2 * rk_ref[...], seg) * v
    if t_valid < period:
        rows = lax.broadcasted_iota(I32, (cur.shape[0], 1), 0)
        valid = (rows % period) < t_valid
        lw = jnp.where(valid, lw, 0.0)
        kk = jnp.where(valid, kk, 0.0)
        k2 = jnp.where(valid, k2, 0.0)
    r_o[...] = r
    lw_o[...] = lw
    k_o[...] = k2
    v_o[...] = v
    kk_o[...] = kk
    b_o[...] = kk * a
    g_o[...] = g
    bonus_o[...] = bonus


def _rwkv_features(p_cur, p_prev, prm, tm, period, t_valid):
    n = p_cur.shape[0]
    row = lambda v: v.reshape(1, -1).astype(F32)
    const = lambda i: (0, 0)
    params = [row(prm['rw_mu']), row(prm['rw_w0']), prm['rw_decay_up'], row(prm['rw_a0']), prm['rw_a_up'],
              prm['rw_g_up'], row(prm['rw_k_k']), row(prm['rw_k_a']), row(prm['rw_r_k']),
              _seg_ones(WIDTH_B, HEAD_DIM_B)]
    tok = pl.BlockSpec((tm, RWKV_COLS), lambda i: (i, 0))
    out = pl.BlockSpec((tm, WIDTH_B), lambda i: (i, 0))
    return pl.pallas_call(
        functools.partial(_rwkv_feat_kernel, period, t_valid),
        grid=(n // tm,),
        in_specs=[tok, tok] + [pl.BlockSpec(p.shape, const) for p in params],
        out_specs=[out] * 8,
        out_shape=[jax.ShapeDtypeStruct((n, WIDTH_B), F32)] * 8,
        compiler_params=_cparams(("parallel",)),
        name="rwkv_features",
    )(p_cur, p_prev, *params)


def _rwkv_chunk_kernel(r_ref, lw_ref, k_ref, v_ref, kk_ref, b_ref, g_ref, bonus_ref, s0_ref,
                       lnw_ref, lnb_ref, o_ref, sT_ref, st_sc):
    c = pl.program_id(1)
    L = r_ref.shape[1]
    nh, dh = N_HEADS_B, HEAD_DIM_B

    @pl.when(c == 0)
    def _():
        st_sc[...] = s0_ref[0]

    lw = lw_ref[0]
    ri = lax.broadcasted_iota(I32, (L, L), 0)
    ci = lax.broadcasted_iota(I32, (L, L), 1)
    strict = ri > ci
    incl = ri >= ci
    tri = jnp.where(incl, 1.0, 0.0).astype(BF16)
    cum = _dot2_exact_rhs_left(tri, lw)
    cum_last = cum[L - 1:L, :]
    e_pos = jnp.exp(cum)
    e_neg = jnp.exp(-cum)
    e_prev = jnp.exp(cum - lw)
    e_tail = jnp.exp(cum_last - cum)
    g_last = jnp.exp(cum_last)
    kk = kk_ref[0]
    alpha_all = kk * e_prev
    beta_all = b_ref[0] * e_neg
    kappa_all = k_ref[0] * e_neg
    rho_all = r_ref[0] * e_pos
    beta_t_all = b_ref[0] * e_tail
    kappa_t_all = k_ref[0] * e_tail
    v_all = v_ref[0]
    eye_l = jnp.where(ri == ci, 1.0, 0.0)
    rk = lax.broadcasted_iota(I32, (dh, dh), 0)
    ck = lax.broadcasted_iota(I32, (dh, dh), 1)

    ys = []
    for h in range(nh):
        sl = slice(h * dh, (h + 1) * dh)
        alpha, beta, kappa, rho, vh = alpha_all[:, sl], beta_all[:, sl], kappa_all[:, sl], rho_all[:, sl], v_all[:, sl]
        wcat = jnp.concatenate([beta, kappa], axis=0)
        za = _dot3(alpha, wcat, 1, 1)
        zr = _dot3(rho, wcat, 1, 1)
        n_mat = jnp.where(strict, za[:, :L], 0.0)
        m_mat = jnp.where(strict, za[:, L:], 0.0)
        nr = jnp.where(incl, zr[:, :L], 0.0)
        mr = jnp.where(incl, zr[:, L:], 0.0)
        d = eye_l - jnp.where((ri // 2 == ci // 2), n_mat, 0.0)
        s = 2
        while s < L:
            e_s = jnp.where((ri // (2 * s) == ci // (2 * s)) & ((ri % (2 * s)) >= s) & ((ci % (2 * s)) < s),
                            n_mat, 0.0)
            d = d - _dot3(_dot3(d, e_s), d)
            s *= 2
        mv = _dot3(m_mat, vh)
        ta = _dot3(d, jnp.concatenate([alpha, mv], axis=1))
        abar = ta[:, :dh]
        p0 = -ta[:, dh:]
        pv = jnp.concatenate([p0, vh], axis=0)
        rpp = rho - _dot3(nr, abar)
        y0 = _dot3(jnp.concatenate([nr, mr], axis=1), pv)
        beta_t, kappa_t = beta_t_all[:, sl], kappa_t_all[:, sl]
        gt = jnp.where(rk == ck, g_last[:, sl], 0.0) - _dot3(beta_t, abar, 0, 0)
        ht = _dot3(jnp.concatenate([beta_t, kappa_t], axis=0), pv, 0, 0)
        st = st_sc[h]
        upd = _dot3(jnp.concatenate([rpp, gt], axis=0), st)
        y = y0 + upd[:L]
        st_sc[h] = upd[L:] + ht
        mu = jnp.mean(y, axis=-1, keepdims=True)
        yc = y - mu
        var = jnp.mean(yc * yc, axis=-1, keepdims=True)
        ys.append(yc * lax.rsqrt(var + LN_X_EPS))
    yn = jnp.concatenate(ys, axis=1) * lnw_ref[...] + lnb_ref[...]
    o_ref[0] = ((yn + bonus_ref[0]) * g_ref[0]).astype(o_ref.dtype)
    sT_ref[0] = st_sc[...]


def _dot2_exact_rhs_left(m_bf16, x):
    xh, xl = _split(x)
    return _dg(m_bf16, xh, 1, 0) + _dg(m_bf16, xl, 1, 0)


def _rwkv_chunked(feats, s0_t, ln_w, ln_b, chunk):
    r, lw, k2, v, kk, b, g, bonus = feats
    bq, s, w = r.shape
    nc = s // chunk
    tok = pl.BlockSpec((1, chunk, w), lambda bi, c: (bi, c, 0))
    st = pl.BlockSpec((1, N_HEADS_B, HEAD_DIM_B, HEAD_DIM_B), lambda bi, c: (bi, 0, 0, 0))
    const = lambda bi, c: (0, 0)
    return pl.pallas_call(
        _rwkv_chunk_kernel,
        grid=(bq, nc),
        in_specs=[tok] * 8 + [st, pl.BlockSpec((1, w), const), pl.BlockSpec((1, w), const)],
        out_specs=[tok, st],
        out_shape=[jax.ShapeDtypeStruct((bq, s, w), BF16),
                   jax.ShapeDtypeStruct((bq, N_HEADS_B, HEAD_DIM_B, HEAD_DIM_B), F32)],
        scratch_shapes=[pltpu.VMEM((N_HEADS_B, HEAD_DIM_B, HEAD_DIM_B), F32)],
        compiler_params=_cparams(("parallel", "arbitrary")),
        name="rwkv_chunked",
    )(r, lw, k2, v, kk, b, g, bonus, s0_t, ln_w.reshape(1, w), ln_b.reshape(1, w))


def _mem_kv_kernel(mem_ref, g_ref, w_ref, kg_ref, mk_ref, mv_ref):
    h = _rms(mem_ref[0], g_ref[...]).astype(BF16)
    kv = _dg(h, w_ref[...], 1, 0)
    for hm in range(N_HEADS_M):
        sl = slice(hm * HEAD_DIM_M, (hm + 1) * HEAD_DIM_M)
        mk_ref[0, :, sl] = _rms(kv[:, sl], kg_ref[...])
    mv_ref[0] = kv[:, WIDTH_M:]


def _memory_kv(mem, norm_mem, w_mem_kv, k_norm_m):
    b, m, d = mem.shape
    const = lambda i: (0, 0)
    out = pl.BlockSpec((1, m, WIDTH_M), lambda i: (i, 0, 0))
    return pl.pallas_call(
        _mem_kv_kernel,
        grid=(b,),
        in_specs=[pl.BlockSpec((1, m, d), lambda i: (i, 0, 0)), pl.BlockSpec((1, d), const),
                  pl.BlockSpec((d, 2 * WIDTH_M), const), pl.BlockSpec((1, HEAD_DIM_M), const)],
        out_specs=[out, out],
        out_shape=[jax.ShapeDtypeStruct((b, m, WIDTH_M), F32)] * 2,
        compiler_params=_cparams(("parallel",)),
        name="memory_kv",
    )(mem, norm_mem.reshape(1, d), w_mem_kv.astype(BF16), k_norm_m.reshape(1, HEAD_DIM_M))


def _mem_attend_kernel(q_ref, mk_ref, mv_ref, g_ref, o_ref):
    q = q_ref[0]
    for hm in range(N_HEADS_M):
        sl = slice(hm * HEAD_DIM_M, (hm + 1) * HEAD_DIM_M)
        qh = (_rms(q[:, sl], g_ref[...]) * (HEAD_DIM_M ** -0.5)).astype(BF16)
        s = _dg(qh, mk_ref[0, :, sl].astype(BF16), 1, 1)
        p = jnp.exp(s - jnp.max(s, axis=1, keepdims=True))
        o = _dg(p.astype(BF16), mv_ref[0, :, sl].astype(BF16), 1, 0) / jnp.sum(p, axis=1, keepdims=True)
        o_ref[0, :, sl] = o.astype(o_ref.dtype)


def _memory_attend(q_m, mk, mv, q_norm_m, tq):
    b, s, w = q_m.shape
    m = mk.shape[1]
    kv = pl.BlockSpec((1, m, w), lambda bi, j: (bi, 0, 0))
    return pl.pallas_call(
        _mem_attend_kernel,
        grid=(b, s // tq),
        in_specs=[pl.BlockSpec((1, tq, w), lambda bi, j: (bi, j, 0)), kv, kv,
                  pl.BlockSpec((1, HEAD_DIM_M), lambda bi, j: (0, 0))],
        out_specs=pl.BlockSpec((1, tq, w), lambda bi, j: (bi, j, 0)),
        out_shape=jax.ShapeDtypeStruct((b, s, w), BF16),
        compiler_params=_cparams(("parallel", "parallel")),
        name="memory_attend",
    )(q_m, mk, mv, q_norm_m.reshape(1, HEAD_DIM_M))


def _merge_kernel(x_ref, oa_ref, ob_ref, om_ref, gl_ref, wb_ref, wo_ref, ng_ref, rw_ref, rb_ref,
                  x1_ref, h_ref, e_ref, gate_ref):
    d = x_ref.shape[1]
    gl = gl_ref[...]
    merged = jnp.zeros(x_ref.shape, F32)
    for n, o_ref in enumerate((oa_ref, ob_ref, om_ref)):
        y = _dg(o_ref[...], wb_ref[n], 1, 0)
        merged = merged + jax.nn.sigmoid(gl[:, n * d:(n + 1) * d]) * y
    x1 = x_ref[...] + _dg(merged.astype(BF16), wo_ref[...], 1, 0)
    x1_ref[...] = x1
    hn = _rms(x1, ng_ref[...])
    h_ref[...] = hn
    logits = _dot3(hn, rw_ref[...]) + rb_ref[...]
    tm, ne = logits.shape
    e_idx = lax.broadcasted_iota(I32, (tm, ne), 1)
    vals, idxs = [], []
    g = logits
    for _ in range(TOP_K):
        mx = jnp.max(g, axis=1, keepdims=True)
        first = jnp.min(jnp.where(g == mx, e_idx, ne), axis=1, keepdims=True)
        vals.append(mx)
        idxs.append(first)
        g = jnp.where(e_idx == first, -jnp.inf, g)
    top = jnp.concatenate(vals, axis=1)
    pe = jnp.exp(top - vals[0])
    gate_ref[...] = pe / jnp.sum(pe, axis=1, keepdims=True)
    e_ref[...] = jnp.concatenate(idxs, axis=1)


def _merge_and_route(x2, o_a, o_b, o_m, gl, prm, tm):
    n, d = x2.shape
    const2 = lambda i: (0, 0)
    tok = lambda wd: pl.BlockSpec((tm, wd), lambda i: (i, 0))
    return pl.pallas_call(
        _merge_kernel,
        grid=(n // tm,),
        in_specs=[tok(d), tok(512), tok(512), tok(512), tok(3 * d),
                  pl.BlockSpec((3, 512, d), lambda i: (0, 0, 0)), pl.BlockSpec((d, d), const2),
                  pl.BlockSpec((1, d), const2), pl.BlockSpec((d, N_EXPERTS), const2),
                  pl.BlockSpec((1, N_EXPERTS), const2)],
        out_specs=[tok(d), tok(d), tok(TOP_K), tok(TOP_K)],
        out_shape=[jax.ShapeDtypeStruct((n, d), F32), jax.ShapeDtypeStruct((n, d), F32),
                   jax.ShapeDtypeStruct((n, TOP_K), I32), jax.ShapeDtypeStruct((n, TOP_K), F32)],
        compiler_params=_cparams(("parallel",)),
        name="merge_route",
    )(x2, o_a, o_b, o_m, gl, prm['w_branch'].astype(BF16), prm['w_out'].astype(BF16),
      prm['norm_ffn'].reshape(1, d), prm['router_w'], prm['router_b'].reshape(1, N_EXPERTS))


def _onehots(e):
    tm = e.shape[0]
    e_idx = lax.broadcasted_iota(I32, (tm, N_EXPERTS), 1)
    return [jnp.where(e[:, k:k + 1] == e_idx, 1.0, 0.0) for k in range(TOP_K)]


def _moe_rank_kernel(e_ref, rank_ref, cnt_ref, base_sc):
    i = pl.program_id(0)

    @pl.when(i == 0)
    def _():
        base_sc[...] = jnp.zeros_like(base_sc)

    ohs = _onehots(e_ref[...])
    cnt = ohs[0] + ohs[1] + ohs[2] + ohs[3]
    tm = cnt.shape[0]
    ri = lax.broadcasted_iota(I32, (tm, tm), 0)
    ci = lax.broadcasted_iota(I32, (tm, tm), 1)
    tri = jnp.where(ri > ci, 1.0, 0.0).astype(BF16)
    tot = _dg(tri, cnt.astype(BF16), 1, 0) + base_sc[...]
    rank_ref[...] = jnp.concatenate([jnp.sum(oh * tot, axis=1, keepdims=True) for oh in ohs],
                                    axis=1).astype(I32)
    base_sc[...] = base_sc[...] + jnp.sum(cnt, axis=0, keepdims=True)
    cnt_ref[...] = base_sc[...].astype(I32)


def _moe_rank(top_e, tm):
    n = top_e.shape[0]
    return pl.pallas_call(
        _moe_rank_kernel,
        grid=(n // tm,),
        in_specs=[pl.BlockSpec((tm, TOP_K), lambda i: (i, 0))],
        out_specs=[pl.BlockSpec((tm, TOP_K), lambda i: (i, 0)), pl.BlockSpec((1, N_EXPERTS), lambda i: (0, 0))],
        out_shape=[jax.ShapeDtypeStruct((n, TOP_K), I32), jax.ShapeDtypeStruct((1, N_EXPERTS), I32)],
        scratch_shapes=[pltpu.VMEM((1, N_EXPERTS), F32)],
        compiler_params=_cparams(("arbitrary",)),
        name="moe_rank",
    )(top_e)


def _moe_rows_kernel(e_ref, rank_ref, start_ref, row_ref):
    ohs = _onehots(e_ref[...])
    st = start_ref[...].astype(F32)
    base = jnp.concatenate([jnp.sum(oh * st, axis=1, keepdims=True) for oh in ohs], axis=1)
    row_ref[...] = rank_ref[...] + base.astype(I32)


def _moe_rows(top_e, rank, starts, tm):
    n = top_e.shape[0]
    tok = pl.BlockSpec((tm, TOP_K), lambda i: (i, 0))
    return pl.pallas_call(
        _moe_rows_kernel,
        grid=(n // tm,),
        in_specs=[tok, tok, pl.BlockSpec((1, N_EXPERTS), lambda i: (0, 0))],
        out_specs=tok,
        out_shape=jax.ShapeDtypeStruct((n, TOP_K), I32),
        compiler_params=_cparams(("parallel",)),
        name="moe_rows",
    )(top_e, rank, starts)


def _dispatch_kernel(row_ref, h_ref, xs_in_ref, xs_ref, sem):
    del xs_in_ref
    tm = h_ref.shape[0]

    def copy(t, k):
        r = row_ref[t * TOP_K + k]
        return pltpu.make_async_copy(h_ref.at[pl.ds(t, 1)], xs_ref.at[pl.ds(r, 1)], sem)

    def start(t, carry):
        for k in range(TOP_K):
            copy(t, k).start()
        return carry

    def wait(t, carry):
        for k in range(TOP_K):
            copy(t, k).wait()
        return carry

    lax.fori_loop(0, tm, start, 0)
    lax.fori_loop(0, tm, wait, 0)


def _moe_dispatch(h, row_flat, n_rows, tm):
    n, d = h.shape
    xs0 = jnp.zeros((n_rows, d), F32)
    return pl.pallas_call(
        _dispatch_kernel,
        grid=(n // tm,),
        in_specs=[pl.BlockSpec((tm * TOP_K,), lambda i: (i,), memory_space=pltpu.SMEM),
                  pl.BlockSpec((tm, d), lambda i: (i, 0)),
                  pl.BlockSpec(memory_space=pl.ANY)],
        out_specs=pl.BlockSpec(memory_space=pl.ANY),
        out_shape=jax.ShapeDtypeStruct((n_rows, d), F32),
        scratch_shapes=[pltpu.SemaphoreType.DMA(())],
        input_output_aliases={2: 0},
        compiler_params=_cparams(("arbitrary",)),
        name="moe_dispatch",
    )(row_flat, h, xs0)


def _swiglu(u, d_ff):
    u_glu = jnp.minimum(u[:, :d_ff], SWIGLU_LIMIT)
    u_lin = jnp.clip(u[:, d_ff:], -SWIGLU_LIMIT, SWIGLU_LIMIT)
    return u_glu * jax.nn.sigmoid(SWIGLU_ALPHA * u_glu) * (u_lin + 1.0)


def _moe_ffn_kernel(be_ref, nu_ref, xs_ref, w1_ref, b1_ref, w2_ref, b2_ref, y_ref, w1_sc, w2_sc):
    i = pl.program_id(0)
    prev = be_ref[jnp.maximum(i - 1, 0)]
    first = (i == 0) | (be_ref[i] != prev)

    @pl.when(first)
    def _():
        w1_sc[...] = w1_ref[...].astype(BF16)
        w2_sc[...] = w2_ref[...].astype(BF16)

    @pl.when(i < nu_ref[0])
    def _():
        u = _dg(xs_ref[...].astype(BF16), w1_sc[...], 1, 0) + b1_ref[...]
        act = _swiglu(u, w2_ref.shape[0])
        y_ref[...] = _dg(act.astype(BF16), w2_sc[...], 1, 0) + b2_ref[...]

    @pl.when(i >= nu_ref[0])
    def _():
        y_ref[...] = jnp.zeros_like(y_ref)


def _moe_ffn(xs, blk_e, n_used, w1, b1, w2, b2):
    n_rows, d = xs.shape
    ne, _, f2 = w1.shape
    d_ff = w2.shape[1]
    nblk = n_rows // MOE_ROWS
    return pl.pallas_call(
        _moe_ffn_kernel,
        grid_spec=pltpu.PrefetchScalarGridSpec(
            num_scalar_prefetch=2,
            grid=(nblk,),
            in_specs=[pl.BlockSpec((MOE_ROWS, d), lambda i, be, nu: (i, 0)),
                      pl.BlockSpec((None, d, f2), lambda i, be, nu: (be[i], 0, 0)),
                      pl.BlockSpec((None, 1, f2), lambda i, be, nu: (be[i], 0, 0)),
                      pl.BlockSpec((None, d_ff, d), lambda i, be, nu: (be[i], 0, 0)),
                      pl.BlockSpec((None, 1, d), lambda i, be, nu: (be[i], 0, 0))],
            out_specs=pl.BlockSpec((MOE_ROWS, d), lambda i, be, nu: (i, 0)),
            scratch_shapes=[pltpu.VMEM((d, f2), BF16), pltpu.VMEM((d_ff, d), BF16)],
        ),
        out_shape=jax.ShapeDtypeStruct((n_rows, d), F32),
        compiler_params=_cparams(("arbitrary",)),
        name="moe_ffn",
    )(blk_e, n_used, xs, w1, b1.reshape(ne, 1, f2), w2, b2.reshape(ne, 1, d))


def _combine_kernel(row_ref, yb_ref, x1_ref, gate_ref, y_ref, buf, sem):
    tm = x1_ref.shape[0]

    def copy(t, k):
        r = row_ref[t * TOP_K + k]
        return pltpu.make_async_copy(yb_ref.at[pl.ds(r, 1)], buf.at[k, pl.ds(t, 1)], sem)

    def start(t, carry):
        for k in range(TOP_K):
            copy(t, k).start()
        return carry

    def wait(t, carry):
        for k in range(TOP_K):
            copy(t, k).wait()
        return carry

    lax.fori_loop(0, tm, start, 0)
    lax.fori_loop(0, tm, wait, 0)
    gates = gate_ref[...]
    acc = buf[0] * gates[:, 0:1]
    for k in range(1, TOP_K):
        acc = acc + buf[k] * gates[:, k:k + 1]
    y_ref[...] = x1_ref[...] + acc


def _moe_combine(yb, row_flat, x1, gates, tm):
    n, d = x1.shape
    return pl.pallas_call(
        _combine_kernel,
        grid=(n // tm,),
        in_specs=[pl.BlockSpec((tm * TOP_K,), lambda i: (i,), memory_space=pltpu.SMEM),
                  pl.BlockSpec(memory_space=pl.ANY),
                  pl.BlockSpec((tm, d), lambda i: (i, 0)),
                  pl.BlockSpec((tm, TOP_K), lambda i: (i, 0))],
        out_specs=pl.BlockSpec((tm, d), lambda i: (i, 0)),
        out_shape=jax.ShapeDtypeStruct((n, d), F32),
        scratch_shapes=[pltpu.VMEM((TOP_K, tm, d), F32), pltpu.SemaphoreType.DMA(())],
        compiler_params=_cparams(("arbitrary",)),
        name="moe_combine",
    )(row_flat, yb, x1, gates)


def _moe_block(x1, hn, top_e, gates, prm, tm):
    n, d = x1.shape
    rank, counts = _moe_rank(top_e, tm)
    counts = counts[0]
    padded = (counts + MOE_ROWS - 1) // MOE_ROWS * MOE_ROWS
    p_end = jnp.cumsum(padded)
    starts = (p_end - padded).astype(I32)
    nblk = -(-(n * TOP_K + N_EXPERTS * (MOE_ROWS - 1)) // MOE_ROWS)
    n_used = (p_end[-1] // MOE_ROWS).astype(I32)
    blk_i = jnp.minimum(jnp.arange(nblk, dtype=I32), n_used - 1)
    n_before = jnp.sum((p_end[None, :] <= (blk_i * MOE_ROWS)[:, None]).astype(I32), axis=1)
    blk_e = jnp.minimum(n_before, N_EXPERTS - 1).astype(I32)
    row = _moe_rows(top_e, rank, starts[None, :], tm)
    row_flat = row.reshape(-1)
    xs = _moe_dispatch(hn, row_flat, nblk * MOE_ROWS, tm)
    yb = _moe_ffn(xs, blk_e, n_used[None], prm['moe_w1'], prm['moe_b1'], prm['moe_w2'], prm['moe_b2'])
    return _moe_combine(yb, row_flat, x1, gates, tm)


def _split_w_in(w_in):
    d = w_in.shape[0]
    a = 3 * WIDTH_A
    b = a + RWKV_COLS
    c = b + WIDTH_M
    wb = w_in.astype(BF16)
    return [wb[:, :a], wb[:, a:b], wb[:, b:c], wb[:, c:]]


def _token_mix_tail(x2, o_a, o_b, o_m, gl, prm, tm):
    x1, hn, top_e, gates = _merge_and_route(x2, o_a, o_b, o_m, gl, prm, tm)
    return _moe_block(x1, hn, top_e, gates, prm, tm)


def _layer_prompt(x, mem, prm, ws):
    b, s, d = x.shape
    tm = 256
    x2 = x.reshape(b * s, d)
    qkv, p_rw, q_m, gl = _in_projection(x2, prm['norm_mix'].reshape(1, d), ws, tm)
    pos = jnp.arange(s, dtype=I32)
    q_s, k_o, v_o, kaug, vb, kmean = _moba_prep(qkv.reshape(b, s, -1), pos, prm['q_norm_a'], prm['k_norm_a'],
                                                MOBA_BLOCK, True)
    o_a = _moba_flash(q_s, kaug, vb, kmean).reshape(b * s, WIDTH_A)

    p3 = p_rw.reshape(b, s, RWKV_COLS)
    p_prev = jnp.concatenate([jnp.zeros_like(p3[:, :1]), p3[:, :-1]], axis=1).reshape(b * s, RWKV_COLS)
    feats = _rwkv_features(p_rw, p_prev, prm, tm, 1, 1)
    feats = [f.reshape(b, s, WIDTH_B) for f in feats]
    s0_t = jnp.zeros((b, N_HEADS_B, HEAD_DIM_B, HEAD_DIM_B), F32)
    o_b, st = _rwkv_chunked(feats, s0_t, prm['ln_x_w'], prm['ln_x_b'], RWKV_CHUNK)
    wkv = jnp.swapaxes(st, 2, 3)

    mk, mv = _memory_kv(mem, prm['norm_mem'], prm['w_mem_kv'], prm['k_norm_m'])
    o_m = _memory_attend(q_m.reshape(b, s, WIDTH_M), mk, mv, prm['q_norm_m'], 512)

    y = _token_mix_tail(x2, o_a, o_b.reshape(b * s, WIDTH_B), o_m.reshape(b * s, WIDTH_M), gl, prm, tm)
    m = mem.shape[1]
    return (y.reshape(b, s, d), k_o, v_o, wkv, p3[:, -1],
            mk.reshape(b, m, N_HEADS_M, HEAD_DIM_M), mv.reshape(b, m, N_HEADS_M, HEAD_DIM_M))


def _layer_sample(x, cache_k, cache_v, mem_k, mem_v, wkv0, shift0, page_table, layer, prm, ws):
    db, t, d = x.shape
    tp = SAMPLE_T_PAD
    past_len = page_table.shape[1] * PAGE_SIZE
    assert past_len % MOBA_BLOCK == 0 and t <= tp
    ppb = MOBA_BLOCK // PAGE_SIZE
    n = db * tp
    x2 = jnp.pad(x, ((0, 0), (0, tp - t), (0, 0))).reshape(n, d)
    qkv, p_rw, q_m, gl = _in_projection(x2, prm['norm_mix'].reshape(1, d), ws, n)
    pos = past_len + jnp.arange(tp, dtype=I32)
    q_s, k_o, v_o = _moba_prep(qkv.reshape(db, tp, -1), pos, prm['q_norm_a'], prm['k_norm_a'], tp, False)

    kmean_s = _page_means(cache_k, page_table, layer)
    idx = _sample_select(q_s, kmean_s)
    idx = jnp.transpose(idx[:, :, :t], (0, 2, 1, 3))
    logical = idx[..., None] * ppb + jnp.arange(ppb, dtype=I32)
    phys = page_table[jnp.arange(db)[:, None, None, None, None], logical].reshape(-1).astype(I32)
    o_a = _sample_attend(q_s, k_o, v_o, cache_k, cache_v, phys, t, layer)
    o_a = jnp.pad(o_a.reshape(db, t, WIDTH_A), ((0, 0), (0, tp - t), (0, 0))).reshape(n, WIDTH_A).astype(BF16)

    p3 = p_rw.reshape(db, tp, RWKV_COLS)
    p_prev = jnp.concatenate([shift0[:, None, :], p3[:, :-1]], axis=1).reshape(n, RWKV_COLS)
    feats = _rwkv_features(p_rw, p_prev, prm, n, tp, t)
    feats = [f.reshape(db, tp, WIDTH_B) for f in feats]
    o_b, st = _rwkv_chunked(feats, jnp.swapaxes(wkv0, 2, 3), prm['ln_x_w'], prm['ln_x_b'], tp)
    wkv = jnp.swapaxes(st, 2, 3)

    m = mem_k.shape[1]
    o_m = _memory_attend(q_m.reshape(db, tp, WIDTH_M), mem_k.reshape(db, m, WIDTH_M),
                         mem_v.reshape(db, m, WIDTH_M), prm['q_norm_m'], tp)

    y = _token_mix_tail(x2, o_a, o_b.reshape(n, WIDTH_B), o_m.reshape(n, WIDTH_M), gl, prm, n)
    return (y.reshape(db, tp, d)[:, :t], k_o[:, :, :t], v_o[:, :, :t], wkv, p3[:, t - 1])


def kernel(x_prompt, x_sample, mem_prompt, cache_k, cache_v, cache_mem_k, cache_mem_v, state_wkv, state_shift,
           page_table, norm_mix, norm_mem, norm_ffn, w_in, q_norm_a, k_norm_a, q_norm_m, k_norm_m, w_mem_kv,
           rw_mu, rw_w0, rw_decay_up, rw_a0, rw_a_up, rw_g_up, rw_k_k, rw_k_a, rw_r_k, ln_x_w, ln_x_b,
           w_branch, w_out, router_w, router_b, moe_w1, moe_b1, moe_w2, moe_b2):
    depth = w_in.shape[0]
    xp, xs = x_prompt, x_sample
    outs_p = [[] for _ in range(6)]
    outs_s = [[] for _ in range(4)]
    for l in range(depth):
        prm = dict(norm_mix=norm_mix[l], norm_mem=norm_mem[l], norm_ffn=norm_ffn[l], w_in=w_in[l],
                   q_norm_a=q_norm_a[l], k_norm_a=k_norm_a[l], q_norm_m=q_norm_m[l], k_norm_m=k_norm_m[l],
                   w_mem_kv=w_mem_kv[l], rw_mu=rw_mu[l], rw_w0=rw_w0[l], rw_decay_up=rw_decay_up[l],
                   rw_a0=rw_a0[l], rw_a_up=rw_a_up[l], rw_g_up=rw_g_up[l], rw_k_k=rw_k_k[l], rw_k_a=rw_k_a[l],
                   rw_r_k=rw_r_k[l].reshape(-1), ln_x_w=ln_x_w[l], ln_x_b=ln_x_b[l], w_branch=w_branch[l],
                   w_out=w_out[l], router_w=router_w[l], router_b=router_b[l], moe_w1=moe_w1[l],
                   moe_b1=moe_b1[l], moe_w2=moe_w2[l], moe_b2=moe_b2[l])
        ws = _split_w_in(prm['w_in'])
        xp, *rest_p = _layer_prompt(xp, mem_prompt, prm, ws)
        for acc, val in zip(outs_p, rest_p):
            acc.append(val)
        xs, *rest_s = _layer_sample(xs, cache_k, cache_v, cache_mem_k[l], cache_mem_v[l], state_wkv[l],
                                    state_shift[l], page_table, l, prm, ws)
        for acc, val in zip(outs_s, rest_s):
            acc.append(val)
    kp, vp, wkvp, shp, mkp, mvp = [jnp.stack(a) for a in outs_p]
    ksm, vsm, wkvs, shs = [jnp.stack(a) for a in outs_s]
    return (xp, xs, kp, vp, wkvp, shp, mkp, mvp, ksm, vsm, wkvs, shs)
```

```python
import functools
import math

import jax
import jax.numpy as jnp
from jax import lax
from jax.experimental import pallas as pl
from jax.experimental.pallas import tpu as pltpu

F32 = jnp.float32
BF16 = jnp.bfloat16
I32 = jnp.int32

N_HEADS_A = 8
HEAD_DIM_A = 64
WIDTH_A = 512
MOBA_BLOCK = 256
MOBA_TOPK = 3
ROT_DIM = 16
ROPE_THETA = 500000.0
PAGE_SIZE = 128
N_HEADS_B = 8
HEAD_DIM_B = 64
WIDTH_B = 512
DECAY_LORA = 64
AAA_LORA = 64
GATE_LORA = 128
RWKV_COLS = 1792
DECAY_SCALE = math.exp(-0.5)
LN_X_EPS = 64e-5
N_HEADS_M = 4
HEAD_DIM_M = 128
WIDTH_M = 512
N_EXPERTS = 32
TOP_K = 4
SWIGLU_ALPHA = 1.702
SWIGLU_LIMIT = 7.0
NORM_EPS = 1e-6

NEG_BIG = -1e30
SAMPLE_T_PAD = 8
RWKV_CHUNK = 64
MOE_ROWS = 256
VMEM_LIMIT = 56 * 1024 * 1024


def _cparams(sem, vmem=None):
    return pltpu.CompilerParams(dimension_semantics=sem, vmem_limit_bytes=vmem or VMEM_LIMIT)


def _dg(a, b, ca, cb):
    return lax.dot_general(a, b, (((ca,), (cb,)), ((), ())), preferred_element_type=F32)


def _split(x):
    hi = x.astype(BF16)
    lo = (x - hi.astype(F32)).astype(BF16)
    return hi, lo


def _dot3(a, b, ca=1, cb=0):
    ah, al = _split(a)
    bh, bl = _split(b)
    return _dg(ah, bh, ca, cb) + (_dg(ah, bl, ca, cb) + _dg(al, bh, ca, cb))


def _dot2_exact_rhs(a, b_bf16):
    ah, al = _split(a)
    return _dg(ah, b_bf16, 1, 0) + _dg(al, b_bf16, 1, 0)


def _dotb(a, b, ca=1, cb=0):
    return _dg(a.astype(BF16), b.astype(BF16), ca, cb)


def _rms(x, gain_row):
    ms = jnp.mean(x * x, axis=-1, keepdims=True)
    return x * lax.rsqrt(ms + NORM_EPS) * gain_row


def _seg_ones(width, seg):
    r = lax.broadcasted_iota(I32, (width, width), 0) // seg
    c = lax.broadcasted_iota(I32, (width, width), 1) // seg
    return jnp.where(r == c, 1.0, 0.0).astype(BF16)


def _proj_kernel(x_ref, g_ref, w1, w2, w3, w4, o1, o2, o3, o4):
    h = _rms(x_ref[...], g_ref[...]).astype(BF16)
    o1[...] = _dg(h, w1[...], 1, 0)
    o2[...] = _dg(h, w2[...], 1, 0)
    o3[...] = _dg(h, w3[...], 1, 0)
    o4[...] = _dg(h, w4[...], 1, 0)


def _in_projection(x2, gain, ws, tm):
    n, d = x2.shape
    widths = [w.shape[1] for w in ws]
    const = lambda i: (0, 0)
    return pl.pallas_call(
        _proj_kernel,
        grid=(n // tm,),
        in_specs=[pl.BlockSpec((tm, d), lambda i: (i, 0)), pl.BlockSpec((1, d), const)]
        + [pl.BlockSpec((d, wd), const) for wd in widths],
        out_specs=[pl.BlockSpec((tm, wd), lambda i: (i, 0)) for wd in widths],
        out_shape=[jax.ShapeDtypeStruct((n, wd), F32) for wd in widths],
        compiler_params=_cparams(("parallel",)),
        name="in_projection",
    )(x2, gain, *ws)


def _rope_tables(pos):
    half = ROT_DIM // 2
    inv_freq = 1.0 / (ROPE_THETA ** (jnp.arange(0, ROT_DIM, 2, dtype=F32) / ROT_DIM))
    ang = pos.astype(F32)[:, None] * inv_freq[None, :]
    cos, sin = jnp.cos(ang), jnp.sin(ang)
    n = pos.shape[0]
    rest = HEAD_DIM_A - ROT_DIM
    c = jnp.concatenate([cos, cos, jnp.ones((n, rest), F32)], axis=1)
    s_up = jnp.concatenate([-sin, jnp.zeros((n, half + rest), F32)], axis=1)
    s_dn = jnp.concatenate([jnp.zeros((n, half), F32), sin, jnp.zeros((n, rest), F32)], axis=1)
    two = lambda t: jnp.concatenate([t, t], axis=1)
    return two(c), two(s_up), two(s_dn)


def _norm_rope(x, seg, gain, c, s_up, s_dn):
    ss = _dot2_exact_rhs(x * x, seg)
    y = x * lax.rsqrt(ss * (1.0 / HEAD_DIM_A) + NORM_EPS) * gain
    half = ROT_DIM // 2
    up = pltpu.roll(y, WIDTH_A - half, 1)
    dn = pltpu.roll(y, half, 1)
    return y * c + up * s_up + dn * s_dn


def _moba_prep_kernel(with_blocks, qkv_ref, seg_ref, qg_ref, kg_ref, c_ref, su_ref, sd_ref, *outs):
    if with_blocks:
        qs_ref, kt_ref, vt_ref, kaug_ref, vaug_ref, kmean_ref = outs
    else:
        qs_ref, k_ref, v_ref = outs
    x = qkv_ref[0]
    tm = x.shape[0]
    rep = lambda r: jnp.concatenate([r[...]] * (WIDTH_A // 128), axis=1)
    c, su, sd = rep(c_ref), rep(su_ref), rep(sd_ref)
    seg = seg_ref[...]
    q = _norm_rope(x[:, :WIDTH_A], seg, qg_ref[...], c, su, sd) * (HEAD_DIM_A ** -0.5)
    k = _norm_rope(x[:, WIDTH_A:2 * WIDTH_A], seg, kg_ref[...], c, su, sd)
    v = x[:, 2 * WIDTH_A:]
    if with_blocks:
        blk = pl.program_id(1)
        lane = lax.broadcasted_iota(I32, (tm, HEAD_DIM_A), 1)
        onehot = jnp.where(lane == blk, 1.0, 0.0).astype(BF16)
        ones_col = jnp.where(lane == 0, 1.0, 0.0).astype(BF16)

        @pl.when(blk == 0)
        def _():
            kmean_ref[...] = jnp.zeros_like(kmean_ref)

    for h in range(N_HEADS_A):
        sl = slice(h * HEAD_DIM_A, (h + 1) * HEAD_DIM_A)
        qs_ref[0, h] = q[:, sl]
        if with_blocks:
            kt_ref[0, h] = k[:, sl].T
            vt_ref[0, h] = v[:, sl].T
            kaug_ref[0, h] = jnp.concatenate([k[:, sl].astype(BF16), onehot], axis=1)
            vaug_ref[0, h] = jnp.concatenate([v[:, sl].astype(BF16), ones_col], axis=1)
            kmean_ref[0, h, pl.ds(blk, 1), :] = jnp.mean(k[:, sl], axis=0, keepdims=True)
        else:
            k_ref[0, h] = k[:, sl]
            v_ref[0, h] = v[:, sl]


def _moba_prep(qkv, pos, q_gain, k_gain, tm, with_blocks):
    bq, s, _ = qkv.shape
    nb = s // tm
    c, su, sd = _rope_tables(pos)
    seg = _seg_ones(WIDTH_A, HEAD_DIM_A)
    tile8 = lambda g: jnp.tile(g.astype(F32), N_HEADS_A)[None, :]
    hm = jax.ShapeDtypeStruct((bq, N_HEADS_A, s, HEAD_DIM_A), F32)
    hm_spec = pl.BlockSpec((1, N_HEADS_A, tm, HEAD_DIM_A), lambda b, j: (b, 0, j, 0))
    if with_blocks:
        assert nb <= HEAD_DIM_A
        tr = jax.ShapeDtypeStruct((bq, N_HEADS_A, HEAD_DIM_A, s), F32)
        tr_spec = pl.BlockSpec((1, N_HEADS_A, HEAD_DIM_A, tm), lambda b, j: (b, 0, 0, j))
        aug = jax.ShapeDtypeStruct((bq, N_HEADS_A, s, 128), BF16)
        aug_spec = pl.BlockSpec((1, N_HEADS_A, tm, 128), lambda b, j: (b, 0, j, 0))
        out_shape = [hm, tr, tr, aug, aug,
                     jax.ShapeDtypeStruct((bq, N_HEADS_A, HEAD_DIM_A, HEAD_DIM_A), F32)]
        out_specs = [hm_spec, tr_spec, tr_spec, aug_spec, aug_spec,
                     pl.BlockSpec((1, N_HEADS_A, HEAD_DIM_A, HEAD_DIM_A), lambda b, j: (b, 0, 0, 0))]
    else:
        out_shape = [hm, hm, hm]
        out_specs = [hm_spec, hm_spec, hm_spec]
    const = lambda b, j: (0, 0)
    tab = pl.BlockSpec((tm, 128), lambda b, j: (j, 0))
    return pl.pallas_call(
        functools.partial(_moba_prep_kernel, with_blocks),
        grid=(bq, nb),
        in_specs=[pl.BlockSpec((1, tm, 3 * WIDTH_A), lambda b, j: (b, j, 0)),
                  pl.BlockSpec((WIDTH_A, WIDTH_A), const),
                  pl.BlockSpec((1, WIDTH_A), const), pl.BlockSpec((1, WIDTH_A), const), tab, tab, tab],
        out_specs=out_specs,
        out_shape=out_shape,
        compiler_params=_cparams(("parallel", "arbitrary")),
        name="moba_prep",
    )(qkv, seg, tile8(q_gain), tile8(k_gain), c, su, sd)


def _moba_flash_kernel(q_ref, kaug_ref, vaug_ref, kmean_ref, o_ref, m_sc, acc_sc):
    i = pl.program_id(1)
    h = pl.program_id(2)
    tq, dh = q_ref.shape[2], q_ref.shape[3]
    q = q_ref[0, 0]
    gate_t = _dot3(kmean_ref[0, 0], q, 1, 1)
    n_idx = lax.broadcasted_iota(I32, gate_t.shape, 0)
    n_tot = gate_t.shape[0]
    g = jnp.where(n_idx < i, gate_t, -jnp.inf)
    sel = n_idx == i
    for _ in range(MOBA_TOPK):
        mx = jnp.max(g, axis=0, keepdims=True)
        cand = (g == mx) & (mx > -jnp.inf)
        first = jnp.min(jnp.where(cand, n_idx, n_tot), axis=0, keepdims=True)
        pick = n_idx == first
        sel = sel | pick
        g = jnp.where(pick, -jnp.inf, g)
    bias_t = jnp.where(sel, 0.0, NEG_BIG)
    bias = jnp.concatenate([jnp.zeros((dh, tq), F32), bias_t], axis=0).T
    qaug = (jnp.concatenate([q, jnp.zeros((tq, dh), F32)], axis=1) + bias).astype(BF16)

    start = pl.multiple_of(i * tq, tq)
    s = _dg(qaug, kaug_ref[0, 0, pl.ds(start, tq), :], 1, 1)
    row = lax.broadcasted_iota(I32, (tq, tq), 0)
    col = lax.broadcasted_iota(I32, (tq, tq), 1)
    s = jnp.where(col <= row, s, -jnp.inf)
    m0 = jnp.max(s, axis=1, keepdims=True)
    m_sc[...] = jnp.broadcast_to(m0, m_sc.shape)
    acc_sc[...] = _dg(jnp.exp(s - m0).astype(BF16), vaug_ref[0, 0, pl.ds(start, tq), :], 1, 0)

    def step(off, width):
        sj = _dg(qaug, kaug_ref[0, 0, pl.ds(off, width), :], 1, 1)
        m_old = m_sc[...]
        m_new = jnp.maximum(m_old, jnp.max(sj, axis=1, keepdims=True))
        pj = jnp.exp(sj - jnp.concatenate([m_new] * (width // 128), axis=1))
        acc_sc[...] = (jnp.exp(m_old - m_new) * acc_sc[...]
                       + _dg(pj.astype(BF16), vaug_ref[0, 0, pl.ds(off, width), :], 1, 0))
        m_sc[...] = m_new

    def quad(j, carry):
        step(pl.multiple_of(j * (4 * tq), 4 * tq), 4 * tq)
        return carry

    lax.fori_loop(0, i // 4, quad, 0)

    @pl.when(i % 4 >= 2)
    def _():
        step(pl.multiple_of((i // 4) * (4 * tq), 2 * tq), 2 * tq)

    @pl.when(i % 2 == 1)
    def _():
        step(pl.multiple_of((i - 1) * tq, tq), tq)

    acc = acc_sc[...]
    out = (acc[:, :dh] / acc[:, dh:dh + 1]).astype(o_ref.dtype)
    for hh in range(N_HEADS_A):
        @pl.when(h == hh)
        def _():
            o_ref[0, :, hh * dh:(hh + 1) * dh] = out


def _moba_flash(q_s, kaug, vaug, kmean):
    b, nh, s, dh = q_s.shape
    tq = MOBA_BLOCK
    nb = s // tq
    return pl.pallas_call(
        _moba_flash_kernel,
        grid=(b, nb, nh),
        in_specs=[pl.BlockSpec((1, 1, tq, dh), lambda bi, i, h: (bi, h, i, 0)),
                  pl.BlockSpec((1, 1, s, 128), lambda bi, i, h: (bi, h, 0, 0)),
                  pl.BlockSpec((1, 1, s, 128), lambda bi, i, h: (bi, h, 0, 0)),
                  pl.BlockSpec((1, 1, dh, dh), lambda bi, i, h: (bi, h, 0, 0))],
        out_specs=pl.BlockSpec((1, tq, nh * dh), lambda bi, i, h: (bi, i, 0)),
        out_shape=jax.ShapeDtypeStruct((b, s, nh * dh), BF16),
        scratch_shapes=[pltpu.VMEM((tq, 128), F32), pltpu.VMEM((tq, 128), F32)],
        compiler_params=_cparams(("parallel", "parallel", "arbitrary")),
        name="moba_flash",
    )(q_s, kaug, vaug, kmean)


PAGES_PER_STEP = 16


def _page_mean_kernel(pt_ref, *refs):
    pages, out_ref = refs[:PAGES_PER_STEP], refs[PAGES_PER_STEP]
    s = pl.program_id(1)
    ppb = MOBA_BLOCK // PAGE_SIZE
    bps = PAGES_PER_STEP // ppb

    @pl.when(s == 0)
    def _():
        out_ref[...] = jnp.zeros_like(out_ref)

    lane = lax.broadcasted_iota(I32, out_ref.shape[2:], 1)
    for h in range(N_HEADS_A):
        acc = out_ref[0, h]
        for j in range(bps):
            tot = pages[ppb * j][h]
            for u in range(1, ppb):
                tot = tot + pages[ppb * j + u][h]
            col = jnp.sum(tot, axis=1, keepdims=True) * (1.0 / MOBA_BLOCK)
            acc = jnp.where(lane == s * bps + j, col, acc)
        out_ref[0, h] = acc


def _page_means(cache_kt, page_table, layer):
    db, n_pages = page_table.shape
    ppb = MOBA_BLOCK // PAGE_SIZE
    n_full = n_pages // ppb
    steps = n_full * ppb // PAGES_PER_STEP
    _, _, nh, dh, pg = cache_kt.shape

    def page_spec(u):
        return pl.BlockSpec((None, None, nh, dh, pg),
                            lambda b, s, pt: (layer, pt[b, s * PAGES_PER_STEP + u], 0, 0, 0))

    return pl.pallas_call(
        _page_mean_kernel,
        grid_spec=pltpu.PrefetchScalarGridSpec(
            num_scalar_prefetch=1,
            grid=(db, steps),
            in_specs=[page_spec(u) for u in range(PAGES_PER_STEP)],
            out_specs=pl.BlockSpec((1, nh, dh, n_full), lambda b, s, pt: (b, 0, 0, 0)),
        ),
        out_shape=jax.ShapeDtypeStruct((db, nh, dh, n_full), F32),
        compiler_params=_cparams(("parallel", "arbitrary")),
        name="page_means",
    )(page_table, *([cache_kt] * PAGES_PER_STEP))


def _sample_select_kernel(q_ref, km_ref, idx_ref):
    tp = q_ref.shape[2]
    nb = km_ref.shape[3]
    n_idx = lax.broadcasted_iota(I32, (tp, nb), 1)
    for h in range(N_HEADS_A):
        g = _dot3(q_ref[0, h], km_ref[0, h])
        cols = []
        for _ in range(MOBA_TOPK):
            mx = jnp.max(g, axis=1, keepdims=True)
            first = jnp.min(jnp.where(g == mx, n_idx, nb), axis=1, keepdims=True)
            cols.append(first)
            g = jnp.where(n_idx == first, -jnp.inf, g)
        idx_ref[0, h] = jnp.concatenate(cols, axis=1)


def _sample_select(q_s, kmean_t):
    db, nh, tp, dh = q_s.shape
    nb = kmean_t.shape[3]
    return pl.pallas_call(
        _sample_select_kernel,
        grid=(db,),
        in_specs=[pl.BlockSpec((1, nh, tp, dh), lambda b: (b, 0, 0, 0)),
                  pl.BlockSpec((1, nh, dh, nb), lambda b: (b, 0, 0, 0))],
        out_specs=pl.BlockSpec((1, nh, tp, MOBA_TOPK), lambda b: (b, 0, 0, 0)),
        out_shape=jax.ShapeDtypeStruct((db, nh, tp, MOBA_TOPK), I32),
        compiler_params=_cparams(("parallel",)),
        name="sample_select",
    )(q_s, kmean_t)


def _sample_attend_kernel(t_valid, layer, phys_ref, q_ref, kn_ref, vn_ref, ck_ref, cv_ref, o_ref,
                          kbuf, vbuf, sem):
    b = pl.program_id(0)
    t = pl.program_id(1)
    ppb = MOBA_BLOCK // PAGE_SIZE
    n_slab = MOBA_TOPK * ppb
    tp = q_ref.shape[2]

    def copies(h, u):
        page = phys_ref[((b * t_valid + t) * N_HEADS_A + h) * n_slab + u]
        return (pltpu.make_async_copy(ck_ref.at[layer, page, h], kbuf.at[h, u], sem.at[0]),
                pltpu.make_async_copy(cv_ref.at[layer, page, h], vbuf.at[h, u], sem.at[1]))

    for h in range(N_HEADS_A):
        for u in range(n_slab):
            ck, cv = copies(h, u)
            ck.start()
            cv.start()
    for h in range(N_HEADS_A):
        for u in range(n_slab):
            ck, cv = copies(h, u)
            ck.wait()
            cv.wait()

    row = lax.broadcasted_iota(I32, (tp, 1), 0)
    key = lax.broadcasted_iota(I32, (tp, tp), 1)
    for h in range(N_HEADS_A):
        qh = q_ref[0, h].astype(BF16)
        s_own = _dg(qh, kn_ref[0, h].astype(BF16), 1, 1)
        s_own = jnp.where((key <= t) & (key < t_valid), s_own, -jnp.inf)
        s_sel = [_dg(qh, kbuf[h, u].astype(BF16), 1, 0) for u in range(n_slab)]
        m = jnp.max(s_own, axis=1, keepdims=True)
        for sj in s_sel:
            m = jnp.maximum(m, jnp.max(sj, axis=1, keepdims=True))
        p_own = jnp.exp(s_own - m)
        l = jnp.sum(p_own, axis=1, keepdims=True)
        acc = _dg(p_own.astype(BF16), vn_ref[0, h].astype(BF16), 1, 0)
        for u, sj in enumerate(s_sel):
            pj = jnp.exp(sj - m)
            l = l + jnp.sum(pj, axis=1, keepdims=True)
            acc = acc + _dg(pj.astype(BF16), vbuf[h, u].astype(BF16), 1, 1)
        out = acc / l
        o_ref[0, 0, h] = jnp.sum(jnp.where(row == t, out, 0.0), axis=0, keepdims=True)


def _sample_attend(q_s, k_new, v_new, cache_kt, cache_vt, phys, t_valid, layer):
    db, nh, tp, dh = q_s.shape
    n_slab = MOBA_TOPK * (MOBA_BLOCK // PAGE_SIZE)
    hm = pl.BlockSpec((1, nh, tp, dh), lambda b, t, ph: (b, 0, 0, 0))
    return pl.pallas_call(
        functools.partial(_sample_attend_kernel, t_valid, layer),
        grid_spec=pltpu.PrefetchScalarGridSpec(
            num_scalar_prefetch=1,
            grid=(db, t_valid),
            in_specs=[hm, hm, hm, pl.BlockSpec(memory_space=pl.ANY), pl.BlockSpec(memory_space=pl.ANY)],
            out_specs=pl.BlockSpec((1, 1, nh, 1, dh), lambda b, t, ph: (b, t, 0, 0, 0)),
            scratch_shapes=[pltpu.VMEM((nh, n_slab, dh, PAGE_SIZE), F32),
                            pltpu.VMEM((nh, n_slab, dh, PAGE_SIZE), F32),
                            pltpu.SemaphoreType.DMA((2,))],
        ),
        out_shape=jax.ShapeDtypeStruct((db, t_valid, nh, 1, dh), F32),
        compiler_params=_cparams(("arbitrary", "arbitrary")),
        name="sample_attend",
    )(phys, q_s, k_new, v_new, cache_kt, cache_vt)


def _rwkv_feat_kernel(period, t_valid, cur_ref, prev_ref, mu_ref, w0_ref, dup_ref, a0_ref, aup_ref,
                      gup_ref, kk_ref, ka_ref, rk_ref, seg_ref,
                      r_o, lw_o, k_o, v_o, kk_o, b_o, g_o, bonus_o):
    cur = cur_ref[...]
    xs = cur + (prev_ref[...] - cur) * mu_ref[...]
    w = WIDTH_B
    r, k, v = xs[:, :w], xs[:, w:2 * w], xs[:, 2 * w:3 * w]
    dw = xs[:, 3 * w:3 * w + DECAY_LORA]
    da = xs[:, 3 * w + DECAY_LORA:3 * w + DECAY_LORA + AAA_LORA]
    dg = xs[:, 3 * w + DECAY_LORA + AAA_LORA:]
    lw = -DECAY_SCALE * jax.nn.sigmoid(w0_ref[...] + _dot3(jnp.tanh(dw), dup_ref[...]))
    a = jax.nn.sigmoid(a0_ref[...] + _dot3(da, aup_ref[...]))
    g = _dot3(jax.nn.sigmoid(dg), gup_ref[...])
    seg = seg_ref[...]
    kkr = k * kk_ref[...]
    kk = kkr / jnp.maximum(jnp.sqrt(_dot2_exact_rhs(kkr * kkr, seg)), 1e-12)
    k2 = k * (1.0 + (a - 1.0) * ka_ref[...])
    bonus = _dot2_exact_rhs(r * k2 * rk_ref[...], seg) * v
    if t_valid < period:
        rows = lax.broadcasted_iota(I32, (cur.shape[0], 1), 0)
        valid = (rows % period) < t_valid
        lw = jnp.where(valid, lw, 0.0)
        kk = jnp.where(valid, kk, 0.0)
        k2 = jnp.where(valid, k2, 0.0)
    r_o[...] = r
    lw_o[...] = lw
    k_o[...] = k2
    v_o[...] = v
    kk_o[...] = kk
    b_o[...] = kk * a
    g_o[...] = g
    bonus_o[...] = bonus


def _rwkv_features(p_cur, p_prev, prm, tm, period, t_valid):
    n = p_cur.shape[0]
    row = lambda v: v.reshape(1, -1).astype(F32)
    const = lambda i: (0, 0)
    params = [row(prm['rw_mu']), row(prm['rw_w0']), prm['rw_decay_up'], row(prm['rw_a0']), prm['rw_a_up'],
              prm['rw_g_up'], row(prm['rw_k_k']), row(prm['rw_k_a']), row(prm['rw_r_k']),
              _seg_ones(WIDTH_B, HEAD_DIM_B)]
    tok = pl.BlockSpec((tm, RWKV_COLS), lambda i: (i, 0))
    out = pl.BlockSpec((tm, WIDTH_B), lambda i: (i, 0))
    return pl.pallas_call(
        functools.partial(_rwkv_feat_kernel, period, t_valid),
        grid=(n // tm,),
        in_specs=[tok, tok] + [pl.BlockSpec(p.shape, const) for p in params],
        out_specs=[out] * 8,
        out_shape=[jax.ShapeDtypeStruct((n, WIDTH_B), F32)] * 8,
        compiler_params=_cparams(("parallel",)),
        name="rwkv_features",
    )(p_cur, p_prev, *params)


def _rwkv_chunk_kernel(r_ref, lw_ref, k_ref, v_ref, kk_ref, b_ref, g_ref, bonus_ref, s0_ref,
                       lnw_ref, lnb_ref, seg_ref, o_ref, sT_ref, st_sc):
    c = pl.program_id(1)
    L = r_ref.shape[1]
    nh, dh = N_HEADS_B, HEAD_DIM_B

    @pl.when(c == 0)
    def _():
        st_sc[...] = s0_ref[0]

    lw = lw_ref[0]
    ri = lax.broadcasted_iota(I32, (L, L), 0)
    ci = lax.broadcasted_iota(I32, (L, L), 1)
    strict = ri > ci
    incl = ri >= ci
    tri = jnp.where(incl, 1.0, 0.0).astype(BF16)
    cum = _dot2_exact_rhs_left(tri, lw)
    cum_last = cum[L - 1:L, :]
    e_pos = jnp.exp(cum)
    e_neg = jnp.exp(-cum)
    e_prev = jnp.exp(cum - lw)
    e_tail = jnp.exp(cum_last - cum)
    g_last = jnp.exp(cum_last)
    kk = kk_ref[0]
    alpha_all = kk * e_prev
    beta_all = b_ref[0] * e_neg
    kappa_all = k_ref[0] * e_neg
    rho_all = r_ref[0] * e_pos
    beta_t_all = b_ref[0] * e_tail
    kappa_t_all = k_ref[0] * e_tail
    v_all = v_ref[0]
    eye_l = jnp.where(ri == ci, 1.0, 0.0)
    rk = lax.broadcasted_iota(I32, (dh, dh), 0)
    ck = lax.broadcasted_iota(I32, (dh, dh), 1)

    hs = range(nh)
    cut = lambda x: [x[:, h * dh:(h + 1) * dh] for h in hs]
    alpha, beta, kappa, rho, vh = cut(alpha_all), cut(beta_all), cut(kappa_all), cut(rho_all), cut(v_all)
    beta_t, kappa_t = cut(beta_t_all), cut(kappa_t_all)
    wcat = [jnp.concatenate([beta[h], kappa[h]], axis=0) for h in hs]
    za = [_dot3(alpha[h], wcat[h], 1, 1) for h in hs]
    zr = [_dot3(rho[h], wcat[h], 1, 1) for h in hs]
    n_mat = [jnp.where(strict, za[h][:, :L], 0.0) for h in hs]
    m_mat = [jnp.where(strict, za[h][:, L:], 0.0) for h in hs]
    ri2 = lax.broadcasted_iota(I32, (L, 2 * L), 0)
    ci2 = lax.broadcasted_iota(I32, (L, 2 * L), 1)
    incl2 = ri2 >= jnp.where(ci2 >= L, ci2 - L, ci2)
    nrmr = [jnp.where(incl2, zr[h], 0.0) for h in hs]
    mv = [_dot3(m_mat[h], vh[h]) for h in hs]
    d = [eye_l - jnp.where(ri // 2 == ci // 2, n_mat[h], 0.0) for h in hs]
    s = 2
    while s < L:
        lower_left = (ri // (2 * s) == ci // (2 * s)) & ((ri % (2 * s)) >= s) & ((ci % (2 * s)) < s)
        de = [_dot3(d[h], jnp.where(lower_left, n_mat[h], 0.0)) for h in hs]
        d = [d[h] - _dot3(de[h], d[h]) for h in hs]
        s *= 2
    ta = [_dot3(d[h], jnp.concatenate([alpha[h], mv[h]], axis=1)) for h in hs]
    abar = [ta[h][:, :dh] for h in hs]
    pv = [jnp.concatenate([-ta[h][:, dh:], vh[h]], axis=0) for h in hs]
    rpp = [rho[h] - _dot3(nrmr[h][:, :L], abar[h]) for h in hs]
    y0 = [_dot3(nrmr[h], pv[h]) for h in hs]
    gt = [jnp.where(rk == ck, g_last[:, h * dh:(h + 1) * dh], 0.0) - _dot3(beta_t[h], abar[h], 0, 0)
          for h in hs]
    ht = [_dot3(jnp.concatenate([beta_t[h], kappa_t[h]], axis=0), pv[h], 0, 0) for h in hs]
    upd = [_dot3(jnp.concatenate([rpp[h], gt[h]], axis=0), st_sc[h]) for h in hs]
    for h in hs:
        st_sc[h] = upd[h][L:] + ht[h]
    y = jnp.concatenate([y0[h] + upd[h][:L] for h in hs], axis=1)
    seg = seg_ref[...]
    mu = _dot2_exact_rhs(y, seg) * (1.0 / dh)
    yc = y - mu
    var = _dot2_exact_rhs(yc * yc, seg) * (1.0 / dh)
    yn = yc * lax.rsqrt(var + LN_X_EPS) * lnw_ref[...] + lnb_ref[...]
    o_ref[0] = ((yn + bonus_ref[0]) * g_ref[0]).astype(o_ref.dtype)
    sT_ref[0] = st_sc[...]


def _dot2_exact_rhs_left(m_bf16, x):
    xh, xl = _split(x)
    return _dg(m_bf16, xh, 1, 0) + _dg(m_bf16, xl, 1, 0)


def _rwkv_chunked(feats, s0_t, ln_w, ln_b, chunk):
    r, lw, k2, v, kk, b, g, bonus = feats
    bq, s, w = r.shape
    nc = s // chunk
    tok = pl.BlockSpec((1, chunk, w), lambda bi, c: (bi, c, 0))
    st = pl.BlockSpec((1, N_HEADS_B, HEAD_DIM_B, HEAD_DIM_B), lambda bi, c: (bi, 0, 0, 0))
    const = lambda bi, c: (0, 0)
    return pl.pallas_call(
        _rwkv_chunk_kernel,
        grid=(bq, nc),
        in_specs=[tok] * 8 + [st, pl.BlockSpec((1, w), const), pl.BlockSpec((1, w), const),
                              pl.BlockSpec((w, w), const)],
        out_specs=[tok, st],
        out_shape=[jax.ShapeDtypeStruct((bq, s, w), BF16),
                   jax.ShapeDtypeStruct((bq, N_HEADS_B, HEAD_DIM_B, HEAD_DIM_B), F32)],
        scratch_shapes=[pltpu.VMEM((N_HEADS_B, HEAD_DIM_B, HEAD_DIM_B), F32)],
        compiler_params=_cparams(("parallel", "arbitrary")),
        name="rwkv_chunked",
    )(r, lw, k2, v, kk, b, g, bonus, s0_t, ln_w.reshape(1, w), ln_b.reshape(1, w),
      _seg_ones(w, HEAD_DIM_B))


def _mem_kv_kernel(mem_ref, g_ref, w_ref, kg_ref, mk_ref, mv_ref):
    h = _rms(mem_ref[0], g_ref[...]).astype(BF16)
    kv = _dg(h, w_ref[...], 1, 0)
    for hm in range(N_HEADS_M):
        sl = slice(hm * HEAD_DIM_M, (hm + 1) * HEAD_DIM_M)
        mk_ref[0, :, sl] = _rms(kv[:, sl], kg_ref[...])
    mv_ref[0] = kv[:, WIDTH_M:]


def _memory_kv(mem, norm_mem, w_mem_kv, k_norm_m):
    b, m, d = mem.shape
    const = lambda i: (0, 0)
    out = pl.BlockSpec((1, m, WIDTH_M), lambda i: (i, 0, 0))
    return pl.pallas_call(
        _mem_kv_kernel,
        grid=(b,),
        in_specs=[pl.BlockSpec((1, m, d), lambda i: (i, 0, 0)), pl.BlockSpec((1, d), const),
                  pl.BlockSpec((d, 2 * WIDTH_M), const), pl.BlockSpec((1, HEAD_DIM_M), const)],
        out_specs=[out, out],
        out_shape=[jax.ShapeDtypeStruct((b, m, WIDTH_M), F32)] * 2,
        compiler_params=_cparams(("parallel",)),
        name="memory_kv",
    )(mem, norm_mem.reshape(1, d), w_mem_kv.astype(BF16), k_norm_m.reshape(1, HEAD_DIM_M))


def _mem_attend_kernel(q_ref, mk_ref, mv_ref, g_ref, o_ref):
    q = q_ref[0]
    for hm in range(N_HEADS_M):
        sl = slice(hm * HEAD_DIM_M, (hm + 1) * HEAD_DIM_M)
        qh = (_rms(q[:, sl], g_ref[...]) * (HEAD_DIM_M ** -0.5)).astype(BF16)
        s = _dg(qh, mk_ref[0, :, sl].astype(BF16), 1, 1)
        p = jnp.exp(s - jnp.max(s, axis=1, keepdims=True))
        o = _dg(p.astype(BF16), mv_ref[0, :, sl].astype(BF16), 1, 0) / jnp.sum(p, axis=1, keepdims=True)
        o_ref[0, :, sl] = o.astype(o_ref.dtype)


def _memory_attend(q_m, mk, mv, q_norm_m, tq):
    b, s, w = q_m.shape
    m = mk.shape[1]
    kv = pl.BlockSpec((1, m, w), lambda bi, j: (bi, 0, 0))
    return pl.pallas_call(
        _mem_attend_kernel,
        grid=(b, s // tq),
        in_specs=[pl.BlockSpec((1, tq, w), lambda bi, j: (bi, j, 0)), kv, kv,
                  pl.BlockSpec((1, HEAD_DIM_M), lambda bi, j: (0, 0))],
        out_specs=pl.BlockSpec((1, tq, w), lambda bi, j: (bi, j, 0)),
        out_shape=jax.ShapeDtypeStruct((b, s, w), BF16),
        compiler_params=_cparams(("parallel", "parallel")),
        name="memory_attend",
    )(q_m, mk, mv, q_norm_m.reshape(1, HEAD_DIM_M))


def _merge_kernel(x_ref, oa_ref, ob_ref, om_ref, gl_ref, wb_ref, wo_ref, ng_ref, rw_ref, rb_ref,
                  x1_ref, h_ref, e_ref, gate_ref):
    d = x_ref.shape[1]
    gl = gl_ref[...]
    merged = jnp.zeros(x_ref.shape, F32)
    for n, o_ref in enumerate((oa_ref, ob_ref, om_ref)):
        y = _dg(o_ref[...], wb_ref[n], 1, 0)
        merged = merged + jax.nn.sigmoid(gl[:, n * d:(n + 1) * d]) * y
    x1 = x_ref[...] + _dg(merged.astype(BF16), wo_ref[...], 1, 0)
    x1_ref[...] = x1
    hn = _rms(x1, ng_ref[...])
    h_ref[...] = hn
    logits = _dot3(hn, rw_ref[...]) + rb_ref[...]
    tm, ne = logits.shape
    e_idx = lax.broadcasted_iota(I32, (tm, ne), 1)
    vals, idxs = [], []
    g = logits
    for _ in range(TOP_K):
        mx = jnp.max(g, axis=1, keepdims=True)
        first = jnp.min(jnp.where(g == mx, e_idx, ne), axis=1, keepdims=True)
        vals.append(mx)
        idxs.append(first)
        g = jnp.where(e_idx == first, -jnp.inf, g)
    top = jnp.concatenate(vals, axis=1)
    pe = jnp.exp(top - vals[0])
    gate_ref[...] = pe / jnp.sum(pe, axis=1, keepdims=True)
    e_ref[...] = jnp.concatenate(idxs, axis=1)


def _merge_and_route(x2, o_a, o_b, o_m, gl, prm, tm):
    n, d = x2.shape
    const2 = lambda i: (0, 0)
    tok = lambda wd: pl.BlockSpec((tm, wd), lambda i: (i, 0))
    return pl.pallas_call(
        _merge_kernel,
        grid=(n // tm,),
        in_specs=[tok(d), tok(512), tok(512), tok(512), tok(3 * d),
                  pl.BlockSpec((3, 512, d), lambda i: (0, 0, 0)), pl.BlockSpec((d, d), const2),
                  pl.BlockSpec((1, d), const2), pl.BlockSpec((d, N_EXPERTS), const2),
                  pl.BlockSpec((1, N_EXPERTS), const2)],
        out_specs=[tok(d), tok(d), tok(TOP_K), tok(TOP_K)],
        out_shape=[jax.ShapeDtypeStruct((n, d), F32), jax.ShapeDtypeStruct((n, d), F32),
                   jax.ShapeDtypeStruct((n, TOP_K), I32), jax.ShapeDtypeStruct((n, TOP_K), F32)],
        compiler_params=_cparams(("parallel",)),
        name="merge_route",
    )(x2, o_a, o_b, o_m, gl, prm['w_branch'].astype(BF16), prm['w_out'].astype(BF16),
      prm['norm_ffn'].reshape(1, d), prm['router_w'], prm['router_b'].reshape(1, N_EXPERTS))


def _onehots(e):
    tm = e.shape[0]
    e_idx = lax.broadcasted_iota(I32, (tm, N_EXPERTS), 1)
    return [jnp.where(e[:, k:k + 1] == e_idx, 1.0, 0.0) for k in range(TOP_K)]


def _moe_rank_kernel(e_ref, rank_ref, cnt_ref, base_sc):
    i = pl.program_id(0)

    @pl.when(i == 0)
    def _():
        base_sc[...] = jnp.zeros_like(base_sc)

    ohs = _onehots(e_ref[...])
    cnt = ohs[0] + ohs[1] + ohs[2] + ohs[3]
    tm = cnt.shape[0]
    ri = lax.broadcasted_iota(I32, (tm, tm), 0)
    ci = lax.broadcasted_iota(I32, (tm, tm), 1)
    tri = jnp.where(ri > ci, 1.0, 0.0).astype(BF16)
    tot = _dg(tri, cnt.astype(BF16), 1, 0) + base_sc[...]
    rank_ref[...] = jnp.concatenate([jnp.sum(oh * tot, axis=1, keepdims=True) for oh in ohs],
                                    axis=1).astype(I32)
    base_sc[...] = base_sc[...] + jnp.sum(cnt, axis=0, keepdims=True)
    cnt_ref[...] = base_sc[...].astype(I32)


def _moe_rank(top_e, tm):
    n = top_e.shape[0]
    return pl.pallas_call(
        _moe_rank_kernel,
        grid=(n // tm,),
        in_specs=[pl.BlockSpec((tm, TOP_K), lambda i: (i, 0))],
        out_specs=[pl.BlockSpec((tm, TOP_K), lambda i: (i, 0)), pl.BlockSpec((1, N_EXPERTS), lambda i: (0, 0))],
        out_shape=[jax.ShapeDtypeStruct((n, TOP_K), I32), jax.ShapeDtypeStruct((1, N_EXPERTS), I32)],
        scratch_shapes=[pltpu.VMEM((1, N_EXPERTS), F32)],
        compiler_params=_cparams(("arbitrary",)),
        name="moe_rank",
    )(top_e)


def _moe_rows_kernel(e_ref, rank_ref, start_ref, row_ref):
    ohs = _onehots(e_ref[...])
    st = start_ref[...].astype(F32)
    base = jnp.concatenate([jnp.sum(oh * st, axis=1, keepdims=True) for oh in ohs], axis=1)
    row_ref[...] = rank_ref[...] + base.astype(I32)


def _moe_rows(top_e, rank, starts, tm):
    n = top_e.shape[0]
    tok = pl.BlockSpec((tm, TOP_K), lambda i: (i, 0))
    return pl.pallas_call(
        _moe_rows_kernel,
        grid=(n // tm,),
        in_specs=[tok, tok, pl.BlockSpec((1, N_EXPERTS), lambda i: (0, 0))],
        out_specs=tok,
        out_shape=jax.ShapeDtypeStruct((n, TOP_K), I32),
        compiler_params=_cparams(("parallel",)),
        name="moe_rows",
    )(top_e, rank, starts)


def _dispatch_kernel(row_ref, h_ref, xs_in_ref, xs_ref, sem):
    del xs_in_ref
    tm = h_ref.shape[0]

    def copy(t, k):
        r = row_ref[t * TOP_K + k]
        return pltpu.make_async_copy(h_ref.at[pl.ds(t, 1)], xs_ref.at[pl.ds(r, 1)], sem)

    def start(t, carry):
        for k in range(TOP_K):
            copy(t, k).start()
        return carry

    def wait(t, carry):
        for k in range(TOP_K):
            copy(t, k).wait()
        return carry

    lax.fori_loop(0, tm, start, 0)
    lax.fori_loop(0, tm, wait, 0)


def _moe_dispatch(h, row_flat, n_rows, tm):
    n, d = h.shape
    xs0 = jnp.zeros((n_rows, d), F32)
    return pl.pallas_call(
        _dispatch_kernel,
        grid=(n // tm,),
        in_specs=[pl.BlockSpec((tm * TOP_K,), lambda i: (i,), memory_space=pltpu.SMEM),
                  pl.BlockSpec((tm, d), lambda i: (i, 0)),
                  pl.BlockSpec(memory_space=pl.ANY)],
        out_specs=pl.BlockSpec(memory_space=pl.ANY),
        out_shape=jax.ShapeDtypeStruct((n_rows, d), F32),
        scratch_shapes=[pltpu.SemaphoreType.DMA(())],
        input_output_aliases={2: 0},
        compiler_params=_cparams(("arbitrary",)),
        name="moe_dispatch",
    )(row_flat, h, xs0)


def _swiglu(u, d_ff):
    u_glu = jnp.minimum(u[:, :d_ff], SWIGLU_LIMIT)
    u_lin = jnp.clip(u[:, d_ff:], -SWIGLU_LIMIT, SWIGLU_LIMIT)
    return u_glu * jax.nn.sigmoid(SWIGLU_ALPHA * u_glu) * (u_lin + 1.0)


def _moe_ffn_kernel(be_ref, nu_ref, xs_ref, w1_ref, b1_ref, w2_ref, b2_ref, y_ref, w1_sc, w2_sc):
    i = pl.program_id(0)
    prev = be_ref[jnp.maximum(i - 1, 0)]
    first = (i == 0) | (be_ref[i] != prev)

    @pl.when(first)
    def _():
        w1_sc[...] = w1_ref[...].astype(BF16)
        w2_sc[...] = w2_ref[...].astype(BF16)

    @pl.when(i < nu_ref[0])
    def _():
        u = _dg(xs_ref[...].astype(BF16), w1_sc[...], 1, 0) + b1_ref[...]
        act = _swiglu(u, w2_ref.shape[0])
        y_ref[...] = _dg(act.astype(BF16), w2_sc[...], 1, 0) + b2_ref[...]

    @pl.when(i >= nu_ref[0])
    def _():
        y_ref[...] = jnp.zeros_like(y_ref)


def _moe_ffn(xs, blk_e, n_used, w1, b1, w2, b2):
    n_rows, d = xs.shape
    ne, _, f2 = w1.shape
    d_ff = w2.shape[1]
    nblk = n_rows // MOE_ROWS
    return pl.pallas_call(
        _moe_ffn_kernel,
        grid_spec=pltpu.PrefetchScalarGridSpec(
            num_scalar_prefetch=2,
            grid=(nblk,),
            in_specs=[pl.BlockSpec((MOE_ROWS, d), lambda i, be, nu: (i, 0)),
                      pl.BlockSpec((None, d, f2), lambda i, be, nu: (be[i], 0, 0)),
                      pl.BlockSpec((None, 1, f2), lambda i, be, nu: (be[i], 0, 0)),
                      pl.BlockSpec((None, d_ff, d), lambda i, be, nu: (be[i], 0, 0)),
                      pl.BlockSpec((None, 1, d), lambda i, be, nu: (be[i], 0, 0))],
            out_specs=pl.BlockSpec((MOE_ROWS, d), lambda i, be, nu: (i, 0)),
            scratch_shapes=[pltpu.VMEM((d, f2), BF16), pltpu.VMEM((d_ff, d), BF16)],
        ),
        out_shape=jax.ShapeDtypeStruct((n_rows, d), F32),
        compiler_params=_cparams(("arbitrary",)),
        name="moe_ffn",
    )(blk_e, n_used, xs, w1, b1.reshape(ne, 1, f2), w2, b2.reshape(ne, 1, d))


def _combine_kernel(row_ref, yb_ref, x1_ref, gate_ref, y_ref, buf, sem):
    tm = x1_ref.shape[0]

    def copy(t, k):
        r = row_ref[t * TOP_K + k]
        return pltpu.make_async_copy(yb_ref.at[pl.ds(r, 1)], buf.at[k, pl.ds(t, 1)], sem)

    def start(t, carry):
        for k in range(TOP_K):
            copy(t, k).start()
        return carry

    def wait(t, carry):
        for k in range(TOP_K):
            copy(t, k).wait()
        return carry

    lax.fori_loop(0, tm, start, 0)
    lax.fori_loop(0, tm, wait, 0)
    gates = gate_ref[...]
    acc = buf[0] * gates[:, 0:1]
    for k in range(1, TOP_K):
        acc = acc + buf[k] * gates[:, k:k + 1]
    y_ref[...] = x1_ref[...] + acc


def _moe_combine(yb, row_flat, x1, gates, tm):
    n, d = x1.shape
    return pl.pallas_call(
        _combine_kernel,
        grid=(n // tm,),
        in_specs=[pl.BlockSpec((tm * TOP_K,), lambda i: (i,), memory_space=pltpu.SMEM),
                  pl.BlockSpec(memory_space=pl.ANY),
                  pl.BlockSpec((tm, d), lambda i: (i, 0)),
                  pl.BlockSpec((tm, TOP_K), lambda i: (i, 0))],
        out_specs=pl.BlockSpec((tm, d), lambda i: (i, 0)),
        out_shape=jax.ShapeDtypeStruct((n, d), F32),
        scratch_shapes=[pltpu.VMEM((TOP_K, tm, d), F32), pltpu.SemaphoreType.DMA(())],
        compiler_params=_cparams(("arbitrary",)),
        name="moe_combine",
    )(row_flat, yb, x1, gates)


def _moe_block(x1, hn, top_e, gates, prm, tm):
    n, d = x1.shape
    rank, counts = _moe_rank(top_e, tm)
    counts = counts[0]
    padded = (counts + MOE_ROWS - 1) // MOE_ROWS * MOE_ROWS
    p_end = jnp.cumsum(padded)
    starts = (p_end - padded).astype(I32)
    nblk = -(-(n * TOP_K + N_EXPERTS * (MOE_ROWS - 1)) // MOE_ROWS)
    n_used = (p_end[-1] // MOE_ROWS).astype(I32)
    blk_i = jnp.minimum(jnp.arange(nblk, dtype=I32), n_used - 1)
    n_before = jnp.sum((p_end[None, :] <= (blk_i * MOE_ROWS)[:, None]).astype(I32), axis=1)
    blk_e = jnp.minimum(n_before, N_EXPERTS - 1).astype(I32)
    row = _moe_rows(top_e, rank, starts[None, :], tm)
    row_flat = row.reshape(-1)
    xs = _moe_dispatch(hn, row_flat, nblk * MOE_ROWS, tm)
    yb = _moe_ffn(xs, blk_e, n_used[None], prm['moe_w1'], prm['moe_b1'], prm['moe_w2'], prm['moe_b2'])
    return _moe_combine(yb, row_flat, x1, gates, tm)


def _split_w_in(w_in):
    d = w_in.shape[0]
    a = 3 * WIDTH_A
    b = a + RWKV_COLS
    c = b + WIDTH_M
    wb = w_in.astype(BF16)
    return [wb[:, :a], wb[:, a:b], wb[:, b:c], wb[:, c:]]


def _token_mix_tail(x2, o_a, o_b, o_m, gl, prm, tm):
    x1, hn, top_e, gates = _merge_and_route(x2, o_a, o_b, o_m, gl, prm, tm)
    return _moe_block(x1, hn, top_e, gates, prm, tm)


def _layer_prompt(x, mem, prm, ws):
    b, s, d = x.shape
    tm = 256
    x2 = x.reshape(b * s, d)
    qkv, p_rw, q_m, gl = _in_projection(x2, prm['norm_mix'].reshape(1, d), ws, tm)
    pos = jnp.arange(s, dtype=I32)
    q_s, k_t, v_t, kaug, vaug, kmean = _moba_prep(qkv.reshape(b, s, -1), pos, prm['q_norm_a'], prm['k_norm_a'],
                                                  MOBA_BLOCK, True)
    k_o, v_o = jnp.swapaxes(k_t, 2, 3), jnp.swapaxes(v_t, 2, 3)
    o_a = _moba_flash(q_s, kaug, vaug, kmean).reshape(b * s, WIDTH_A)

    p3 = p_rw.reshape(b, s, RWKV_COLS)
    p_prev = jnp.concatenate([jnp.zeros_like(p3[:, :1]), p3[:, :-1]], axis=1).reshape(b * s, RWKV_COLS)
    feats = _rwkv_features(p_rw, p_prev, prm, tm, 1, 1)
    feats = [f.reshape(b, s, WIDTH_B) for f in feats]
    s0_t = jnp.zeros((b, N_HEADS_B, HEAD_DIM_B, HEAD_DIM_B), F32)
    o_b, st = _rwkv_chunked(feats, s0_t, prm['ln_x_w'], prm['ln_x_b'], RWKV_CHUNK)
    wkv = jnp.swapaxes(st, 2, 3)

    mk, mv = _memory_kv(mem, prm['norm_mem'], prm['w_mem_kv'], prm['k_norm_m'])
    o_m = _memory_attend(q_m.reshape(b, s, WIDTH_M), mk, mv, prm['q_norm_m'], 512)

    y = _token_mix_tail(x2, o_a, o_b.reshape(b * s, WIDTH_B), o_m.reshape(b * s, WIDTH_M), gl, prm, tm)
    m = mem.shape[1]
    return (y.reshape(b, s, d), k_o, v_o, wkv, p3[:, -1],
            mk.reshape(b, m, N_HEADS_M, HEAD_DIM_M), mv.reshape(b, m, N_HEADS_M, HEAD_DIM_M))


def _layer_sample(x, cache_k, cache_v, mem_k, mem_v, wkv0, shift0, page_table, layer, prm, ws):
    db, t, d = x.shape
    tp = SAMPLE_T_PAD
    past_len = page_table.shape[1] * PAGE_SIZE
    assert past_len % MOBA_BLOCK == 0 and t <= tp
    ppb = MOBA_BLOCK // PAGE_SIZE
    n = db * tp
    x2 = jnp.pad(x, ((0, 0), (0, tp - t), (0, 0))).reshape(n, d)
    qkv, p_rw, q_m, gl = _in_projection(x2, prm['norm_mix'].reshape(1, d), ws, n)
    pos = past_len + jnp.arange(tp, dtype=I32)
    q_s, k_o, v_o = _moba_prep(qkv.reshape(db, tp, -1), pos, prm['q_norm_a'], prm['k_norm_a'], tp, False)

    cache_kt, cache_vt = jnp.swapaxes(cache_k, 3, 4), jnp.swapaxes(cache_v, 3, 4)
    kmean_t = _page_means(cache_kt, page_table, layer)
    idx = _sample_select(q_s, kmean_t)
    idx = jnp.transpose(idx[:, :, :t], (0, 2, 1, 3))
    logical = idx[..., None] * ppb + jnp.arange(ppb, dtype=I32)
    phys = page_table[jnp.arange(db)[:, None, None, None, None], logical].reshape(-1).astype(I32)
    o_a = _sample_attend(q_s, k_o, v_o, cache_kt, cache_vt, phys, t, layer)
    o_a = jnp.pad(o_a.reshape(db, t, WIDTH_A), ((0, 0), (0, tp - t), (0, 0))).reshape(n, WIDTH_A).astype(BF16)

    p3 = p_rw.reshape(db, tp, RWKV_COLS)
    p_prev = jnp.concatenate([shift0[:, None, :], p3[:, :-1]], axis=1).reshape(n, RWKV_COLS)
    feats = _rwkv_features(p_rw, p_prev, prm, n, tp, t)
    feats = [f.reshape(db, tp, WIDTH_B) for f in feats]
    o_b, st = _rwkv_chunked(feats, jnp.swapaxes(wkv0, 2, 3), prm['ln_x_w'], prm['ln_x_b'], tp)
    wkv = jnp.swapaxes(st, 2, 3)

    m = mem_k.shape[1]
    o_m = _memory_attend(q_m.reshape(db, tp, WIDTH_M), mem_k.reshape(db, m, WIDTH_M),
                         mem_v.reshape(db, m, WIDTH_M), prm['q_norm_m'], tp)

    y = _token_mix_tail(x2, o_a, o_b.reshape(n, WIDTH_B), o_m.reshape(n, WIDTH_M), gl, prm, n)
    return (y.reshape(db, tp, d)[:, :t], k_o[:, :, :t], v_o[:, :, :t], wkv, p3[:, t - 1])


def kernel(x_prompt, x_sample, mem_prompt, cache_k, cache_v, cache_mem_k, cache_mem_v, state_wkv, state_shift,
           page_table, norm_mix, norm_mem, norm_ffn, w_in, q_norm_a, k_norm_a, q_norm_m, k_norm_m, w_mem_kv,
           rw_mu, rw_w0, rw_decay_up, rw_a0, rw_a_up, rw_g_up, rw_k_k, rw_k_a, rw_r_k, ln_x_w, ln_x_b,
           w_branch, w_out, router_w, router_b, moe_w1, moe_b1, moe_w2, moe_b2):
    depth = w_in.shape[0]
    xp, xs = x_prompt, x_sample
    outs_p = [[] for _ in range(6)]
    outs_s = [[] for _ in range(4)]
    for l in range(depth):
        prm = dict(norm_mix=norm_mix[l], norm_mem=norm_mem[l], norm_ffn=norm_ffn[l], w_in=w_in[l],
                   q_norm_a=q_norm_a[l], k_norm_a=k_norm_a[l], q_norm_m=q_norm_m[l], k_norm_m=k_norm_m[l],
                   w_mem_kv=w_mem_kv[l], rw_mu=rw_mu[l], rw_w0=rw_w0[l], rw_decay_up=rw_decay_up[l],
                   rw_a0=rw_a0[l], rw_a_up=rw_a_up[l], rw_g_up=rw_g_up[l], rw_k_k=rw_k_k[l], rw_k_a=rw_k_a[l],
                   rw_r_k=rw_r_k[l].reshape(-1), ln_x_w=ln_x_w[l], ln_x_b=ln_x_b[l], w_branch=w_branch[l],
                   w_out=w_out[l], router_w=router_w[l], router_b=router_b[l], moe_w1=moe_w1[l],
                   moe_b1=moe_b1[l], moe_w2=moe_w2[l], moe_b2=moe_b2[l])
        ws = _split_w_in(prm['w_in'])
        xp, *rest_p = _layer_prompt(xp, mem_prompt, prm, ws)
        for acc, val in zip(outs_p, rest_p):
            acc.append(val)
        xs, *rest_s = _layer_sample(xs, cache_k, cache_v, cache_mem_k[l], cache_mem_v[l], state_wkv[l],
                                    state_shift[l], page_table, l, prm, ws)
        for acc, val in zip(outs_s, rest_s):
            acc.append(val)
    kp, vp, wkvp, shp, mkp, mvp = [jnp.stack(a) for a in outs_p]
    ksm, vsm, wkvs, shs = [jnp.stack(a) for a in outs_s]
    return (xp, xs, kp, vp, wkvp, shp, mkp, mvp, ksm, vsm, wkvs, shs)
```

```python
import functools
import math

import jax
import jax.numpy as jnp
from jax import lax
from jax.experimental import pallas as pl
from jax.experimental.pallas import tpu as pltpu

F32 = jnp.float32
BF16 = jnp.bfloat16
I32 = jnp.int32

N_HEADS_A = 8
HEAD_DIM_A = 64
WIDTH_A = 512
MOBA_BLOCK = 256
MOBA_TOPK = 3
ROT_DIM = 16
ROPE_THETA = 500000.0
PAGE_SIZE = 128
N_HEADS_B = 8
HEAD_DIM_B = 64
WIDTH_B = 512
DECAY_LORA = 64
AAA_LORA = 64
GATE_LORA = 128
RWKV_COLS = 1792
DECAY_SCALE = math.exp(-0.5)
LN_X_EPS = 64e-5
N_HEADS_M = 4
HEAD_DIM_M = 128
WIDTH_M = 512
N_EXPERTS = 32
TOP_K = 4
SWIGLU_ALPHA = 1.702
SWIGLU_LIMIT = 7.0
NORM_EPS = 1e-6

NEG_BIG = -1e30
SAMPLE_T_PAD = 8
RWKV_CHUNK = 64
MOE_ROWS = 256
VMEM_LIMIT = 56 * 1024 * 1024


def _cparams(sem, vmem=None):
    return pltpu.CompilerParams(dimension_semantics=sem, vmem_limit_bytes=vmem or VMEM_LIMIT)


def _dg(a, b, ca, cb):
    return lax.dot_general(a, b, (((ca,), (cb,)), ((), ())), preferred_element_type=F32)


def _split(x):
    hi = x.astype(BF16)
    lo = (x - hi.astype(F32)).astype(BF16)
    return hi, lo


def _dot3(a, b, ca=1, cb=0):
    ah, al = _split(a)
    bh, bl = _split(b)
    return _dg(ah, bh, ca, cb) + (_dg(ah, bl, ca, cb) + _dg(al, bh, ca, cb))


def _dot2_exact_rhs(a, b_bf16):
    ah, al = _split(a)
    return _dg(ah, b_bf16, 1, 0) + _dg(al, b_bf16, 1, 0)


def _dotb(a, b, ca=1, cb=0):
    return _dg(a.astype(BF16), b.astype(BF16), ca, cb)


def _rms(x, gain_row):
    ms = jnp.mean(x * x, axis=-1, keepdims=True)
    return x * lax.rsqrt(ms + NORM_EPS) * gain_row


def _seg_ones(width, seg):
    r = lax.broadcasted_iota(I32, (width, width), 0) // seg
    c = lax.broadcasted_iota(I32, (width, width), 1) // seg
    return jnp.where(r == c, 1.0, 0.0).astype(BF16)


def _proj_kernel(x_ref, g_ref, w1, w2, w3, w4, o1, o2, o3, o4):
    h = _rms(x_ref[...], g_ref[...]).astype(BF16)
    o1[...] = _dg(h, w1[...], 1, 0)
    o2[...] = _dg(h, w2[...], 1, 0)
    o3[...] = _dg(h, w3[...], 1, 0)
    o4[...] = _dg(h, w4[...], 1, 0).astype(o4.dtype)


def _in_projection(x2, gain, ws, tm):
    n, d = x2.shape
    widths = [w.shape[1] for w in ws]
    dtypes = [F32, F32, F32, BF16]
    const = lambda i: (0, 0)
    return pl.pallas_call(
        _proj_kernel,
        grid=(n // tm,),
        in_specs=[pl.BlockSpec((tm, d), lambda i: (i, 0)), pl.BlockSpec((1, d), const)]
        + [pl.BlockSpec((d, wd), const) for wd in widths],
        out_specs=[pl.BlockSpec((tm, wd), lambda i: (i, 0)) for wd in widths],
        out_shape=[jax.ShapeDtypeStruct((n, wd), dt) for wd, dt in zip(widths, dtypes)],
        compiler_params=_cparams(("parallel",)),
        name="in_projection",
    )(x2, gain, *ws)


def _rope_tables(pos):
    half = ROT_DIM // 2
    inv_freq = 1.0 / (ROPE_THETA ** (jnp.arange(0, ROT_DIM, 2, dtype=F32) / ROT_DIM))
    ang = pos.astype(F32)[:, None] * inv_freq[None, :]
    cos, sin = jnp.cos(ang), jnp.sin(ang)
    n = pos.shape[0]
    rest = HEAD_DIM_A - ROT_DIM
    c = jnp.concatenate([cos, cos, jnp.ones((n, rest), F32)], axis=1)
    s_up = jnp.concatenate([-sin, jnp.zeros((n, half + rest), F32)], axis=1)
    s_dn = jnp.concatenate([jnp.zeros((n, half), F32), sin, jnp.zeros((n, rest), F32)], axis=1)
    two = lambda t: jnp.concatenate([t, t], axis=1)
    return two(c), two(s_up), two(s_dn)


def _norm_rope(x, seg, gain, c, s_up, s_dn):
    ss = _dot2_exact_rhs(x * x, seg)
    y = x * lax.rsqrt(ss * (1.0 / HEAD_DIM_A) + NORM_EPS) * gain
    half = ROT_DIM // 2
    up = pltpu.roll(y, WIDTH_A - half, 1)
    dn = pltpu.roll(y, half, 1)
    return y * c + up * s_up + dn * s_dn


def _moba_prep_kernel(with_blocks, qkv_ref, seg_ref, qg_ref, kg_ref, c_ref, su_ref, sd_ref, *outs):
    if with_blocks:
        qs_ref, kt_ref, vt_ref, kaug_ref, vaug_ref, kmean_ref = outs
    else:
        qs_ref, k_ref, v_ref = outs
    x = qkv_ref[0]
    tm = x.shape[0]
    rep = lambda r: jnp.concatenate([r[...]] * (WIDTH_A // 128), axis=1)
    c, su, sd = rep(c_ref), rep(su_ref), rep(sd_ref)
    seg = seg_ref[...]
    q = _norm_rope(x[:, :WIDTH_A], seg, qg_ref[...], c, su, sd) * (HEAD_DIM_A ** -0.5)
    k = _norm_rope(x[:, WIDTH_A:2 * WIDTH_A], seg, kg_ref[...], c, su, sd)
    v = x[:, 2 * WIDTH_A:]
    if with_blocks:
        blk = pl.program_id(1)
        lane = lax.broadcasted_iota(I32, (tm, HEAD_DIM_A), 1)
        onehot = jnp.where(lane == blk, 1.0, 0.0).astype(BF16)
        ones_col = jnp.where(lane == 0, 1.0, 0.0).astype(BF16)

        @pl.when(blk == 0)
        def _():
            kmean_ref[...] = jnp.zeros_like(kmean_ref)

    for h in range(N_HEADS_A):
        sl = slice(h * HEAD_DIM_A, (h + 1) * HEAD_DIM_A)
        qs_ref[0, h] = q[:, sl]
        if with_blocks:
            kt_ref[0, h] = k[:, sl].T
            vt_ref[0, h] = v[:, sl].T
            kaug_ref[0, h] = jnp.concatenate([k[:, sl].astype(BF16), onehot], axis=1)
            vaug_ref[0, h] = jnp.concatenate([v[:, sl].astype(BF16), ones_col], axis=1)
            kmean_ref[0, h, pl.ds(blk, 1), :] = jnp.mean(k[:, sl], axis=0, keepdims=True)
        else:
            k_ref[0, h] = k[:, sl]
            v_ref[0, h] = v[:, sl]


def _moba_prep(qkv, pos, q_gain, k_gain, tm, with_blocks):
    bq, s, _ = qkv.shape
    nb = s // tm
    c, su, sd = _rope_tables(pos)
    seg = _seg_ones(WIDTH_A, HEAD_DIM_A)
    tile8 = lambda g: jnp.tile(g.astype(F32), N_HEADS_A)[None, :]
    hm = jax.ShapeDtypeStruct((bq, N_HEADS_A, s, HEAD_DIM_A), F32)
    hm_spec = pl.BlockSpec((1, N_HEADS_A, tm, HEAD_DIM_A), lambda b, j: (b, 0, j, 0))
    if with_blocks:
        assert nb <= HEAD_DIM_A
        tr = jax.ShapeDtypeStruct((bq, N_HEADS_A, HEAD_DIM_A, s), F32)
        tr_spec = pl.BlockSpec((1, N_HEADS_A, HEAD_DIM_A, tm), lambda b, j: (b, 0, 0, j))
        aug = jax.ShapeDtypeStruct((bq, N_HEADS_A, s, 128), BF16)
        aug_spec = pl.BlockSpec((1, N_HEADS_A, tm, 128), lambda b, j: (b, 0, j, 0))
        out_shape = [hm, tr, tr, aug, aug,
                     jax.ShapeDtypeStruct((bq, N_HEADS_A, HEAD_DIM_A, HEAD_DIM_A), F32)]
        out_specs = [hm_spec, tr_spec, tr_spec, aug_spec, aug_spec,
                     pl.BlockSpec((1, N_HEADS_A, HEAD_DIM_A, HEAD_DIM_A), lambda b, j: (b, 0, 0, 0))]
    else:
        out_shape = [hm, hm, hm]
        out_specs = [hm_spec, hm_spec, hm_spec]
    const = lambda b, j: (0, 0)
    tab = pl.BlockSpec((tm, 128), lambda b, j: (j, 0))
    return pl.pallas_call(
        functools.partial(_moba_prep_kernel, with_blocks),
        grid=(bq, nb),
        in_specs=[pl.BlockSpec((1, tm, 3 * WIDTH_A), lambda b, j: (b, j, 0)),
                  pl.BlockSpec((WIDTH_A, WIDTH_A), const),
                  pl.BlockSpec((1, WIDTH_A), const), pl.BlockSpec((1, WIDTH_A), const), tab, tab, tab],
        out_specs=out_specs,
        out_shape=out_shape,
        compiler_params=_cparams(("parallel", "arbitrary")),
        name="moba_prep",
    )(qkv, seg, tile8(q_gain), tile8(k_gain), c, su, sd)


FLASH_HEADS = 2


def _moba_flash_kernel(q_ref, kaug_ref, vaug_ref, kmean_ref, o_ref, m_sc, acc_sc):
    i = pl.program_id(1)
    hp = pl.program_id(2)
    tq, dh = q_ref.shape[2], q_ref.shape[3]
    gs = range(FLASH_HEADS)
    q = [q_ref[0, g] for g in gs]
    gate_t = [_dot3(kmean_ref[0, g], q[g], 1, 1) for g in gs]
    n_idx = lax.broadcasted_iota(I32, gate_t[0].shape, 0)
    n_tot = gate_t[0].shape[0]
    gv = [jnp.where(n_idx < i, gate_t[g], -jnp.inf) for g in gs]
    sel = [n_idx == i for g in gs]
    for _ in range(MOBA_TOPK):
        mx = [jnp.max(gv[g], axis=0, keepdims=True) for g in gs]
        cand = [(gv[g] == mx[g]) & (mx[g] > -jnp.inf) for g in gs]
        first = [jnp.min(jnp.where(cand[g], n_idx, n_tot), axis=0, keepdims=True) for g in gs]
        pick = [n_idx == first[g] for g in gs]
        sel = [sel[g] | pick[g] for g in gs]
        gv = [jnp.where(pick[g], -jnp.inf, gv[g]) for g in gs]
    zeros_t = jnp.zeros((dh, tq), F32)
    zeros_q = jnp.zeros((tq, dh), F32)
    bias = [jnp.concatenate([zeros_t, jnp.where(sel[g], 0.0, NEG_BIG)], axis=0).T for g in gs]
    qaug = [(jnp.concatenate([q[g], zeros_q], axis=1) + bias[g]).astype(BF16) for g in gs]

    start = pl.multiple_of(i * tq, tq)
    row = lax.broadcasted_iota(I32, (tq, tq), 0)
    col = lax.broadcasted_iota(I32, (tq, tq), 1)
    s = [jnp.where(col <= row, _dg(qaug[g], kaug_ref[0, g, pl.ds(start, tq), :], 1, 1), -jnp.inf) for g in gs]
    m0 = [jnp.max(s[g], axis=1, keepdims=True) for g in gs]
    for g in gs:
        m_sc[g] = jnp.broadcast_to(m0[g], (tq, 128))
        acc_sc[g] = _dg(jnp.exp(s[g] - m0[g]).astype(BF16), vaug_ref[0, g, pl.ds(start, tq), :], 1, 0)

    def step(off, width):
        sj = [_dg(qaug[g], kaug_ref[0, g, pl.ds(off, width), :], 1, 1) for g in gs]
        m_old = [m_sc[g] for g in gs]
        m_new = [jnp.maximum(m_old[g], jnp.max(sj[g], axis=1, keepdims=True)) for g in gs]
        pj = [jnp.exp(sj[g] - jnp.concatenate([m_new[g]] * (width // 128), axis=1)) for g in gs]
        for g in gs:
            acc_sc[g] = (jnp.exp(m_old[g] - m_new[g]) * acc_sc[g]
                         + _dg(pj[g].astype(BF16), vaug_ref[0, g, pl.ds(off, width), :], 1, 0))
            m_sc[g] = m_new[g]

    def quad(j, carry):
        step(pl.multiple_of(j * (4 * tq), 4 * tq), 4 * tq)
        return carry

    lax.fori_loop(0, i // 4, quad, 0)

    @pl.when(i % 4 >= 2)
    def _():
        step(pl.multiple_of((i // 4) * (4 * tq), 2 * tq), 2 * tq)

    @pl.when(i % 2 == 1)
    def _():
        step(pl.multiple_of((i - 1) * tq, tq), tq)

    outs = []
    for g in gs:
        acc = acc_sc[g]
        outs.append(acc[:, :dh] / acc[:, dh:dh + 1])
    out = jnp.concatenate(outs, axis=1).astype(o_ref.dtype)
    wd = FLASH_HEADS * dh
    for pp in range(N_HEADS_A // FLASH_HEADS):
        @pl.when(hp == pp)
        def _():
            o_ref[0, :, pp * wd:(pp + 1) * wd] = out


def _moba_flash(q_s, kaug, vaug, kmean):
    b, nh, s, dh = q_s.shape
    tq = MOBA_BLOCK
    nb = s // tq
    g = FLASH_HEADS
    return pl.pallas_call(
        _moba_flash_kernel,
        grid=(b, nb, nh // g),
        in_specs=[pl.BlockSpec((1, g, tq, dh), lambda bi, i, h: (bi, h, i, 0)),
                  pl.BlockSpec((1, g, s, 128), lambda bi, i, h: (bi, h, 0, 0)),
                  pl.BlockSpec((1, g, s, 128), lambda bi, i, h: (bi, h, 0, 0)),
                  pl.BlockSpec((1, g, dh, dh), lambda bi, i, h: (bi, h, 0, 0))],
        out_specs=pl.BlockSpec((1, tq, nh * dh), lambda bi, i, h: (bi, i, 0)),
        out_shape=jax.ShapeDtypeStruct((b, s, nh * dh), BF16),
        scratch_shapes=[pltpu.VMEM((g, tq, 128), F32), pltpu.VMEM((g, tq, 128), F32)],
        compiler_params=_cparams(("parallel", "parallel", "arbitrary")),
        name="moba_flash",
    )(q_s, kaug, vaug, kmean)


PAGES_PER_STEP = 16


def _page_mean_kernel(pt_ref, *refs):
    pages, out_ref = refs[:PAGES_PER_STEP], refs[PAGES_PER_STEP]
    s = pl.program_id(1)
    ppb = MOBA_BLOCK // PAGE_SIZE
    bps = PAGES_PER_STEP // ppb

    @pl.when(s == 0)
    def _():
        out_ref[...] = jnp.zeros_like(out_ref)

    lane = lax.broadcasted_iota(I32, out_ref.shape[2:], 1)
    for h in range(N_HEADS_A):
        acc = out_ref[0, h]
        for j in range(bps):
            tot = pages[ppb * j][h]
            for u in range(1, ppb):
                tot = tot + pages[ppb * j + u][h]
            col = jnp.sum(tot, axis=1, keepdims=True) * (1.0 / MOBA_BLOCK)
            acc = jnp.where(lane == s * bps + j, col, acc)
        out_ref[0, h] = acc


def _page_means(cache_kt, page_table, layer):
    db, n_pages = page_table.shape
    ppb = MOBA_BLOCK // PAGE_SIZE
    n_full = n_pages // ppb
    steps = n_full * ppb // PAGES_PER_STEP
    _, _, nh, dh, pg = cache_kt.shape

    def page_spec(u):
        return pl.BlockSpec((None, None, nh, dh, pg),
                            lambda b, s, pt: (layer, pt[b, s * PAGES_PER_STEP + u], 0, 0, 0))

    return pl.pallas_call(
        _page_mean_kernel,
        grid_spec=pltpu.PrefetchScalarGridSpec(
            num_scalar_prefetch=1,
            grid=(db, steps),
            in_specs=[page_spec(u) for u in range(PAGES_PER_STEP)],
            out_specs=pl.BlockSpec((1, nh, dh, n_full), lambda b, s, pt: (b, 0, 0, 0)),
        ),
        out_shape=jax.ShapeDtypeStruct((db, nh, dh, n_full), F32),
        compiler_params=_cparams(("parallel", "arbitrary")),
        name="page_means",
    )(page_table, *([cache_kt] * PAGES_PER_STEP))


def _sample_select_kernel(q_ref, km_ref, idx_ref):
    tp = q_ref.shape[2]
    nb = km_ref.shape[3]
    n_idx = lax.broadcasted_iota(I32, (tp, nb), 1)
    for h in range(N_HEADS_A):
        g = _dot3(q_ref[0, h], km_ref[0, h])
        cols = []
        for _ in range(MOBA_TOPK):
            mx = jnp.max(g, axis=1, keepdims=True)
            first = jnp.min(jnp.where(g == mx, n_idx, nb), axis=1, keepdims=True)
            cols.append(first)
            g = jnp.where(n_idx == first, -jnp.inf, g)
        idx_ref[0, h] = jnp.concatenate(cols, axis=1)


def _sample_select(q_s, kmean_t):
    db, nh, tp, dh = q_s.shape
    nb = kmean_t.shape[3]
    return pl.pallas_call(
        _sample_select_kernel,
        grid=(db,),
        in_specs=[pl.BlockSpec((1, nh, tp, dh), lambda b: (b, 0, 0, 0)),
                  pl.BlockSpec((1, nh, dh, nb), lambda b: (b, 0, 0, 0))],
        out_specs=pl.BlockSpec((1, nh, tp, MOBA_TOPK), lambda b: (b, 0, 0, 0)),
        out_shape=jax.ShapeDtypeStruct((db, nh, tp, MOBA_TOPK), I32),
        compiler_params=_cparams(("parallel",)),
        name="sample_select",
    )(q_s, kmean_t)


def _sample_attend_kernel(t_valid, layer, phys_ref, q_ref, kn_ref, vn_ref, ck_ref, cv_ref, o_ref,
                          kbuf, vbuf, sem):
    b = pl.program_id(0)
    t = pl.program_id(1)
    ppb = MOBA_BLOCK // PAGE_SIZE
    n_slab = MOBA_TOPK * ppb
    tp = q_ref.shape[2]

    n = b * t_valid + t
    slot = n % 2

    def copies(step, sl, h, u):
        page = phys_ref[(step * N_HEADS_A + h) * n_slab + u]
        return (pltpu.make_async_copy(ck_ref.at[layer, page, h], kbuf.at[sl, h, u], sem.at[sl, 0]),
                pltpu.make_async_copy(cv_ref.at[layer, page, h], vbuf.at[sl, h, u], sem.at[sl, 1]))

    def issue(step, sl):
        for h in range(N_HEADS_A):
            for u in range(n_slab):
                ck, cv = copies(step, sl, h, u)
                ck.start()
                cv.start()

    @pl.when(n == 0)
    def _():
        issue(0, 0)

    @pl.when(n + 1 < pl.num_programs(0) * t_valid)
    def _():
        issue(n + 1, 1 - slot)

    pltpu.make_async_copy(kbuf.at[slot], kbuf.at[slot], sem.at[slot, 0]).wait()
    pltpu.make_async_copy(vbuf.at[slot], vbuf.at[slot], sem.at[slot, 1]).wait()

    row = lax.broadcasted_iota(I32, (tp, 1), 0)
    key = lax.broadcasted_iota(I32, (tp, tp), 1)
    for h in range(N_HEADS_A):
        qh = q_ref[0, h].astype(BF16)
        s_own = _dg(qh, kn_ref[0, h].astype(BF16), 1, 1)
        s_own = jnp.where((key <= t) & (key < t_valid), s_own, -jnp.inf)
        s_sel = [_dg(qh, kbuf[slot, h, u].astype(BF16), 1, 0) for u in range(n_slab)]
        m = jnp.max(s_own, axis=1, keepdims=True)
        for sj in s_sel:
            m = jnp.maximum(m, jnp.max(sj, axis=1, keepdims=True))
        p_own = jnp.exp(s_own - m)
        l = jnp.sum(p_own, axis=1, keepdims=True)
        acc = _dg(p_own.astype(BF16), vn_ref[0, h].astype(BF16), 1, 0)
        for u, sj in enumerate(s_sel):
            pj = jnp.exp(sj - m)
            l = l + jnp.sum(pj, axis=1, keepdims=True)
            acc = acc + _dg(pj.astype(BF16), vbuf[slot, h, u].astype(BF16), 1, 1)
        out = acc / l
        o_ref[0, 0, h] = jnp.sum(jnp.where(row == t, out, 0.0), axis=0, keepdims=True)


def _sample_attend(q_s, k_new, v_new, cache_kt, cache_vt, phys, t_valid, layer):
    db, nh, tp, dh = q_s.shape
    n_slab = MOBA_TOPK * (MOBA_BLOCK // PAGE_SIZE)
    hm = pl.BlockSpec((1, nh, tp, dh), lambda b, t, ph: (b, 0, 0, 0))
    return pl.pallas_call(
        functools.partial(_sample_attend_kernel, t_valid, layer),
        grid_spec=pltpu.PrefetchScalarGridSpec(
            num_scalar_prefetch=1,
            grid=(db, t_valid),
            in_specs=[hm, hm, hm, pl.BlockSpec(memory_space=pl.ANY), pl.BlockSpec(memory_space=pl.ANY)],
            out_specs=pl.BlockSpec((1, 1, nh, 1, dh), lambda b, t, ph: (b, t, 0, 0, 0)),
            scratch_shapes=[pltpu.VMEM((2, nh, n_slab, dh, PAGE_SIZE), F32),
                            pltpu.VMEM((2, nh, n_slab, dh, PAGE_SIZE), F32),
                            pltpu.SemaphoreType.DMA((2, 2))],
        ),
        out_shape=jax.ShapeDtypeStruct((db, t_valid, nh, 1, dh), F32),
        compiler_params=_cparams(("arbitrary", "arbitrary")),
        name="sample_attend",
    )(phys, q_s, k_new, v_new, cache_kt, cache_vt)


def _rwkv_feat_kernel(period, t_valid, tiles_per_seq, cur_ref, prev_ref, mu_ref, w0_ref, dup_ref, a0_ref,
                      aup_ref, gup_ref, kk_ref, ka_ref, rk_ref, seg_ref,
                      r_o, lw_o, k_o, v_o, kk_o, b_o, g_o, bonus_o):
    cur = cur_ref[...]
    if tiles_per_seq:
        edge = prev_ref[7:8, :]
        edge = jnp.where(pl.program_id(0) % tiles_per_seq == 0, 0.0, edge)
        first = lax.broadcasted_iota(I32, (cur.shape[0], 1), 0) == 0
        prev = jnp.where(first, edge, pltpu.roll(cur, 1, 0))
    else:
        prev = prev_ref[...]
    xs = cur + (prev - cur) * mu_ref[...]
    w = WIDTH_B
    r, k, v = xs[:, :w], xs[:, w:2 * w], xs[:, 2 * w:3 * w]
    dw = xs[:, 3 * w:3 * w + DECAY_LORA]
    da = xs[:, 3 * w + DECAY_LORA:3 * w + DECAY_LORA + AAA_LORA]
    dg = xs[:, 3 * w + DECAY_LORA + AAA_LORA:]
    lw = -DECAY_SCALE * jax.nn.sigmoid(w0_ref[...] + _dot3(jnp.tanh(dw), dup_ref[...]))
    a = jax.nn.sigmoid(a0_ref[...] + _dot3(da, aup_ref[...]))
    g = _dot3(jax.nn.sigmoid(dg), gup_ref[...])
    seg = seg_ref[...]
    kkr = k * kk_ref[...]
    kk = kkr / jnp.maximum(jnp.sqrt(_dot2_exact_rhs(kkr * kkr, seg)), 1e-12)
    k2 = k * (1.0 + (a - 1.0) * ka_ref[...])
    bonus = _dot2_exact_rhs(r * k2 * rk_ref[...], seg) * v
    if t_valid < period:
        rows = lax.broadcasted_iota(I32, (cur.shape[0], 1), 0)
        valid = (rows % period) < t_valid
        lw = jnp.where(valid, lw, 0.0)
        kk = jnp.where(valid, kk, 0.0)
        k2 = jnp.where(valid, k2, 0.0)
    r_o[...] = r
    lw_o[...] = lw
    k_o[...] = k2
    v_o[...] = v
    kk_o[...] = kk
    b_o[...] = kk * a
    g_o[...] = g
    bonus_o[...] = bonus


def _rwkv_features(p_cur, p_prev, prm, tm, period, t_valid, tiles_per_seq=0):
    n = p_cur.shape[0]
    row = lambda v: v.reshape(1, -1).astype(F32)
    const = lambda i: (0, 0)
    params = [row(prm['rw_mu']), row(prm['rw_w0']), prm['rw_decay_up'], row(prm['rw_a0']), prm['rw_a_up'],
              prm['rw_g_up'], row(prm['rw_k_k']), row(prm['rw_k_a']), row(prm['rw_r_k']),
              _seg_ones(WIDTH_B, HEAD_DIM_B)]
    tok = pl.BlockSpec((tm, RWKV_COLS), lambda i: (i, 0))
    if tiles_per_seq:
        prev_spec = pl.BlockSpec((8, RWKV_COLS), lambda i: (jnp.maximum(i * (tm // 8) - 1, 0), 0))
    else:
        prev_spec = tok
    out = pl.BlockSpec((tm, WIDTH_B), lambda i: (i, 0))
    return pl.pallas_call(
        functools.partial(_rwkv_feat_kernel, period, t_valid, tiles_per_seq),
        grid=(n // tm,),
        in_specs=[tok, prev_spec] + [pl.BlockSpec(p.shape, const) for p in params],
        out_specs=[out] * 8,
        out_shape=[jax.ShapeDtypeStruct((n, WIDTH_B), F32)] * 8,
        compiler_params=_cparams(("parallel",)),
        name="rwkv_features",
    )(p_cur, p_prev, *params)


def _rwkv_chunk_kernel(r_ref, lw_ref, k_ref, v_ref, kk_ref, b_ref, g_ref, bonus_ref, s0_ref,
                       lnw_ref, lnb_ref, seg_ref, o_ref, sT_ref, st_sc):
    c = pl.program_id(1)
    L = r_ref.shape[1]
    nh, dh = N_HEADS_B, HEAD_DIM_B

    @pl.when(c == 0)
    def _():
        st_sc[...] = s0_ref[0]

    lw = lw_ref[0]
    ri = lax.broadcasted_iota(I32, (L, L), 0)
    ci = lax.broadcasted_iota(I32, (L, L), 1)
    strict = ri > ci
    incl = ri >= ci
    tri = jnp.where(incl, 1.0, 0.0).astype(BF16)
    cum = _dot2_exact_rhs_left(tri, lw)
    cum_last = cum[L - 1:L, :]
    e_pos = jnp.exp(cum)
    e_neg = jnp.exp(-cum)
    e_prev = jnp.exp(cum - lw)
    e_tail = jnp.exp(cum_last - cum)
    g_last = jnp.exp(cum_last)
    kk = kk_ref[0]
    alpha_all = kk * e_prev
    beta_all = b_ref[0] * e_neg
    kappa_all = k_ref[0] * e_neg
    rho_all = r_ref[0] * e_pos
    beta_t_all = b_ref[0] * e_tail
    kappa_t_all = k_ref[0] * e_tail
    v_all = v_ref[0]
    eye_l = jnp.where(ri == ci, 1.0, 0.0)
    rk = lax.broadcasted_iota(I32, (dh, dh), 0)
    ck = lax.broadcasted_iota(I32, (dh, dh), 1)

    hs = range(nh)
    cut = lambda x: [x[:, h * dh:(h + 1) * dh] for h in hs]
    alpha, beta, kappa, rho, vh = cut(alpha_all), cut(beta_all), cut(kappa_all), cut(rho_all), cut(v_all)
    beta_t, kappa_t = cut(beta_t_all), cut(kappa_t_all)
    wcat = [jnp.concatenate([beta[h], kappa[h]], axis=0) for h in hs]
    za = [_dot3(alpha[h], wcat[h], 1, 1) for h in hs]
    zr = [_dotb(rho[h], wcat[h], 1, 1) for h in hs]
    n_mat = [jnp.where(strict, za[h][:, :L], 0.0) for h in hs]
    m_mat = [jnp.where(strict, za[h][:, L:], 0.0) for h in hs]
    ri2 = lax.broadcasted_iota(I32, (L, 2 * L), 0)
    ci2 = lax.broadcasted_iota(I32, (L, 2 * L), 1)
    incl2 = ri2 >= jnp.where(ci2 >= L, ci2 - L, ci2)
    nrmr = [jnp.where(incl2, zr[h], 0.0) for h in hs]
    mv = [_dot3(m_mat[h], vh[h]) for h in hs]
    d = [eye_l - jnp.where(ri // 2 == ci // 2, n_mat[h], 0.0) for h in hs]
    s = 2
    while s < L:
        lower_left = (ri // (2 * s) == ci // (2 * s)) & ((ri % (2 * s)) >= s) & ((ci % (2 * s)) < s)
        de = [_dot3(d[h], jnp.where(lower_left, n_mat[h], 0.0)) for h in hs]
        d = [d[h] - _dot3(de[h], d[h]) for h in hs]
        s *= 2
    ta = [_dot3(d[h], jnp.concatenate([alpha[h], mv[h]], axis=1)) for h in hs]
    abar = [ta[h][:, :dh] for h in hs]
    pv = [jnp.concatenate([-ta[h][:, dh:], vh[h]], axis=0) for h in hs]
    rpp = [rho[h] - _dotb(nrmr[h][:, :L], abar[h]) for h in hs]
    y0 = [_dotb(nrmr[h], pv[h]) for h in hs]
    gt = [jnp.where(rk == ck, g_last[:, h * dh:(h + 1) * dh], 0.0) - _dot3(beta_t[h], abar[h], 0, 0)
          for h in hs]
    ht = [_dot3(jnp.concatenate([beta_t[h], kappa_t[h]], axis=0), pv[h], 0, 0) for h in hs]
    upd = [_dot3(jnp.concatenate([rpp[h], gt[h]], axis=0), st_sc[h]) for h in hs]
    for h in hs:
        st_sc[h] = upd[h][L:] + ht[h]
    y = jnp.concatenate([y0[h] + upd[h][:L] for h in hs], axis=1)
    seg = seg_ref[...]
    mu = _dot2_exact_rhs(y, seg) * (1.0 / dh)
    yc = y - mu
    var = _dot2_exact_rhs(yc * yc, seg) * (1.0 / dh)
    yn = yc * lax.rsqrt(var + LN_X_EPS) * lnw_ref[...] + lnb_ref[...]
    o_ref[0] = ((yn + bonus_ref[0]) * g_ref[0]).astype(o_ref.dtype)
    sT_ref[0] = st_sc[...]


def _dot2_exact_rhs_left(m_bf16, x):
    xh, xl = _split(x)
    return _dg(m_bf16, xh, 1, 0) + _dg(m_bf16, xl, 1, 0)


def _rwkv_chunked(feats, s0_t, ln_w, ln_b, chunk):
    r, lw, k2, v, kk, b, g, bonus = feats
    bq, s, w = r.shape
    nc = s // chunk
    tok = pl.BlockSpec((1, chunk, w), lambda bi, c: (bi, c, 0))
    st = pl.BlockSpec((1, N_HEADS_B, HEAD_DIM_B, HEAD_DIM_B), lambda bi, c: (bi, 0, 0, 0))
    const = lambda bi, c: (0, 0)
    return pl.pallas_call(
        _rwkv_chunk_kernel,
        grid=(bq, nc),
        in_specs=[tok] * 8 + [st, pl.BlockSpec((1, w), const), pl.BlockSpec((1, w), const),
                              pl.BlockSpec((w, w), const)],
        out_specs=[tok, st],
        out_shape=[jax.ShapeDtypeStruct((bq, s, w), BF16),
                   jax.ShapeDtypeStruct((bq, N_HEADS_B, HEAD_DIM_B, HEAD_DIM_B), F32)],
        scratch_shapes=[pltpu.VMEM((N_HEADS_B, HEAD_DIM_B, HEAD_DIM_B), F32)],
        compiler_params=_cparams(("parallel", "arbitrary")),
        name="rwkv_chunked",
    )(r, lw, k2, v, kk, b, g, bonus, s0_t, ln_w.reshape(1, w), ln_b.reshape(1, w),
      _seg_ones(w, HEAD_DIM_B))


def _mem_kv_kernel(mem_ref, g_ref, w_ref, kg_ref, mk_ref, mv_ref):
    h = _rms(mem_ref[0], g_ref[...]).astype(BF16)
    kv = _dg(h, w_ref[...], 1, 0)
    for hm in range(N_HEADS_M):
        sl = slice(hm * HEAD_DIM_M, (hm + 1) * HEAD_DIM_M)
        mk_ref[0, :, sl] = _rms(kv[:, sl], kg_ref[...])
    mv_ref[0] = kv[:, WIDTH_M:]


def _memory_kv(mem, norm_mem, w_mem_kv, k_norm_m):
    b, m, d = mem.shape
    const = lambda i: (0, 0)
    out = pl.BlockSpec((1, m, WIDTH_M), lambda i: (i, 0, 0))
    return pl.pallas_call(
        _mem_kv_kernel,
        grid=(b,),
        in_specs=[pl.BlockSpec((1, m, d), lambda i: (i, 0, 0)), pl.BlockSpec((1, d), const),
                  pl.BlockSpec((d, 2 * WIDTH_M), const), pl.BlockSpec((1, HEAD_DIM_M), const)],
        out_specs=[out, out],
        out_shape=[jax.ShapeDtypeStruct((b, m, WIDTH_M), F32)] * 2,
        compiler_params=_cparams(("parallel",)),
        name="memory_kv",
    )(mem, norm_mem.reshape(1, d), w_mem_kv.astype(BF16), k_norm_m.reshape(1, HEAD_DIM_M))


def _mem_attend_kernel(q_ref, mk_ref, mv_ref, g_ref, o_ref):
    q = q_ref[0]
    for hm in range(N_HEADS_M):
        sl = slice(hm * HEAD_DIM_M, (hm + 1) * HEAD_DIM_M)
        qh = (_rms(q[:, sl], g_ref[...]) * (HEAD_DIM_M ** -0.5)).astype(BF16)
        s = _dg(qh, mk_ref[0, :, sl].astype(BF16), 1, 1)
        p = jnp.exp(s - jnp.max(s, axis=1, keepdims=True))
        o = _dg(p.astype(BF16), mv_ref[0, :, sl].astype(BF16), 1, 0) / jnp.sum(p, axis=1, keepdims=True)
        o_ref[0, :, sl] = o.astype(o_ref.dtype)


def _memory_attend(q_m, mk, mv, q_norm_m, tq):
    b, s, w = q_m.shape
    m = mk.shape[1]
    kv = pl.BlockSpec((1, m, w), lambda bi, j: (bi, 0, 0))
    return pl.pallas_call(
        _mem_attend_kernel,
        grid=(b, s // tq),
        in_specs=[pl.BlockSpec((1, tq, w), lambda bi, j: (bi, j, 0)), kv, kv,
                  pl.BlockSpec((1, HEAD_DIM_M), lambda bi, j: (0, 0))],
        out_specs=pl.BlockSpec((1, tq, w), lambda bi, j: (bi, j, 0)),
        out_shape=jax.ShapeDtypeStruct((b, s, w), BF16),
        compiler_params=_cparams(("parallel", "parallel")),
        name="memory_attend",
    )(q_m, mk, mv, q_norm_m.reshape(1, HEAD_DIM_M))


def _merge_kernel(x_ref, oa_ref, ob_ref, om_ref, gl_ref, wb_ref, wo_ref, ng_ref, rw_ref, rb_ref,
                  x1_ref, h_ref, e_ref, gate_ref):
    d = x_ref.shape[1]
    gl = gl_ref[...]
    merged = jnp.zeros(x_ref.shape, F32)
    for n, o_ref in enumerate((oa_ref, ob_ref, om_ref)):
        y = _dg(o_ref[...], wb_ref[n], 1, 0)
        merged = merged + jax.nn.sigmoid(gl[:, n * d:(n + 1) * d].astype(F32)) * y
    x1 = x_ref[...] + _dg(merged.astype(BF16), wo_ref[...], 1, 0)
    x1_ref[...] = x1
    hn = _rms(x1, ng_ref[...])
    h_ref[...] = hn
    logits = _dot3(hn, rw_ref[...]) + rb_ref[...]
    tm, ne = logits.shape
    e_idx = lax.broadcasted_iota(I32, (tm, ne), 1)
    vals, idxs = [], []
    g = logits
    for _ in range(TOP_K):
        mx = jnp.max(g, axis=1, keepdims=True)
        first = jnp.min(jnp.where(g == mx, e_idx, ne), axis=1, keepdims=True)
        vals.append(mx)
        idxs.append(first)
        g = jnp.where(e_idx == first, -jnp.inf, g)
    top = jnp.concatenate(vals, axis=1)
    pe = jnp.exp(top - vals[0])
    gate_ref[...] = pe / jnp.sum(pe, axis=1, keepdims=True)
    e_ref[...] = jnp.concatenate(idxs, axis=1)


def _merge_and_route(x2, o_a, o_b, o_m, gl, prm, tm):
    n, d = x2.shape
    const2 = lambda i: (0, 0)
    tok = lambda wd: pl.BlockSpec((tm, wd), lambda i: (i, 0))
    return pl.pallas_call(
        _merge_kernel,
        grid=(n // tm,),
        in_specs=[tok(d), tok(512), tok(512), tok(512), tok(3 * d),
                  pl.BlockSpec((3, 512, d), lambda i: (0, 0, 0)), pl.BlockSpec((d, d), const2),
                  pl.BlockSpec((1, d), const2), pl.BlockSpec((d, N_EXPERTS), const2),
                  pl.BlockSpec((1, N_EXPERTS), const2)],
        out_specs=[tok(d), tok(d), tok(TOP_K), tok(TOP_K)],
        out_shape=[jax.ShapeDtypeStruct((n, d), F32), jax.ShapeDtypeStruct((n, d), F32),
                   jax.ShapeDtypeStruct((n, TOP_K), I32), jax.ShapeDtypeStruct((n, TOP_K), F32)],
        compiler_params=_cparams(("parallel",)),
        name="merge_route",
    )(x2, o_a, o_b, o_m, gl, prm['w_branch'].astype(BF16), prm['w_out'].astype(BF16),
      prm['norm_ffn'].reshape(1, d), prm['router_w'], prm['router_b'].reshape(1, N_EXPERTS))


def _onehots(e):
    tm = e.shape[0]
    e_idx = lax.broadcasted_iota(I32, (tm, N_EXPERTS), 1)
    return [jnp.where(e[:, k:k + 1] == e_idx, 1.0, 0.0) for k in range(TOP_K)]


def _moe_rank_kernel(e_ref, rank_ref, cnt_ref, base_sc):
    i = pl.program_id(0)

    @pl.when(i == 0)
    def _():
        base_sc[...] = jnp.zeros_like(base_sc)

    ohs = _onehots(e_ref[...])
    cnt = ohs[0] + ohs[1] + ohs[2] + ohs[3]
    tm = cnt.shape[0]
    ri = lax.broadcasted_iota(I32, (tm, tm), 0)
    ci = lax.broadcasted_iota(I32, (tm, tm), 1)
    tri = jnp.where(ri > ci, 1.0, 0.0).astype(BF16)
    tot = _dg(tri, cnt.astype(BF16), 1, 0) + base_sc[...]
    rank_ref[...] = jnp.concatenate([jnp.sum(oh * tot, axis=1, keepdims=True) for oh in ohs],
                                    axis=1).astype(I32)
    base_sc[...] = base_sc[...] + jnp.sum(cnt, axis=0, keepdims=True)
    cnt_ref[...] = base_sc[...].astype(I32)


def _moe_rank(top_e, tm):
    n = top_e.shape[0]
    return pl.pallas_call(
        _moe_rank_kernel,
        grid=(n // tm,),
        in_specs=[pl.BlockSpec((tm, TOP_K), lambda i: (i, 0))],
        out_specs=[pl.BlockSpec((tm, TOP_K), lambda i: (i, 0)), pl.BlockSpec((1, N_EXPERTS), lambda i: (0, 0))],
        out_shape=[jax.ShapeDtypeStruct((n, TOP_K), I32), jax.ShapeDtypeStruct((1, N_EXPERTS), I32)],
        scratch_shapes=[pltpu.VMEM((1, N_EXPERTS), F32)],
        compiler_params=_cparams(("arbitrary",)),
        name="moe_rank",
    )(top_e)


def _moe_rows_kernel(e_ref, rank_ref, start_ref, row_ref):
    ohs = _onehots(e_ref[...])
    st = start_ref[...].astype(F32)
    base = jnp.concatenate([jnp.sum(oh * st, axis=1, keepdims=True) for oh in ohs], axis=1)
    row_ref[...] = rank_ref[...] + base.astype(I32)


def _moe_rows(top_e, rank, starts, tm):
    n = top_e.shape[0]
    tok = pl.BlockSpec((tm, TOP_K), lambda i: (i, 0))
    return pl.pallas_call(
        _moe_rows_kernel,
        grid=(n // tm,),
        in_specs=[tok, tok, pl.BlockSpec((1, N_EXPERTS), lambda i: (0, 0))],
        out_specs=tok,
        out_shape=jax.ShapeDtypeStruct((n, TOP_K), I32),
        compiler_params=_cparams(("parallel",)),
        name="moe_rows",
    )(top_e, rank, starts)


def _dispatch_kernel(row_ref, h_ref, xs_in_ref, xs_ref, sem):
    del xs_in_ref
    tm = h_ref.shape[0]

    def start(t, carry):
        for k in range(TOP_K):
            r = row_ref[t * TOP_K + k]
            pltpu.make_async_copy(h_ref.at[pl.ds(t, 1)], xs_ref.at[pl.ds(r, 1)], sem).start()
        return carry

    lax.fori_loop(0, tm, start, 0, unroll=2)
    all_rows = xs_ref.at[pl.ds(0, tm * TOP_K)]
    pltpu.make_async_copy(all_rows, all_rows, sem).wait()


def _moe_dispatch(h, row_flat, n_rows, tm):
    n, d = h.shape
    xs0 = jnp.zeros((n_rows, d), F32)
    return pl.pallas_call(
        _dispatch_kernel,
        grid=(n // tm,),
        in_specs=[pl.BlockSpec((tm * TOP_K,), lambda i: (i,), memory_space=pltpu.SMEM),
                  pl.BlockSpec((tm, d), lambda i: (i, 0)),
                  pl.BlockSpec(memory_space=pl.ANY)],
        out_specs=pl.BlockSpec(memory_space=pl.ANY),
        out_shape=jax.ShapeDtypeStruct((n_rows, d), F32),
        scratch_shapes=[pltpu.SemaphoreType.DMA(())],
        input_output_aliases={2: 0},
        compiler_params=_cparams(("arbitrary",)),
        name="moe_dispatch",
    )(row_flat, h, xs0)


def _swiglu(u, d_ff):
    u_glu = jnp.minimum(u[:, :d_ff], SWIGLU_LIMIT)
    u_lin = jnp.clip(u[:, d_ff:], -SWIGLU_LIMIT, SWIGLU_LIMIT)
    return u_glu * jax.nn.sigmoid(SWIGLU_ALPHA * u_glu) * (u_lin + 1.0)


def _moe_ffn_kernel(be_ref, nu_ref, xs_ref, w1_ref, b1_ref, w2_ref, b2_ref, y_ref, w1_sc, w2_sc):
    i = pl.program_id(0)
    prev = be_ref[jnp.maximum(i - 1, 0)]
    first = (i == 0) | (be_ref[i] != prev)

    @pl.when(first)
    def _():
        w1_sc[...] = w1_ref[...].astype(BF16)
        w2_sc[...] = w2_ref[...].astype(BF16)

    @pl.when(i < nu_ref[0])
    def _():
        u = _dg(xs_ref[...].astype(BF16), w1_sc[...], 1, 0) + b1_ref[...]
        act = _swiglu(u, w2_ref.shape[0])
        y_ref[...] = _dg(act.astype(BF16), w2_sc[...], 1, 0) + b2_ref[...]

    @pl.when(i >= nu_ref[0])
    def _():
        y_ref[...] = jnp.zeros_like(y_ref)


def _moe_ffn(xs, blk_e, n_used, w1, b1, w2, b2):
    n_rows, d = xs.shape
    ne, _, f2 = w1.shape
    d_ff = w2.shape[1]
    nblk = n_rows // MOE_ROWS
    return pl.pallas_call(
        _moe_ffn_kernel,
        grid_spec=pltpu.PrefetchScalarGridSpec(
            num_scalar_prefetch=2,
            grid=(nblk,),
            in_specs=[pl.BlockSpec((MOE_ROWS, d), lambda i, be, nu: (i, 0)),
                      pl.BlockSpec((None, d, f2), lambda i, be, nu: (be[i], 0, 0)),
                      pl.BlockSpec((None, 1, f2), lambda i, be, nu: (be[i], 0, 0)),
                      pl.BlockSpec((None, d_ff, d), lambda i, be, nu: (be[i], 0, 0)),
                      pl.BlockSpec((None, 1, d), lambda i, be, nu: (be[i], 0, 0))],
            out_specs=pl.BlockSpec((MOE_ROWS, d), lambda i, be, nu: (i, 0)),
            scratch_shapes=[pltpu.VMEM((d, f2), BF16), pltpu.VMEM((d_ff, d), BF16)],
        ),
        out_shape=jax.ShapeDtypeStruct((n_rows, d), F32),
        compiler_params=_cparams(("arbitrary",)),
        name="moe_ffn",
    )(blk_e, n_used, xs, w1, b1.reshape(ne, 1, f2), w2, b2.reshape(ne, 1, d))


def _combine_kernel(row_ref, yb_ref, x1_ref, gate_ref, y_ref, buf, sem):
    tm = x1_ref.shape[0]

    def start(t, carry):
        for k in range(TOP_K):
            r = row_ref[t * TOP_K + k]
            pltpu.make_async_copy(yb_ref.at[pl.ds(r, 1)], buf.at[pl.ds(k * tm + t, 1)], sem).start()
        return carry

    lax.fori_loop(0, tm, start, 0, unroll=2)
    pltpu.make_async_copy(yb_ref.at[pl.ds(0, tm * TOP_K)], buf, sem).wait()
    gates = gate_ref[...]
    acc = buf[0:tm] * gates[:, 0:1]
    for k in range(1, TOP_K):
        acc = acc + buf[k * tm:(k + 1) * tm] * gates[:, k:k + 1]
    y_ref[...] = x1_ref[...] + acc


def _moe_combine(yb, row_flat, x1, gates, tm):
    n, d = x1.shape
    return pl.pallas_call(
        _combine_kernel,
        grid=(n // tm,),
        in_specs=[pl.BlockSpec((tm * TOP_K,), lambda i: (i,), memory_space=pltpu.SMEM),
                  pl.BlockSpec(memory_space=pl.ANY),
                  pl.BlockSpec((tm, d), lambda i: (i, 0)),
                  pl.BlockSpec((tm, TOP_K), lambda i: (i, 0))],
        out_specs=pl.BlockSpec((tm, d), lambda i: (i, 0)),
        out_shape=jax.ShapeDtypeStruct((n, d), F32),
        scratch_shapes=[pltpu.VMEM((TOP_K * tm, d), F32), pltpu.SemaphoreType.DMA(())],
        compiler_params=_cparams(("arbitrary",)),
        name="moe_combine",
    )(row_flat, yb, x1, gates)


def _moe_block(x1, hn, top_e, gates, prm, tm):
    n, d = x1.shape
    rank, counts = _moe_rank(top_e, tm)
    counts = counts[0]
    padded = (counts + MOE_ROWS - 1) // MOE_ROWS * MOE_ROWS
    p_end = jnp.cumsum(padded)
    starts = (p_end - padded).astype(I32)
    nblk = -(-(n * TOP_K + N_EXPERTS * (MOE_ROWS - 1)) // MOE_ROWS)
    n_used = (p_end[-1] // MOE_ROWS).astype(I32)
    blk_i = jnp.minimum(jnp.arange(nblk, dtype=I32), n_used - 1)
    n_before = jnp.sum((p_end[None, :] <= (blk_i * MOE_ROWS)[:, None]).astype(I32), axis=1)
    blk_e = jnp.minimum(n_before, N_EXPERTS - 1).astype(I32)
    row = _moe_rows(top_e, rank, starts[None, :], tm)
    row_flat = row.reshape(-1)
    xs = _moe_dispatch(hn, row_flat, nblk * MOE_ROWS, tm)
    yb = _moe_ffn(xs, blk_e, n_used[None], prm['moe_w1'], prm['moe_b1'], prm['moe_w2'], prm['moe_b2'])
    return _moe_combine(yb, row_flat, x1, gates, tm)


def _split_w_in(w_in):
    d = w_in.shape[0]
    a = 3 * WIDTH_A
    b = a + RWKV_COLS
    c = b + WIDTH_M
    wb = w_in.astype(BF16)
    return [wb[:, :a], wb[:, a:b], wb[:, b:c], wb[:, c:]]


def _token_mix_tail(x2, o_a, o_b, o_m, gl, prm, tm):
    x1, hn, top_e, gates = _merge_and_route(x2, o_a, o_b, o_m, gl, prm, tm)
    return _moe_block(x1, hn, top_e, gates, prm, tm)


def _layer_prompt(x, mem, prm, ws):
    b, s, d = x.shape
    tm = 256
    x2 = x.reshape(b * s, d)
    qkv, p_rw, q_m, gl = _in_projection(x2, prm['norm_mix'].reshape(1, d), ws, tm)
    pos = jnp.arange(s, dtype=I32)
    q_s, k_t, v_t, kaug, vaug, kmean = _moba_prep(qkv.reshape(b, s, -1), pos, prm['q_norm_a'], prm['k_norm_a'],
                                                  MOBA_BLOCK, True)
    k_o, v_o = jnp.swapaxes(k_t, 2, 3), jnp.swapaxes(v_t, 2, 3)
    o_a = _moba_flash(q_s, kaug, vaug, kmean).reshape(b * s, WIDTH_A)

    p3 = p_rw.reshape(b, s, RWKV_COLS)
    feats = _rwkv_features(p_rw, p_rw, prm, tm, 1, 1, tiles_per_seq=s // tm)
    feats = [f.reshape(b, s, WIDTH_B) for f in feats]
    s0_t = jnp.zeros((b, N_HEADS_B, HEAD_DIM_B, HEAD_DIM_B), F32)
    o_b, st = _rwkv_chunked(feats, s0_t, prm['ln_x_w'], prm['ln_x_b'], RWKV_CHUNK)
    wkv = jnp.swapaxes(st, 2, 3)

    mk, mv = _memory_kv(mem, prm['norm_mem'], prm['w_mem_kv'], prm['k_norm_m'])
    o_m = _memory_attend(q_m.reshape(b, s, WIDTH_M), mk, mv, prm['q_norm_m'], 512)

    y = _token_mix_tail(x2, o_a, o_b.reshape(b * s, WIDTH_B), o_m.reshape(b * s, WIDTH_M), gl, prm, tm)
    m = mem.shape[1]
    return (y.reshape(b, s, d), k_o, v_o, wkv, p3[:, -1],
            mk.reshape(b, m, N_HEADS_M, HEAD_DIM_M), mv.reshape(b, m, N_HEADS_M, HEAD_DIM_M))


def _layer_sample(x, cache_k, cache_v, mem_k, mem_v, wkv0, shift0, page_table, layer, prm, ws):
    db, t, d = x.shape
    tp = SAMPLE_T_PAD
    past_len = page_table.shape[1] * PAGE_SIZE
    assert past_len % MOBA_BLOCK == 0 and t <= tp
    ppb = MOBA_BLOCK // PAGE_SIZE
    n = db * tp
    x2 = jnp.pad(x, ((0, 0), (0, tp - t), (0, 0))).reshape(n, d)
    qkv, p_rw, q_m, gl = _in_projection(x2, prm['norm_mix'].reshape(1, d), ws, n)
    pos = past_len + jnp.arange(tp, dtype=I32)
    q_s, k_o, v_o = _moba_prep(qkv.reshape(db, tp, -1), pos, prm['q_norm_a'], prm['k_norm_a'], tp, False)

    cache_kt, cache_vt = jnp.swapaxes(cache_k, 3, 4), jnp.swapaxes(cache_v, 3, 4)
    kmean_t = _page_means(cache_kt, page_table, layer)
    idx = _sample_select(q_s, kmean_t)
    idx = jnp.transpose(idx[:, :, :t], (0, 2, 1, 3))
    logical = idx[..., None] * ppb + jnp.arange(ppb, dtype=I32)
    phys = page_table[jnp.arange(db)[:, None, None, None, None], logical].reshape(-1).astype(I32)
    o_a = _sample_attend(q_s, k_o, v_o, cache_kt, cache_vt, phys, t, layer)
    o_a = jnp.pad(o_a.reshape(db, t, WIDTH_A), ((0, 0), (0, tp - t), (0, 0))).reshape(n, WIDTH_A).astype(BF16)

    p3 = p_rw.reshape(db, tp, RWKV_COLS)
    p_prev = jnp.concatenate([shift0[:, None, :], p3[:, :-1]], axis=1).reshape(n, RWKV_COLS)
    feats = _rwkv_features(p_rw, p_prev, prm, n, tp, t)
    feats = [f.reshape(db, tp, WIDTH_B) for f in feats]
    o_b, st = _rwkv_chunked(feats, jnp.swapaxes(wkv0, 2, 3), prm['ln_x_w'], prm['ln_x_b'], tp)
    wkv = jnp.swapaxes(st, 2, 3)

    m = mem_k.shape[1]
    o_m = _memory_attend(q_m.reshape(db, tp, WIDTH_M), mem_k.reshape(db, m, WIDTH_M),
                         mem_v.reshape(db, m, WIDTH_M), prm['q_norm_m'], tp)

    y = _token_mix_tail(x2, o_a, o_b.reshape(n, WIDTH_B), o_m.reshape(n, WIDTH_M), gl, prm, n)
    return (y.reshape(db, tp, d)[:, :t], k_o[:, :, :t], v_o[:, :, :t], wkv, p3[:, t - 1])


def kernel(x_prompt, x_sample, mem_prompt, cache_k, cache_v, cache_mem_k, cache_mem_v, state_wkv, state_shift,
           page_table, norm_mix, norm_mem, norm_ffn, w_in, q_norm_a, k_norm_a, q_norm_m, k_norm_m, w_mem_kv,
           rw_mu, rw_w0, rw_decay_up, rw_a0, rw_a_up, rw_g_up, rw_k_k, rw_k_a, rw_r_k, ln_x_w, ln_x_b,
           w_branch, w_out, router_w, router_b, moe_w1, moe_b1, moe_w2, moe_b2):
    depth = w_in.shape[0]
    xp, xs = x_prompt, x_sample
    outs_p = [[] for _ in range(6)]
    outs_s = [[] for _ in range(4)]
    for l in range(depth):
        prm = dict(norm_mix=norm_mix[l], norm_mem=norm_mem[l], norm_ffn=norm_ffn[l], w_in=w_in[l],
                   q_norm_a=q_norm_a[l], k_norm_a=k_norm_a[l], q_norm_m=q_norm_m[l], k_norm_m=k_norm_m[l],
                   w_mem_kv=w_mem_kv[l], rw_mu=rw_mu[l], rw_w0=rw_w0[l], rw_decay_up=rw_decay_up[l],
                   rw_a0=rw_a0[l], rw_a_up=rw_a_up[l], rw_g_up=rw_g_up[l], rw_k_k=rw_k_k[l], rw_k_a=rw_k_a[l],
                   rw_r_k=rw_r_k[l].reshape(-1), ln_x_w=ln_x_w[l], ln_x_b=ln_x_b[l], w_branch=w_branch[l],
                   w_out=w_out[l], router_w=router_w[l], router_b=router_b[l], moe_w1=moe_w1[l],
                   moe_b1=moe_b1[l], moe_w2=moe_w2[l], moe_b2=moe_b2[l])
        ws = _split_w_in(prm['w_in'])
        xp, *rest_p = _layer_prompt(xp, mem_prompt, prm, ws)
        for acc, val in zip(outs_p, rest_p):
            acc.append(val)
        xs, *rest_s = _layer_sample(xs, cache_k, cache_v, cache_mem_k[l], cache_mem_v[l], state_wkv[l],
                                    state_shift[l], page_table, l, prm, ws)
        for acc, val in zip(outs_s, rest_s):
            acc.append(val)
    kp, vp, wkvp, shp, mkp, mvp = [jnp.stack(a) for a in outs_p]
    ksm, vsm, wkvs, shs = [jnp.stack(a) for a in outs_s]
    return (xp, xs, kp, vp, wkvp, shp, mkp, mvp, ksm, vsm, wkvs, shs)
```

```python
import functools
import math

import jax
import jax.numpy as jnp
from jax import lax
from jax.experimental import pallas as pl
from jax.experimental.pallas import tpu as pltpu

F32 = jnp.float32
BF16 = jnp.bfloat16
I32 = jnp.int32

N_HEADS_A = 8
HEAD_DIM_A = 64
WIDTH_A = 512
MOBA_BLOCK = 256
MOBA_TOPK = 3
ROT_DIM = 16
ROPE_THETA = 500000.0
PAGE_SIZE = 128
N_HEADS_B = 8
HEAD_DIM_B = 64
WIDTH_B = 512
DECAY_LORA = 64
AAA_LORA = 64
GATE_LORA = 128
RWKV_COLS = 1792
DECAY_SCALE = math.exp(-0.5)
LN_X_EPS = 64e-5
N_HEADS_M = 4
HEAD_DIM_M = 128
WIDTH_M = 512
N_EXPERTS = 32
TOP_K = 4
SWIGLU_ALPHA = 1.702
SWIGLU_LIMIT = 7.0
NORM_EPS = 1e-6

NEG_BIG = -1e30
SAMPLE_T_PAD = 8
RWKV_CHUNK = 64
RWKV_CHUNKS_PER_STEP = 2
MOE_ROWS = 256
VMEM_LIMIT = 56 * 1024 * 1024


def _cparams(sem, vmem=None):
    return pltpu.CompilerParams(dimension_semantics=sem, vmem_limit_bytes=vmem or VMEM_LIMIT)


def _dg(a, b, ca, cb):
    return lax.dot_general(a, b, (((ca,), (cb,)), ((), ())), preferred_element_type=F32)


def _split(x):
    hi = x.astype(BF16)
    lo = (x - hi.astype(F32)).astype(BF16)
    return hi, lo


def _dot3(a, b, ca=1, cb=0):
    ah, al = _split(a)
    bh, bl = _split(b)
    return _dg(ah, bh, ca, cb) + (_dg(ah, bl, ca, cb) + _dg(al, bh, ca, cb))


def _dot2_exact_rhs(a, b_bf16):
    ah, al = _split(a)
    return _dg(ah, b_bf16, 1, 0) + _dg(al, b_bf16, 1, 0)


def _dotb(a, b, ca=1, cb=0):
    return _dg(a.astype(BF16), b.astype(BF16), ca, cb)


def _rms(x, gain_row):
    ms = jnp.mean(x * x, axis=-1, keepdims=True)
    return x * lax.rsqrt(ms + NORM_EPS) * gain_row


def _seg_ones(width, seg):
    r = lax.broadcasted_iota(I32, (width, width), 0) // seg
    c = lax.broadcasted_iota(I32, (width, width), 1) // seg
    return jnp.where(r == c, 1.0, 0.0).astype(BF16)


def _proj_kernel(x_ref, g_ref, w1, w2, w3, w4, o1, o2, o3, o4):
    h = _rms(x_ref[...], g_ref[...]).astype(BF16)
    o1[...] = _dg(h, w1[...], 1, 0)
    o2[...] = _dg(h, w2[...], 1, 0)
    o3[...] = _dg(h, w3[...], 1, 0)
    o4[...] = _dg(h, w4[...], 1, 0).astype(o4.dtype)


def _in_projection(x2, gain, ws, tm):
    n, d = x2.shape
    widths = [w.shape[1] for w in ws]
    dtypes = [F32, F32, F32, BF16]
    const = lambda i: (0, 0)
    return pl.pallas_call(
        _proj_kernel,
        grid=(n // tm,),
        in_specs=[pl.BlockSpec((tm, d), lambda i: (i, 0)), pl.BlockSpec((1, d), const)]
        + [pl.BlockSpec((d, wd), const) for wd in widths],
        out_specs=[pl.BlockSpec((tm, wd), lambda i: (i, 0)) for wd in widths],
        out_shape=[jax.ShapeDtypeStruct((n, wd), dt) for wd, dt in zip(widths, dtypes)],
        compiler_params=_cparams(("parallel",)),
        name="in_projection",
    )(x2, gain, *ws)


def _rope_tables(pos):
    half = ROT_DIM // 2
    inv_freq = 1.0 / (ROPE_THETA ** (jnp.arange(0, ROT_DIM, 2, dtype=F32) / ROT_DIM))
    ang = pos.astype(F32)[:, None] * inv_freq[None, :]
    cos, sin = jnp.cos(ang), jnp.sin(ang)
    n = pos.shape[0]
    rest = HEAD_DIM_A - ROT_DIM
    c = jnp.concatenate([cos, cos, jnp.ones((n, rest), F32)], axis=1)
    s_up = jnp.concatenate([-sin, jnp.zeros((n, half + rest), F32)], axis=1)
    s_dn = jnp.concatenate([jnp.zeros((n, half), F32), sin, jnp.zeros((n, rest), F32)], axis=1)
    two = lambda t: jnp.concatenate([t, t], axis=1)
    return two(c), two(s_up), two(s_dn)


def _norm_rope(x, seg, gain, c, s_up, s_dn):
    ss = _dot2_exact_rhs(x * x, seg)
    y = x * lax.rsqrt(ss * (1.0 / HEAD_DIM_A) + NORM_EPS) * gain
    half = ROT_DIM // 2
    up = pltpu.roll(y, WIDTH_A - half, 1)
    dn = pltpu.roll(y, half, 1)
    return y * c + up * s_up + dn * s_dn


def _moba_prep_kernel(with_blocks, qkv_ref, seg_ref, qg_ref, kg_ref, c_ref, su_ref, sd_ref, *outs):
    if with_blocks:
        qs_ref, kt_ref, vt_ref, kaug_ref, vaug_ref, kmean_ref = outs
    else:
        qs_ref, k_ref, v_ref = outs
    x = qkv_ref[0]
    tm = x.shape[0]
    rep = lambda r: jnp.concatenate([r[...]] * (WIDTH_A // 128), axis=1)
    c, su, sd = rep(c_ref), rep(su_ref), rep(sd_ref)
    seg = seg_ref[...]
    q = _norm_rope(x[:, :WIDTH_A], seg, qg_ref[...], c, su, sd) * (HEAD_DIM_A ** -0.5)
    k = _norm_rope(x[:, WIDTH_A:2 * WIDTH_A], seg, kg_ref[...], c, su, sd)
    v = x[:, 2 * WIDTH_A:]
    if with_blocks:
        blk = pl.program_id(1)
        lane = lax.broadcasted_iota(I32, (tm, HEAD_DIM_A), 1)
        onehot = jnp.where(lane == blk, 1.0, 0.0).astype(BF16)
        ones_col = jnp.where(lane == 0, 1.0, 0.0).astype(BF16)

        @pl.when(blk == 0)
        def _():
            kmean_ref[...] = jnp.zeros_like(kmean_ref)

    for h in range(N_HEADS_A):
        sl = slice(h * HEAD_DIM_A, (h + 1) * HEAD_DIM_A)
        qs_ref[0, h] = q[:, sl]
        if with_blocks:
            kt_ref[0, h] = k[:, sl].T
            vt_ref[0, h] = v[:, sl].T
            kaug_ref[0, h] = jnp.concatenate([k[:, sl].astype(BF16), onehot], axis=1)
            vaug_ref[0, h] = jnp.concatenate([v[:, sl].astype(BF16), ones_col], axis=1)
            kmean_ref[0, h, pl.ds(blk, 1), :] = jnp.mean(k[:, sl], axis=0, keepdims=True)
        else:
            k_ref[0, h] = k[:, sl]
            v_ref[0, h] = v[:, sl]


def _moba_prep(qkv, pos, q_gain, k_gain, tm, with_blocks):
    bq, s, _ = qkv.shape
    nb = s // tm
    c, su, sd = _rope_tables(pos)
    seg = _seg_ones(WIDTH_A, HEAD_DIM_A)
    tile8 = lambda g: jnp.tile(g.astype(F32), N_HEADS_A)[None, :]
    hm = jax.ShapeDtypeStruct((bq, N_HEADS_A, s, HEAD_DIM_A), F32)
    hm_spec = pl.BlockSpec((1, N_HEADS_A, tm, HEAD_DIM_A), lambda b, j: (b, 0, j, 0))
    if with_blocks:
        assert nb <= HEAD_DIM_A
        tr = jax.ShapeDtypeStruct((bq, N_HEADS_A, HEAD_DIM_A, s), F32)
        tr_spec = pl.BlockSpec((1, N_HEADS_A, HEAD_DIM_A, tm), lambda b, j: (b, 0, 0, j))
        aug = jax.ShapeDtypeStruct((bq, N_HEADS_A, s, 128), BF16)
        aug_spec = pl.BlockSpec((1, N_HEADS_A, tm, 128), lambda b, j: (b, 0, j, 0))
        out_shape = [hm, tr, tr, aug, aug,
                     jax.ShapeDtypeStruct((bq, N_HEADS_A, HEAD_DIM_A, HEAD_DIM_A), F32)]
        out_specs = [hm_spec, tr_spec, tr_spec, aug_spec, aug_spec,
                     pl.BlockSpec((1, N_HEADS_A, HEAD_DIM_A, HEAD_DIM_A), lambda b, j: (b, 0, 0, 0))]
    else:
        out_shape = [hm, hm, hm]
        out_specs = [hm_spec, hm_spec, hm_spec]
    const = lambda b, j: (0, 0)
    tab = pl.BlockSpec((tm, 128), lambda b, j: (j, 0))
    return pl.pallas_call(
        functools.partial(_moba_prep_kernel, with_blocks),
        grid=(bq, nb),
        in_specs=[pl.BlockSpec((1, tm, 3 * WIDTH_A), lambda b, j: (b, j, 0)),
                  pl.BlockSpec((WIDTH_A, WIDTH_A), const),
                  pl.BlockSpec((1, WIDTH_A), const), pl.BlockSpec((1, WIDTH_A), const), tab, tab, tab],
        out_specs=out_specs,
        out_shape=out_shape,
        compiler_params=_cparams(("parallel", "arbitrary")),
        name="moba_prep",
    )(qkv, seg, tile8(q_gain), tile8(k_gain), c, su, sd)


FLASH_HEADS = 4


def _moba_flash_kernel(q_ref, kaug_ref, vaug_ref, kmean_ref, o_ref, m_sc, acc_sc):
    i = pl.program_id(1)
    hp = pl.program_id(2)
    tq, dh = q_ref.shape[2], q_ref.shape[3]
    gs = range(FLASH_HEADS)
    q = [q_ref[0, g] for g in gs]
    gate_t = [_dot3(kmean_ref[0, g], q[g], 1, 1) for g in gs]
    n_idx = lax.broadcasted_iota(I32, gate_t[0].shape, 0)
    n_tot = gate_t[0].shape[0]
    gv = [jnp.where(n_idx < i, gate_t[g], -jnp.inf) for g in gs]
    sel = [n_idx == i for g in gs]
    for _ in range(MOBA_TOPK):
        mx = [jnp.max(gv[g], axis=0, keepdims=True) for g in gs]
        cand = [(gv[g] == mx[g]) & (mx[g] > -jnp.inf) for g in gs]
        first = [jnp.min(jnp.where(cand[g], n_idx, n_tot), axis=0, keepdims=True) for g in gs]
        pick = [n_idx == first[g] for g in gs]
        sel = [sel[g] | pick[g] for g in gs]
        gv = [jnp.where(pick[g], -jnp.inf, gv[g]) for g in gs]
    zeros_t = jnp.zeros((dh, tq), F32)
    zeros_q = jnp.zeros((tq, dh), F32)
    bias = [jnp.concatenate([zeros_t, jnp.where(sel[g], 0.0, NEG_BIG)], axis=0).T for g in gs]
    qaug = [(jnp.concatenate([q[g], zeros_q], axis=1) + bias[g]).astype(BF16) for g in gs]

    start = pl.multiple_of(i * tq, tq)
    row = lax.broadcasted_iota(I32, (tq, tq), 0)
    col = lax.broadcasted_iota(I32, (tq, tq), 1)
    s = [jnp.where(col <= row, _dg(qaug[g], kaug_ref[0, g, pl.ds(start, tq), :], 1, 1), -jnp.inf) for g in gs]
    m0 = [jnp.max(s[g], axis=1, keepdims=True) for g in gs]
    for g in gs:
        m_sc[g] = jnp.broadcast_to(m0[g], (tq, 128))
        acc_sc[g] = _dg(jnp.exp(s[g] - m0[g]).astype(BF16), vaug_ref[0, g, pl.ds(start, tq), :], 1, 0)

    def step(off, width):
        sj = [_dg(qaug[g], kaug_ref[0, g, pl.ds(off, width), :], 1, 1) for g in gs]
        m_old = [m_sc[g] for g in gs]
        m_new = [jnp.maximum(m_old[g], jnp.max(sj[g], axis=1, keepdims=True)) for g in gs]
        pj = [jnp.exp(sj[g] - jnp.concatenate([m_new[g]] * (width // 128), axis=1)) for g in gs]
        for g in gs:
            acc_sc[g] = (jnp.exp(m_old[g] - m_new[g]) * acc_sc[g]
                         + _dg(pj[g].astype(BF16), vaug_ref[0, g, pl.ds(off, width), :], 1, 0))
            m_sc[g] = m_new[g]

    def quad(j, carry):
        step(pl.multiple_of(j * (4 * tq), 4 * tq), 4 * tq)
        return carry

    lax.fori_loop(0, i // 4, quad, 0)

    @pl.when(i % 4 >= 2)
    def _():
        step(pl.multiple_of((i // 4) * (4 * tq), 2 * tq), 2 * tq)

    @pl.when(i % 2 == 1)
    def _():
        step(pl.multiple_of((i - 1) * tq, tq), tq)

    outs = []
    for g in gs:
        acc = acc_sc[g]
        outs.append(acc[:, :dh] / acc[:, dh:dh + 1])
    out = jnp.concatenate(outs, axis=1).astype(o_ref.dtype)
    wd = FLASH_HEADS * dh
    for pp in range(N_HEADS_A // FLASH_HEADS):
        @pl.when(hp == pp)
        def _():
            o_ref[0, :, pp * wd:(pp + 1) * wd] = out


def _moba_flash(q_s, kaug, vaug, kmean):
    b, nh, s, dh = q_s.shape
    tq = MOBA_BLOCK
    nb = s // tq
    g = FLASH_HEADS
    return pl.pallas_call(
        _moba_flash_kernel,
        grid=(b, nb, nh // g),
        in_specs=[pl.BlockSpec((1, g, tq, dh), lambda bi, i, h: (bi, h, i, 0)),
                  pl.BlockSpec((1, g, s, 128), lambda bi, i, h: (bi, h, 0, 0)),
                  pl.BlockSpec((1, g, s, 128), lambda bi, i, h: (bi, h, 0, 0)),
                  pl.BlockSpec((1, g, dh, dh), lambda bi, i, h: (bi, h, 0, 0))],
        out_specs=pl.BlockSpec((1, tq, nh * dh), lambda bi, i, h: (bi, i, 0)),
        out_shape=jax.ShapeDtypeStruct((b, s, nh * dh), BF16),
        scratch_shapes=[pltpu.VMEM((g, tq, 128), F32), pltpu.VMEM((g, tq, 128), F32)],
        compiler_params=_cparams(("parallel", "parallel", "arbitrary")),
        name="moba_flash",
    )(q_s, kaug, vaug, kmean)


PAGES_PER_STEP = 32


def _page_mean_kernel(pt_ref, *refs):
    pages, out_ref = refs[:PAGES_PER_STEP], refs[PAGES_PER_STEP]
    s = pl.program_id(1)
    ppb = MOBA_BLOCK // PAGE_SIZE
    bps = PAGES_PER_STEP // ppb

    @pl.when(s == 0)
    def _():
        out_ref[...] = jnp.zeros_like(out_ref)

    lane = lax.broadcasted_iota(I32, out_ref.shape[2:], 1)
    for h in range(N_HEADS_A):
        acc = out_ref[0, h]
        for j in range(bps):
            tot = pages[ppb * j][h]
            for u in range(1, ppb):
                tot = tot + pages[ppb * j + u][h]
            col = jnp.sum(tot, axis=1, keepdims=True) * (1.0 / MOBA_BLOCK)
            acc = jnp.where(lane == s * bps + j, col, acc)
        out_ref[0, h] = acc


def _page_means(cache_kt, page_table, layer):
    db, n_pages = page_table.shape
    ppb = MOBA_BLOCK // PAGE_SIZE
    n_full = n_pages // ppb
    steps = n_full * ppb // PAGES_PER_STEP
    _, _, nh, dh, pg = cache_kt.shape

    def page_spec(u):
        return pl.BlockSpec((None, None, nh, dh, pg),
                            lambda b, s, pt: (layer, pt[b, s * PAGES_PER_STEP + u], 0, 0, 0))

    return pl.pallas_call(
        _page_mean_kernel,
        grid_spec=pltpu.PrefetchScalarGridSpec(
            num_scalar_prefetch=1,
            grid=(db, steps),
            in_specs=[page_spec(u) for u in range(PAGES_PER_STEP)],
            out_specs=pl.BlockSpec((1, nh, dh, n_full), lambda b, s, pt: (b, 0, 0, 0)),
        ),
        out_shape=jax.ShapeDtypeStruct((db, nh, dh, n_full), F32),
        compiler_params=_cparams(("parallel", "arbitrary")),
        name="page_means",
    )(page_table, *([cache_kt] * PAGES_PER_STEP))


def _sample_select_kernel(q_ref, km_ref, idx_ref):
    tp = q_ref.shape[2]
    nb = km_ref.shape[3]
    n_idx = lax.broadcasted_iota(I32, (tp, nb), 1)
    for h in range(N_HEADS_A):
        g = _dot3(q_ref[0, h], km_ref[0, h])
        cols = []
        for _ in range(MOBA_TOPK):
            mx = jnp.max(g, axis=1, keepdims=True)
            first = jnp.min(jnp.where(g == mx, n_idx, nb), axis=1, keepdims=True)
            cols.append(first)
            g = jnp.where(n_idx == first, -jnp.inf, g)
        idx_ref[0, h] = jnp.concatenate(cols, axis=1)


def _sample_select(q_s, kmean_t):
    db, nh, tp, dh = q_s.shape
    nb = kmean_t.shape[3]
    return pl.pallas_call(
        _sample_select_kernel,
        grid=(db,),
        in_specs=[pl.BlockSpec((1, nh, tp, dh), lambda b: (b, 0, 0, 0)),
                  pl.BlockSpec((1, nh, dh, nb), lambda b: (b, 0, 0, 0))],
        out_specs=pl.BlockSpec((1, nh, tp, MOBA_TOPK), lambda b: (b, 0, 0, 0)),
        out_shape=jax.ShapeDtypeStruct((db, nh, tp, MOBA_TOPK), I32),
        compiler_params=_cparams(("parallel",)),
        name="sample_select",
    )(q_s, kmean_t)


def _sample_attend_kernel(t_valid, layer, phys_ref, q_ref, kn_ref, vn_ref, ck_ref, cv_ref, o_ref,
                          kbuf, vbuf, sem):
    b = pl.program_id(0)
    t = pl.program_id(1)
    ppb = MOBA_BLOCK // PAGE_SIZE
    n_slab = MOBA_TOPK * ppb
    tp = q_ref.shape[2]

    n = b * t_valid + t
    slot = n % 2

    def copies(step, sl, h, u):
        page = phys_ref[(step * N_HEADS_A + h) * n_slab + u]
        return (pltpu.make_async_copy(ck_ref.at[layer, page, h], kbuf.at[sl, h, u], sem.at[sl, 0]),
                pltpu.make_async_copy(cv_ref.at[layer, page, h], vbuf.at[sl, h, u], sem.at[sl, 1]))

    def issue(step, sl):
        for h in range(N_HEADS_A):
            for u in range(n_slab):
                ck, cv = copies(step, sl, h, u)
                ck.start()
                cv.start()

    @pl.when(n == 0)
    def _():
        issue(0, 0)

    @pl.when(n + 1 < pl.num_programs(0) * t_valid)
    def _():
        issue(n + 1, 1 - slot)

    pltpu.make_async_copy(kbuf.at[slot], kbuf.at[slot], sem.at[slot, 0]).wait()
    pltpu.make_async_copy(vbuf.at[slot], vbuf.at[slot], sem.at[slot, 1]).wait()

    row = lax.broadcasted_iota(I32, (tp, 1), 0)
    key = lax.broadcasted_iota(I32, (tp, tp), 1)
    for h in range(N_HEADS_A):
        qh = q_ref[0, h].astype(BF16)
        s_own = _dg(qh, kn_ref[0, h].astype(BF16), 1, 1)
        s_own = jnp.where((key <= t) & (key < t_valid), s_own, -jnp.inf)
        s_sel = [_dg(qh, kbuf[slot, h, u].astype(BF16), 1, 0) for u in range(n_slab)]
        m = jnp.max(s_own, axis=1, keepdims=True)
        for sj in s_sel:
            m = jnp.maximum(m, jnp.max(sj, axis=1, keepdims=True))
        p_own = jnp.exp(s_own - m)
        l = jnp.sum(p_own, axis=1, keepdims=True)
        acc = _dg(p_own.astype(BF16), vn_ref[0, h].astype(BF16), 1, 0)
        for u, sj in enumerate(s_sel):
            pj = jnp.exp(sj - m)
            l = l + jnp.sum(pj, axis=1, keepdims=True)
            acc = acc + _dg(pj.astype(BF16), vbuf[slot, h, u].astype(BF16), 1, 1)
        out = acc / l
        o_ref[0, 0, h] = jnp.sum(jnp.where(row == t, out, 0.0), axis=0, keepdims=True)


def _sample_attend(q_s, k_new, v_new, cache_kt, cache_vt, phys, t_valid, layer):
    db, nh, tp, dh = q_s.shape
    n_slab = MOBA_TOPK * (MOBA_BLOCK // PAGE_SIZE)
    hm = pl.BlockSpec((1, nh, tp, dh), lambda b, t, ph: (b, 0, 0, 0))
    return pl.pallas_call(
        functools.partial(_sample_attend_kernel, t_valid, layer),
        grid_spec=pltpu.PrefetchScalarGridSpec(
            num_scalar_prefetch=1,
            grid=(db, t_valid),
            in_specs=[hm, hm, hm, pl.BlockSpec(memory_space=pl.ANY), pl.BlockSpec(memory_space=pl.ANY)],
            out_specs=pl.BlockSpec((1, 1, nh, 1, dh), lambda b, t, ph: (b, t, 0, 0, 0)),
            scratch_shapes=[pltpu.VMEM((2, nh, n_slab, dh, PAGE_SIZE), F32),
                            pltpu.VMEM((2, nh, n_slab, dh, PAGE_SIZE), F32),
                            pltpu.SemaphoreType.DMA((2, 2))],
        ),
        out_shape=jax.ShapeDtypeStruct((db, t_valid, nh, 1, dh), F32),
        compiler_params=_cparams(("arbitrary", "arbitrary")),
        name="sample_attend",
    )(phys, q_s, k_new, v_new, cache_kt, cache_vt)


def _rwkv_feat_kernel(period, t_valid, tiles_per_seq, cur_ref, prev_ref, mu_ref, w0_ref, dup_ref, a0_ref,
                      aup_ref, gup_ref, kk_ref, ka_ref, rk_ref, seg_ref,
                      r_o, lw_o, k_o, v_o, kk_o, b_o, g_o, bonus_o):
    cur = cur_ref[...]
    if tiles_per_seq:
        edge = prev_ref[7:8, :]
        edge = jnp.where(pl.program_id(0) % tiles_per_seq == 0, 0.0, edge)
        first = lax.broadcasted_iota(I32, (cur.shape[0], 1), 0) == 0
        prev = jnp.where(first, edge, pltpu.roll(cur, 1, 0))
    else:
        prev = prev_ref[...]
    xs = cur + (prev - cur) * mu_ref[...]
    w = WIDTH_B
    r, k, v = xs[:, :w], xs[:, w:2 * w], xs[:, 2 * w:3 * w]
    dw = xs[:, 3 * w:3 * w + DECAY_LORA]
    da = xs[:, 3 * w + DECAY_LORA:3 * w + DECAY_LORA + AAA_LORA]
    dg = xs[:, 3 * w + DECAY_LORA + AAA_LORA:]
    lw = -DECAY_SCALE * jax.nn.sigmoid(w0_ref[...] + _dot3(jnp.tanh(dw), dup_ref[...]))
    a = jax.nn.sigmoid(a0_ref[...] + _dot3(da, aup_ref[...]))
    g = _dot3(jax.nn.sigmoid(dg), gup_ref[...])
    seg = seg_ref[...]
    kkr = k * kk_ref[...]
    kk = kkr / jnp.maximum(jnp.sqrt(_dot2_exact_rhs(kkr * kkr, seg)), 1e-12)
    k2 = k * (1.0 + (a - 1.0) * ka_ref[...])
    bonus = _dot2_exact_rhs(r * k2 * rk_ref[...], seg) * v
    if t_valid < period:
        rows = lax.broadcasted_iota(I32, (cur.shape[0], 1), 0)
        valid = (rows % period) < t_valid
        lw = jnp.where(valid, lw, 0.0)
        kk = jnp.where(valid, kk, 0.0)
        k2 = jnp.where(valid, k2, 0.0)
    r_o[...] = r
    lw_o[...] = lw
    k_o[...] = k2
    v_o[...] = v
    kk_o[...] = kk
    b_o[...] = kk * a
    g_o[...] = g
    bonus_o[...] = bonus


def _rwkv_features(p_cur, p_prev, prm, tm, period, t_valid, tiles_per_seq=0):
    n = p_cur.shape[0]
    row = lambda v: v.reshape(1, -1).astype(F32)
    const = lambda i: (0, 0)
    params = [row(prm['rw_mu']), row(prm['rw_w0']), prm['rw_decay_up'], row(prm['rw_a0']), prm['rw_a_up'],
              prm['rw_g_up'], row(prm['rw_k_k']), row(prm['rw_k_a']), row(prm['rw_r_k']),
              _seg_ones(WIDTH_B, HEAD_DIM_B)]
    tok = pl.BlockSpec((tm, RWKV_COLS), lambda i: (i, 0))
    if tiles_per_seq:
        prev_spec = pl.BlockSpec((8, RWKV_COLS), lambda i: (jnp.maximum(i * (tm // 8) - 1, 0), 0))
    else:
        prev_spec = tok
    out = pl.BlockSpec((tm, WIDTH_B), lambda i: (i, 0))
    return pl.pallas_call(
        functools.partial(_rwkv_feat_kernel, period, t_valid, tiles_per_seq),
        grid=(n // tm,),
        in_specs=[tok, prev_spec] + [pl.BlockSpec(p.shape, const) for p in params],
        out_specs=[out] * 8,
        out_shape=[jax.ShapeDtypeStruct((n, WIDTH_B), F32)] * 8,
        compiler_params=_cparams(("parallel",)),
        name="rwkv_features",
    )(p_cur, p_prev, *params)


def _rwkv_chunk_kernel(L, r_ref, lw_ref, k_ref, v_ref, kk_ref, b_ref, g_ref, bonus_ref, s0_ref,
                       lnw_ref, lnb_ref, seg_ref, o_ref, sT_ref, st_sc):
    c = pl.program_id(1)
    nsub = r_ref.shape[1] // L
    nh, dh = N_HEADS_B, HEAD_DIM_B

    @pl.when(c == 0)
    def _():
        st_sc[...] = s0_ref[0]

    ri = lax.broadcasted_iota(I32, (L, L), 0)
    ci = lax.broadcasted_iota(I32, (L, L), 1)
    strict = ri > ci
    tri = jnp.where(ri >= ci, 1.0, 0.0).astype(BF16)
    eye_l = jnp.where(ri == ci, 1.0, 0.0)
    rk = lax.broadcasted_iota(I32, (dh, dh), 0)
    ck = lax.broadcasted_iota(I32, (dh, dh), 1)
    ri2 = lax.broadcasted_iota(I32, (L, 2 * L), 0)
    ci2 = lax.broadcasted_iota(I32, (L, 2 * L), 1)
    incl2 = ri2 >= jnp.where(ci2 >= L, ci2 - L, ci2)

    alpha, beta, kappa, rho, vh, beta_t, kappa_t, g_last = [], [], [], [], [], [], [], []
    for sub in range(nsub):
        rows = slice(sub * L, (sub + 1) * L)
        lw = lw_ref[0, rows, :]
        cum = _dot2_exact_rhs_left(tri, lw)
        cum_last = cum[L - 1:L, :]
        e_neg = jnp.exp(-cum)
        e_tail = jnp.exp(cum_last - cum)
        gl_all = jnp.exp(cum_last)
        kk_all, b_all, k_all = kk_ref[0, rows, :], b_ref[0, rows, :], k_ref[0, rows, :]
        full = [kk_all * jnp.exp(cum - lw), b_all * e_neg, k_all * e_neg, r_ref[0, rows, :] * jnp.exp(cum),
                v_ref[0, rows, :], b_all * e_tail, k_all * e_tail]
        for dst, x in zip((alpha, beta, kappa, rho, vh, beta_t, kappa_t), full):
            dst.extend(x[:, h * dh:(h + 1) * dh] for h in range(nh))
        g_last.extend(gl_all[:, h * dh:(h + 1) * dh] for h in range(nh))
    its = range(nsub * nh)
    wcat = [jnp.concatenate([beta[i], kappa[i]], axis=0) for i in its]
    za = [_dot3(alpha[i], wcat[i], 1, 1) for i in its]
    zr = [_dotb(rho[i], wcat[i], 1, 1) for i in its]
    n_mat = [jnp.where(strict, za[i][:, :L], 0.0) for i in its]
    m_mat = [jnp.where(strict, za[i][:, L:], 0.0) for i in its]
    nrmr = [jnp.where(incl2, zr[i], 0.0) for i in its]
    mv = [_dot3(m_mat[i], vh[i]) for i in its]
    d = [eye_l - jnp.where(ri // 2 == ci // 2, n_mat[i], 0.0) for i in its]
    s = 2
    while s < L:
        lower_left = (ri // (2 * s) == ci // (2 * s)) & ((ri % (2 * s)) >= s) & ((ci % (2 * s)) < s)
        de = [_dot3(d[i], jnp.where(lower_left, n_mat[i], 0.0)) for i in its]
        d = [d[i] - _dot3(de[i], d[i]) for i in its]
        s *= 2
    ta = [_dot3(d[i], jnp.concatenate([alpha[i], mv[i]], axis=1)) for i in its]
    abar = [ta[i][:, :dh] for i in its]
    pv = [jnp.concatenate([-ta[i][:, dh:], vh[i]], axis=0) for i in its]
    rpp = [rho[i] - _dotb(nrmr[i][:, :L], abar[i]) for i in its]
    y0 = [_dotb(nrmr[i], pv[i]) for i in its]
    gt = [jnp.where(rk == ck, g_last[i], 0.0) - _dot3(beta_t[i], abar[i], 0, 0) for i in its]
    ht = [_dot3(jnp.concatenate([beta_t[i], kappa_t[i]], axis=0), pv[i], 0, 0) for i in its]
    ys = []
    for sub in range(nsub):
        base = sub * nh
        upd = [_dot3(jnp.concatenate([rpp[base + h], gt[base + h]], axis=0), st_sc[h])
               for h in range(nh)]
        for h in range(nh):
            st_sc[h] = upd[h][L:] + ht[base + h]
        ys.append(jnp.concatenate([y0[base + h] + upd[h][:L] for h in range(nh)], axis=1))
    y = ys[0] if nsub == 1 else jnp.concatenate(ys, axis=0)
    seg = seg_ref[...]
    mu = _dot2_exact_rhs(y, seg) * (1.0 / dh)
    yc = y - mu
    var = _dot2_exact_rhs(yc * yc, seg) * (1.0 / dh)
    yn = yc * lax.rsqrt(var + LN_X_EPS) * lnw_ref[...] + lnb_ref[...]
    o_ref[0] = ((yn + bonus_ref[0]) * g_ref[0]).astype(o_ref.dtype)
    sT_ref[0] = st_sc[...]


def _dot2_exact_rhs_left(m_bf16, x):
    xh, xl = _split(x)
    return _dg(m_bf16, xh, 1, 0) + _dg(m_bf16, xl, 1, 0)


def _rwkv_chunked(feats, s0_t, ln_w, ln_b, chunk, chunks_per_step):
    r, lw, k2, v, kk, b, g, bonus = feats
    bq, s, w = r.shape
    rows = chunk * chunks_per_step
    tok = pl.BlockSpec((1, rows, w), lambda bi, c: (bi, c, 0))
    st = pl.BlockSpec((1, N_HEADS_B, HEAD_DIM_B, HEAD_DIM_B), lambda bi, c: (bi, 0, 0, 0))
    const = lambda bi, c: (0, 0)
    return pl.pallas_call(
        functools.partial(_rwkv_chunk_kernel, chunk),
        grid=(bq, s // rows),
        in_specs=[tok] * 8 + [st, pl.BlockSpec((1, w), const), pl.BlockSpec((1, w), const),
                              pl.BlockSpec((w, w), const)],
        out_specs=[tok, st],
        out_shape=[jax.ShapeDtypeStruct((bq, s, w), BF16),
                   jax.ShapeDtypeStruct((bq, N_HEADS_B, HEAD_DIM_B, HEAD_DIM_B), F32)],
        scratch_shapes=[pltpu.VMEM((N_HEADS_B, HEAD_DIM_B, HEAD_DIM_B), F32)],
        compiler_params=_cparams(("parallel", "arbitrary")),
        name="rwkv_chunked",
    )(r, lw, k2, v, kk, b, g, bonus, s0_t, ln_w.reshape(1, w), ln_b.reshape(1, w),
      _seg_ones(w, HEAD_DIM_B))


def _mem_kv_kernel(mem_ref, g_ref, w_ref, kg_ref, mk_ref, mv_ref):
    h = _rms(mem_ref[0], g_ref[...]).astype(BF16)
    kv = _dg(h, w_ref[...], 1, 0)
    for hm in range(N_HEADS_M):
        sl = slice(hm * HEAD_DIM_M, (hm + 1) * HEAD_DIM_M)
        mk_ref[0, :, sl] = _rms(kv[:, sl], kg_ref[...])
    mv_ref[0] = kv[:, WIDTH_M:]


def _memory_kv(mem, norm_mem, w_mem_kv, k_norm_m):
    b, m, d = mem.shape
    const = lambda i: (0, 0)
    out = pl.BlockSpec((1, m, WIDTH_M), lambda i: (i, 0, 0))
    return pl.pallas_call(
        _mem_kv_kernel,
        grid=(b,),
        in_specs=[pl.BlockSpec((1, m, d), lambda i: (i, 0, 0)), pl.BlockSpec((1, d), const),
                  pl.BlockSpec((d, 2 * WIDTH_M), const), pl.BlockSpec((1, HEAD_DIM_M), const)],
        out_specs=[out, out],
        out_shape=[jax.ShapeDtypeStruct((b, m, WIDTH_M), F32)] * 2,
        compiler_params=_cparams(("parallel",)),
        name="memory_kv",
    )(mem, norm_mem.reshape(1, d), w_mem_kv.astype(BF16), k_norm_m.reshape(1, HEAD_DIM_M))


def _mem_attend_kernel(q_ref, mk_ref, mv_ref, g_ref, o_ref):
    q = q_ref[0]
    for hm in range(N_HEADS_M):
        sl = slice(hm * HEAD_DIM_M, (hm + 1) * HEAD_DIM_M)
        qh = (_rms(q[:, sl], g_ref[...]) * (HEAD_DIM_M ** -0.5)).astype(BF16)
        s = _dg(qh, mk_ref[0, :, sl].astype(BF16), 1, 1)
        p = jnp.exp(s - jnp.max(s, axis=1, keepdims=True))
        o = _dg(p.astype(BF16), mv_ref[0, :, sl].astype(BF16), 1, 0) / jnp.sum(p, axis=1, keepdims=True)
        o_ref[0, :, sl] = o.astype(o_ref.dtype)


def _memory_attend(q_m, mk, mv, q_norm_m, tq):
    b, s, w = q_m.shape
    m = mk.shape[1]
    kv = pl.BlockSpec((1, m, w), lambda bi, j: (bi, 0, 0))
    return pl.pallas_call(
        _mem_attend_kernel,
        grid=(b, s // tq),
        in_specs=[pl.BlockSpec((1, tq, w), lambda bi, j: (bi, j, 0)), kv, kv,
                  pl.BlockSpec((1, HEAD_DIM_M), lambda bi, j: (0, 0))],
        out_specs=pl.BlockSpec((1, tq, w), lambda bi, j: (bi, j, 0)),
        out_shape=jax.ShapeDtypeStruct((b, s, w), BF16),
        compiler_params=_cparams(("parallel", "parallel")),
        name="memory_attend",
    )(q_m, mk, mv, q_norm_m.reshape(1, HEAD_DIM_M))


def _merge_kernel(x_ref, oa_ref, ob_ref, om_ref, gl_ref, wb_ref, wo_ref, ng_ref, rw_ref, rb_ref,
                  x1_ref, h_ref, e_ref, gate_ref):
    d = x_ref.shape[1]
    gl = gl_ref[...]
    merged = jnp.zeros(x_ref.shape, F32)
    for n, o_ref in enumerate((oa_ref, ob_ref, om_ref)):
        y = _dg(o_ref[...], wb_ref[n], 1, 0)
        merged = merged + jax.nn.sigmoid(gl[:, n * d:(n + 1) * d].astype(F32)) * y
    x1 = x_ref[...] + _dg(merged.astype(BF16), wo_ref[...], 1, 0)
    x1_ref[...] = x1
    hn = _rms(x1, ng_ref[...])
    h_ref[...] = hn
    logits = _dot3(hn, rw_ref[...]) + rb_ref[...]
    tm, ne = logits.shape
    e_idx = lax.broadcasted_iota(I32, (tm, ne), 1)
    vals, idxs = [], []
    g = logits
    for _ in range(TOP_K):
        mx = jnp.max(g, axis=1, keepdims=True)
        first = jnp.min(jnp.where(g == mx, e_idx, ne), axis=1, keepdims=True)
        vals.append(mx)
        idxs.append(first)
        g = jnp.where(e_idx == first, -jnp.inf, g)
    top = jnp.concatenate(vals, axis=1)
    pe = jnp.exp(top - vals[0])
    gate_ref[...] = pe / jnp.sum(pe, axis=1, keepdims=True)
    e_ref[...] = jnp.concatenate(idxs, axis=1)


def _merge_and_route(x2, o_a, o_b, o_m, gl, prm, tm):
    n, d = x2.shape
    const2 = lambda i: (0, 0)
    tok = lambda wd: pl.BlockSpec((tm, wd), lambda i: (i, 0))
    return pl.pallas_call(
        _merge_kernel,
        grid=(n // tm,),
        in_specs=[tok(d), tok(512), tok(512), tok(512), tok(3 * d),
                  pl.BlockSpec((3, 512, d), lambda i: (0, 0, 0)), pl.BlockSpec((d, d), const2),
                  pl.BlockSpec((1, d), const2), pl.BlockSpec((d, N_EXPERTS), const2),
                  pl.BlockSpec((1, N_EXPERTS), const2)],
        out_specs=[tok(d), tok(d), tok(TOP_K), tok(TOP_K)],
        out_shape=[jax.ShapeDtypeStruct((n, d), F32), jax.ShapeDtypeStruct((n, d), F32),
                   jax.ShapeDtypeStruct((n, TOP_K), I32), jax.ShapeDtypeStruct((n, TOP_K), F32)],
        compiler_params=_cparams(("parallel",)),
        name="merge_route",
    )(x2, o_a, o_b, o_m, gl, prm['w_branch'].astype(BF16), prm['w_out'].astype(BF16),
      prm['norm_ffn'].reshape(1, d), prm['router_w'], prm['router_b'].reshape(1, N_EXPERTS))


def _onehots(e):
    tm = e.shape[0]
    e_idx = lax.broadcasted_iota(I32, (tm, N_EXPERTS), 1)
    return [jnp.where(e[:, k:k + 1] == e_idx, 1.0, 0.0) for k in range(TOP_K)]


def _moe_rank_kernel(e_ref, rank_ref, cnt_ref, base_sc):
    i = pl.program_id(0)

    @pl.when(i == 0)
    def _():
        base_sc[...] = jnp.zeros_like(base_sc)

    ohs = _onehots(e_ref[...])
    cnt = ohs[0] + ohs[1] + ohs[2] + ohs[3]
    tm = cnt.shape[0]
    ri = lax.broadcasted_iota(I32, (tm, tm), 0)
    ci = lax.broadcasted_iota(I32, (tm, tm), 1)
    tri = jnp.where(ri > ci, 1.0, 0.0).astype(BF16)
    tot = _dg(tri, cnt.astype(BF16), 1, 0) + base_sc[...]
    rank_ref[...] = jnp.concatenate([jnp.sum(oh * tot, axis=1, keepdims=True) for oh in ohs],
                                    axis=1).astype(I32)
    base_sc[...] = base_sc[...] + jnp.sum(cnt, axis=0, keepdims=True)
    cnt_ref[...] = base_sc[...].astype(I32)


def _moe_rank(top_e, tm):
    n = top_e.shape[0]
    return pl.pallas_call(
        _moe_rank_kernel,
        grid=(n // tm,),
        in_specs=[pl.BlockSpec((tm, TOP_K), lambda i: (i, 0))],
        out_specs=[pl.BlockSpec((tm, TOP_K), lambda i: (i, 0)), pl.BlockSpec((1, N_EXPERTS), lambda i: (0, 0))],
        out_shape=[jax.ShapeDtypeStruct((n, TOP_K), I32), jax.ShapeDtypeStruct((1, N_EXPERTS), I32)],
        scratch_shapes=[pltpu.VMEM((1, N_EXPERTS), F32)],
        compiler_params=_cparams(("arbitrary",)),
        name="moe_rank",
    )(top_e)


def _moe_rows_kernel(e_ref, rank_ref, start_ref, row_ref):
    ohs = _onehots(e_ref[...])
    st = start_ref[...].astype(F32)
    base = jnp.concatenate([jnp.sum(oh * st, axis=1, keepdims=True) for oh in ohs], axis=1)
    row_ref[...] = rank_ref[...] + base.astype(I32)


def _moe_rows(top_e, rank, starts, tm):
    n = top_e.shape[0]
    tok = pl.BlockSpec((tm, TOP_K), lambda i: (i, 0))
    return pl.pallas_call(
        _moe_rows_kernel,
        grid=(n // tm,),
        in_specs=[tok, tok, pl.BlockSpec((1, N_EXPERTS), lambda i: (0, 0))],
        out_specs=tok,
        out_shape=jax.ShapeDtypeStruct((n, TOP_K), I32),
        compiler_params=_cparams(("parallel",)),
        name="moe_rows",
    )(top_e, rank, starts)


def _dispatch_kernel(row_ref, h_ref, xs_in_ref, xs_ref, sem):
    del xs_in_ref
    tm = h_ref.shape[0]

    def start(t, carry):
        for k in range(TOP_K):
            r = row_ref[t * TOP_K + k]
            pltpu.make_async_copy(h_ref.at[pl.ds(t, 1)], xs_ref.at[pl.ds(r, 1)], sem).start()
        return carry

    lax.fori_loop(0, tm, start, 0, unroll=2)
    all_rows = xs_ref.at[pl.ds(0, tm * TOP_K)]
    pltpu.make_async_copy(all_rows, all_rows, sem).wait()


def _zero_rows_kernel(o_ref):
    o_ref[...] = jnp.zeros_like(o_ref)


def _zero_rows(n_rows, d):
    tile = next(t for t in (1024, 512, MOE_ROWS) if n_rows % t == 0)
    return pl.pallas_call(
        _zero_rows_kernel,
        grid=(n_rows // tile,),
        out_specs=pl.BlockSpec((tile, d), lambda i: (i, 0)),
        out_shape=jax.ShapeDtypeStruct((n_rows, d), F32),
        compiler_params=_cparams(("parallel",)),
        name="zero_rows",
    )()


def _moe_dispatch(h, row_flat, n_rows, tm):
    n, d = h.shape
    xs0 = _zero_rows(n_rows, d)
    return pl.pallas_call(
        _dispatch_kernel,
        grid=(n // tm,),
        in_specs=[pl.BlockSpec((tm * TOP_K,), lambda i: (i,), memory_space=pltpu.SMEM),
                  pl.BlockSpec((tm, d), lambda i: (i, 0)),
                  pl.BlockSpec(memory_space=pl.ANY)],
        out_specs=pl.BlockSpec(memory_space=pl.ANY),
        out_shape=jax.ShapeDtypeStruct((n_rows, d), F32),
        scratch_shapes=[pltpu.SemaphoreType.DMA(())],
        input_output_aliases={2: 0},
        compiler_params=_cparams(("arbitrary",)),
        name="moe_dispatch",
    )(row_flat, h, xs0)


def _swiglu(u, d_ff):
    u_glu = jnp.minimum(u[:, :d_ff], SWIGLU_LIMIT)
    u_lin = jnp.clip(u[:, d_ff:], -SWIGLU_LIMIT, SWIGLU_LIMIT)
    return u_glu * jax.nn.sigmoid(SWIGLU_ALPHA * u_glu) * (u_lin + 1.0)


def _moe_ffn_kernel(be_ref, nu_ref, xs_ref, w1_ref, b1_ref, w2_ref, b2_ref, y_ref, w1_sc, w2_sc):
    i = pl.program_id(0)
    prev = be_ref[jnp.maximum(i - 1, 0)]
    first = (i == 0) | (be_ref[i] != prev)

    @pl.when(first)
    def _():
        w1_sc[...] = w1_ref[...].astype(BF16)
        w2_sc[...] = w2_ref[...].astype(BF16)

    @pl.when(i < nu_ref[0])
    def _():
        u = _dg(xs_ref[...].astype(BF16), w1_sc[...], 1, 0) + b1_ref[...]
        act = _swiglu(u, w2_ref.shape[0])
        y_ref[...] = _dg(act.astype(BF16), w2_sc[...], 1, 0) + b2_ref[...]

    @pl.when(i >= nu_ref[0])
    def _():
        y_ref[...] = jnp.zeros_like(y_ref)


def _moe_ffn(xs, blk_e, n_used, w1, b1, w2, b2):
    n_rows, d = xs.shape
    ne, _, f2 = w1.shape
    d_ff = w2.shape[1]
    nblk = n_rows // MOE_ROWS
    return pl.pallas_call(
        _moe_ffn_kernel,
        grid_spec=pltpu.PrefetchScalarGridSpec(
            num_scalar_prefetch=2,
            grid=(nblk,),
            in_specs=[pl.BlockSpec((MOE_ROWS, d), lambda i, be, nu: (i, 0)),
                      pl.BlockSpec((None, d, f2), lambda i, be, nu: (be[i], 0, 0)),
                      pl.BlockSpec((None, 1, f2), lambda i, be, nu: (be[i], 0, 0)),
                      pl.BlockSpec((None, d_ff, d), lambda i, be, nu: (be[i], 0, 0)),
                      pl.BlockSpec((None, 1, d), lambda i, be, nu: (be[i], 0, 0))],
            out_specs=pl.BlockSpec((MOE_ROWS, d), lambda i, be, nu: (i, 0)),
            scratch_shapes=[pltpu.VMEM((d, f2), BF16), pltpu.VMEM((d_ff, d), BF16)],
        ),
        out_shape=jax.ShapeDtypeStruct((n_rows, d), F32),
        compiler_params=_cparams(("arbitrary",)),
        name="moe_ffn",
    )(blk_e, n_used, xs, w1, b1.reshape(ne, 1, f2), w2, b2.reshape(ne, 1, d))


def _combine_kernel(row_ref, yb_ref, x1_ref, gate_ref, y_ref, buf, sem):
    tm = x1_ref.shape[0]

    def start(t, carry):
        for k in range(TOP_K):
            r = row_ref[t * TOP_K + k]
            pltpu.make_async_copy(yb_ref.at[pl.ds(r, 1)], buf.at[pl.ds(k * tm + t, 1)], sem).start()
        return carry

    lax.fori_loop(0, tm, start, 0, unroll=2)
    pltpu.make_async_copy(yb_ref.at[pl.ds(0, tm * TOP_K)], buf, sem).wait()
    gates = gate_ref[...]
    acc = buf[0:tm] * gates[:, 0:1]
    for k in range(1, TOP_K):
        acc = acc + buf[k * tm:(k + 1) * tm] * gates[:, k:k + 1]
    y_ref[...] = x1_ref[...] + acc


def _moe_combine(yb, row_flat, x1, gates, tm):
    n, d = x1.shape
    return pl.pallas_call(
        _combine_kernel,
        grid=(n // tm,),
        in_specs=[pl.BlockSpec((tm * TOP_K,), lambda i: (i,), memory_space=pltpu.SMEM),
                  pl.BlockSpec(memory_space=pl.ANY),
                  pl.BlockSpec((tm, d), lambda i: (i, 0)),
                  pl.BlockSpec((tm, TOP_K), lambda i: (i, 0))],
        out_specs=pl.BlockSpec((tm, d), lambda i: (i, 0)),
        out_shape=jax.ShapeDtypeStruct((n, d), F32),
        scratch_shapes=[pltpu.VMEM((TOP_K * tm, d), F32), pltpu.SemaphoreType.DMA(())],
        compiler_params=_cparams(("arbitrary",)),
        name="moe_combine",
    )(row_flat, yb, x1, gates)


def _moe_block(x1, hn, top_e, gates, prm, tm):
    n, d = x1.shape
    rank, counts = _moe_rank(top_e, tm)
    counts = counts[0]
    padded = (counts + MOE_ROWS - 1) // MOE_ROWS * MOE_ROWS
    p_end = jnp.cumsum(padded)
    starts = (p_end - padded).astype(I32)
    nblk = -(-(n * TOP_K + N_EXPERTS * (MOE_ROWS - 1)) // MOE_ROWS)
    n_used = (p_end[-1] // MOE_ROWS).astype(I32)
    blk_i = jnp.minimum(jnp.arange(nblk, dtype=I32), n_used - 1)
    n_before = jnp.sum((p_end[None, :] <= (blk_i * MOE_ROWS)[:, None]).astype(I32), axis=1)
    blk_e = jnp.minimum(n_before, N_EXPERTS - 1).astype(I32)
    row = _moe_rows(top_e, rank, starts[None, :], tm)
    row_flat = row.reshape(-1)
    xs = _moe_dispatch(hn, row_flat, nblk * MOE_ROWS, tm)
    yb = _moe_ffn(xs, blk_e, n_used[None], prm['moe_w1'], prm['moe_b1'], prm['moe_w2'], prm['moe_b2'])
    return _moe_combine(yb, row_flat, x1, gates, tm)


def _split_w_in(w_in):
    d = w_in.shape[0]
    a = 3 * WIDTH_A
    b = a + RWKV_COLS
    c = b + WIDTH_M
    wb = w_in.astype(BF16)
    return [wb[:, :a], wb[:, a:b], wb[:, b:c], wb[:, c:]]


def _token_mix_tail(x2, o_a, o_b, o_m, gl, prm, tm):
    x1, hn, top_e, gates = _merge_and_route(x2, o_a, o_b, o_m, gl, prm, tm)
    return _moe_block(x1, hn, top_e, gates, prm, tm)


def _layer_prompt(x, mem, prm, ws):
    b, s, d = x.shape
    tm = 256
    x2 = x.reshape(b * s, d)
    qkv, p_rw, q_m, gl = _in_projection(x2, prm['norm_mix'].reshape(1, d), ws, tm)
    pos = jnp.arange(s, dtype=I32)
    q_s, k_t, v_t, kaug, vaug, kmean = _moba_prep(qkv.reshape(b, s, -1), pos, prm['q_norm_a'], prm['k_norm_a'],
                                                  MOBA_BLOCK, True)
    k_o, v_o = jnp.swapaxes(k_t, 2, 3), jnp.swapaxes(v_t, 2, 3)
    o_a = _moba_flash(q_s, kaug, vaug, kmean).reshape(b * s, WIDTH_A)

    p3 = p_rw.reshape(b, s, RWKV_COLS)
    feats = _rwkv_features(p_rw, p_rw, prm, tm, 1, 1, tiles_per_seq=s // tm)
    feats = [f.reshape(b, s, WIDTH_B) for f in feats]
    s0_t = jnp.zeros((b, N_HEADS_B, HEAD_DIM_B, HEAD_DIM_B), F32)
    o_b, st = _rwkv_chunked(feats, s0_t, prm['ln_x_w'], prm['ln_x_b'], RWKV_CHUNK, RWKV_CHUNKS_PER_STEP)
    wkv = jnp.swapaxes(st, 2, 3)

    mk, mv = _memory_kv(mem, prm['norm_mem'], prm['w_mem_kv'], prm['k_norm_m'])
    o_m = _memory_attend(q_m.reshape(b, s, WIDTH_M), mk, mv, prm['q_norm_m'], 512)

    y = _token_mix_tail(x2, o_a, o_b.reshape(b * s, WIDTH_B), o_m.reshape(b * s, WIDTH_M), gl, prm, tm)
    m = mem.shape[1]
    return (y.reshape(b, s, d), k_o, v_o, wkv, p3[:, -1],
            mk.reshape(b, m, N_HEADS_M, HEAD_DIM_M), mv.reshape(b, m, N_HEADS_M, HEAD_DIM_M))


def _layer_sample(x, cache_k, cache_v, mem_k, mem_v, wkv0, shift0, page_table, layer, prm, ws):
    db, t, d = x.shape
    tp = SAMPLE_T_PAD
    past_len = page_table.shape[1] * PAGE_SIZE
    assert past_len % MOBA_BLOCK == 0 and t <= tp
    ppb = MOBA_BLOCK // PAGE_SIZE
    n = db * tp
    x2 = jnp.pad(x, ((0, 0), (0, tp - t), (0, 0))).reshape(n, d)
    qkv, p_rw, q_m, gl = _in_projection(x2, prm['norm_mix'].reshape(1, d), ws, n)
    pos = past_len + jnp.arange(tp, dtype=I32)
    q_s, k_o, v_o = _moba_prep(qkv.reshape(db, tp, -1), pos, prm['q_norm_a'], prm['k_norm_a'], tp, False)

    cache_kt, cache_vt = jnp.swapaxes(cache_k, 3, 4), jnp.swapaxes(cache_v, 3, 4)
    kmean_t = _page_means(cache_kt, page_table, layer)
    idx = _sample_select(q_s, kmean_t)
    idx = jnp.transpose(idx[:, :, :t], (0, 2, 1, 3))
    logical = idx[..., None] * ppb + jnp.arange(ppb, dtype=I32)
    phys = page_table[jnp.arange(db)[:, None, None, None, None], logical].reshape(-1).astype(I32)
    o_a = _sample_attend(q_s, k_o, v_o, cache_kt, cache_vt, phys, t, layer)
    o_a = jnp.pad(o_a.reshape(db, t, WIDTH_A), ((0, 0), (0, tp - t), (0, 0))).reshape(n, WIDTH_A).astype(BF16)

    p3 = p_rw.reshape(db, tp, RWKV_COLS)
    p_prev = jnp.concatenate([shift0[:, None, :], p3[:, :-1]], axis=1).reshape(n, RWKV_COLS)
    feats = _rwkv_features(p_rw, p_prev, prm, n, tp, t)
    feats = [f.reshape(db, tp, WIDTH_B) for f in feats]
    o_b, st = _rwkv_chunked(feats, jnp.swapaxes(wkv0, 2, 3), prm['ln_x_w'], prm['ln_x_b'], tp, 1)
    wkv = jnp.swapaxes(st, 2, 3)

    m = mem_k.shape[1]
    o_m = _memory_attend(q_m.reshape(db, tp, WIDTH_M), mem_k.reshape(db, m, WIDTH_M),
                         mem_v.reshape(db, m, WIDTH_M), prm['q_norm_m'], tp)

    y = _token_mix_tail(x2, o_a, o_b.reshape(n, WIDTH_B), o_m.reshape(n, WIDTH_M), gl, prm, n)
    return (y.reshape(db, tp, d)[:, :t], k_o[:, :, :t], v_o[:, :, :t], wkv, p3[:, t - 1])


def kernel(x_prompt, x_sample, mem_prompt, cache_k, cache_v, cache_mem_k, cache_mem_v, state_wkv, state_shift,
           page_table, norm_mix, norm_mem, norm_ffn, w_in, q_norm_a, k_norm_a, q_norm_m, k_norm_m, w_mem_kv,
           rw_mu, rw_w0, rw_decay_up, rw_a0, rw_a_up, rw_g_up, rw_k_k, rw_k_a, rw_r_k, ln_x_w, ln_x_b,
           w_branch, w_out, router_w, router_b, moe_w1, moe_b1, moe_w2, moe_b2):
    depth = w_in.shape[0]
    xp, xs = x_prompt, x_sample
    outs_p = [[] for _ in range(6)]
    outs_s = [[] for _ in range(4)]
    for l in range(depth):
        prm = dict(norm_mix=norm_mix[l], norm_mem=norm_mem[l], norm_ffn=norm_ffn[l], w_in=w_in[l],
                   q_norm_a=q_norm_a[l], k_norm_a=k_norm_a[l], q_norm_m=q_norm_m[l], k_norm_m=k_norm_m[l],
                   w_mem_kv=w_mem_kv[l], rw_mu=rw_mu[l], rw_w0=rw_w0[l], rw_decay_up=rw_decay_up[l],
                   rw_a0=rw_a0[l], rw_a_up=rw_a_up[l], rw_g_up=rw_g_up[l], rw_k_k=rw_k_k[l], rw_k_a=rw_k_a[l],
                   rw_r_k=rw_r_k[l].reshape(-1), ln_x_w=ln_x_w[l], ln_x_b=ln_x_b[l], w_branch=w_branch[l],
                   w_out=w_out[l], router_w=router_w[l], router_b=router_b[l], moe_w1=moe_w1[l],
                   moe_b1=moe_b1[l], moe_w2=moe_w2[l], moe_b2=moe_b2[l])
        ws = _split_w_in(prm['w_in'])
        xp, *rest_p = _layer_prompt(xp, mem_prompt, prm, ws)
        for acc, val in zip(outs_p, rest_p):
            acc.append(val)
        xs, *rest_s = _layer_sample(xs, cache_k, cache_v, cache_mem_k[l], cache_mem_v[l], state_wkv[l],
                                    state_shift[l], page_table, l, prm, ws)
        for acc, val in zip(outs_s, rest_s):
            acc.append(val)
    kp, vp, wkvp, shp, mkp, mvp = [jnp.stack(a) for a in outs_p]
    ksm, vsm, wkvs, shs = [jnp.stack(a) for a in outs_s]
    return (xp, xs, kp, vp, wkvp, shp, mkp, mvp, ksm, vsm, wkvs, shs)
```

```python
import functools
import math

import jax
import jax.numpy as jnp
from jax import lax
from jax.experimental import pallas as pl
from jax.experimental.pallas import tpu as pltpu

F32 = jnp.float32
BF16 = jnp.bfloat16
I32 = jnp.int32

N_HEADS_A = 8
HEAD_DIM_A = 64
WIDTH_A = 512
MOBA_BLOCK = 256
MOBA_TOPK = 3
ROT_DIM = 16
ROPE_THETA = 500000.0
PAGE_SIZE = 128
N_HEADS_B = 8
HEAD_DIM_B = 64
WIDTH_B = 512
DECAY_LORA = 64
AAA_LORA = 64
GATE_LORA = 128
RWKV_COLS = 1792
DECAY_SCALE = math.exp(-0.5)
LN_X_EPS = 64e-5
N_HEADS_M = 4
HEAD_DIM_M = 128
WIDTH_M = 512
N_EXPERTS = 32
TOP_K = 4
SWIGLU_ALPHA = 1.702
SWIGLU_LIMIT = 7.0
NORM_EPS = 1e-6

NEG_BIG = -1e30
SAMPLE_T_PAD = 8
RWKV_CHUNK = 64
RWKV_CHUNKS_PER_STEP = 2
MOE_ROWS = 512
VMEM_LIMIT = 56 * 1024 * 1024


def _cparams(sem, vmem=None):
    return pltpu.CompilerParams(dimension_semantics=sem, vmem_limit_bytes=vmem or VMEM_LIMIT)


def _dg(a, b, ca, cb):
    return lax.dot_general(a, b, (((ca,), (cb,)), ((), ())), preferred_element_type=F32)


def _split(x):
    hi = x.astype(BF16)
    lo = (x - hi.astype(F32)).astype(BF16)
    return hi, lo


def _dot3(a, b, ca=1, cb=0):
    ah, al = _split(a)
    bh, bl = _split(b)
    return _dg(ah, bh, ca, cb) + (_dg(ah, bl, ca, cb) + _dg(al, bh, ca, cb))


def _dot2_exact_rhs(a, b_bf16):
    ah, al = _split(a)
    return _dg(ah, b_bf16, 1, 0) + _dg(al, b_bf16, 1, 0)


def _dotb(a, b, ca=1, cb=0):
    return _dg(a.astype(BF16), b.astype(BF16), ca, cb)


def _rms(x, gain_row):
    ms = jnp.mean(x * x, axis=-1, keepdims=True)
    return x * lax.rsqrt(ms + NORM_EPS) * gain_row


def _seg_ones(width, seg):
    r = lax.broadcasted_iota(I32, (width, width), 0) // seg
    c = lax.broadcasted_iota(I32, (width, width), 1) // seg
    return jnp.where(r == c, 1.0, 0.0).astype(BF16)


def _proj_kernel(x_ref, g_ref, w1, w2, w3, w4, o1, o2, o3, o4):
    h = _rms(x_ref[...], g_ref[...]).astype(BF16)
    o1[...] = _dg(h, w1[...], 1, 0)
    o2[...] = _dg(h, w2[...], 1, 0)
    o3[...] = _dg(h, w3[...], 1, 0)
    o4[...] = _dg(h, w4[...], 1, 0).astype(o4.dtype)


def _in_projection(x2, gain, ws, tm):
    n, d = x2.shape
    widths = [w.shape[1] for w in ws]
    dtypes = [F32, F32, F32, BF16]
    const = lambda i: (0, 0)
    return pl.pallas_call(
        _proj_kernel,
        grid=(n // tm,),
        in_specs=[pl.BlockSpec((tm, d), lambda i: (i, 0)), pl.BlockSpec((1, d), const)]
        + [pl.BlockSpec((d, wd), const) for wd in widths],
        out_specs=[pl.BlockSpec((tm, wd), lambda i: (i, 0)) for wd in widths],
        out_shape=[jax.ShapeDtypeStruct((n, wd), dt) for wd, dt in zip(widths, dtypes)],
        compiler_params=_cparams(("parallel",)),
        name="in_projection",
    )(x2, gain, *ws)


def _rope_tables(pos):
    half = ROT_DIM // 2
    inv_freq = 1.0 / (ROPE_THETA ** (jnp.arange(0, ROT_DIM, 2, dtype=F32) / ROT_DIM))
    ang = pos.astype(F32)[:, None] * inv_freq[None, :]
    cos, sin = jnp.cos(ang), jnp.sin(ang)
    n = pos.shape[0]
    rest = HEAD_DIM_A - ROT_DIM
    c = jnp.concatenate([cos, cos, jnp.ones((n, rest), F32)], axis=1)
    s_up = jnp.concatenate([-sin, jnp.zeros((n, half + rest), F32)], axis=1)
    s_dn = jnp.concatenate([jnp.zeros((n, half), F32), sin, jnp.zeros((n, rest), F32)], axis=1)
    two = lambda t: jnp.concatenate([t, t], axis=1)
    return two(c), two(s_up), two(s_dn)


def _norm_rope(x, seg, gain, c, s_up, s_dn):
    ss = _dot2_exact_rhs(x * x, seg)
    y = x * lax.rsqrt(ss * (1.0 / HEAD_DIM_A) + NORM_EPS) * gain
    half = ROT_DIM // 2
    up = pltpu.roll(y, WIDTH_A - half, 1)
    dn = pltpu.roll(y, half, 1)
    return y * c + up * s_up + dn * s_dn


def _moba_prep_kernel(qkv_ref, seg_ref, qg_ref, kg_ref, c_ref, su_ref, sd_ref, *outs):
    _prep_body(False, qkv_ref[0], None, seg_ref, qg_ref, kg_ref, c_ref, su_ref, sd_ref, outs)


def _proj_prep_kernel(nb, x_ref, g_ref, w1, w2, w3, w4, seg_ref, qg_ref, kg_ref, c_ref, su_ref, sd_ref,
                      o_rw, o_qm, o_gl, *outs):
    h = _rms(x_ref[...], g_ref[...]).astype(BF16)
    _prep_body(True, _dg(h, w1[...], 1, 0), pl.program_id(0) % nb, seg_ref, qg_ref, kg_ref,
               c_ref, su_ref, sd_ref, outs)
    o_rw[...] = _dg(h, w2[...], 1, 0)
    o_qm[...] = _dg(h, w3[...], 1, 0)
    o_gl[...] = _dg(h, w4[...], 1, 0).astype(o_gl.dtype)


def _prep_body(with_blocks, x, blk, seg_ref, qg_ref, kg_ref, c_ref, su_ref, sd_ref, outs):
    if with_blocks:
        qs_ref, kt_ref, vt_ref, kaug_ref, vaug_ref, kmean_ref = outs
    else:
        qs_ref, k_ref, v_ref = outs
    tm = x.shape[0]
    rep = lambda r: jnp.concatenate([r[...]] * (WIDTH_A // 128), axis=1)
    c, su, sd = rep(c_ref), rep(su_ref), rep(sd_ref)
    seg = seg_ref[...]
    q = _norm_rope(x[:, :WIDTH_A], seg, qg_ref[...], c, su, sd) * (HEAD_DIM_A ** -0.5)
    k = _norm_rope(x[:, WIDTH_A:2 * WIDTH_A], seg, kg_ref[...], c, su, sd)
    v = x[:, 2 * WIDTH_A:]
    if with_blocks:
        lane = lax.broadcasted_iota(I32, (tm, HEAD_DIM_A), 1)
        onehot = jnp.where(lane == blk, 1.0, 0.0).astype(BF16)
        ones_col = jnp.where(lane == 0, 1.0, 0.0).astype(BF16)

        @pl.when(blk == 0)
        def _():
            kmean_ref[...] = jnp.zeros_like(kmean_ref)

    for h in range(N_HEADS_A):
        sl = slice(h * HEAD_DIM_A, (h + 1) * HEAD_DIM_A)
        qs_ref[0, h] = q[:, sl]
        if with_blocks:
            kt_ref[0, h] = k[:, sl].T
            vt_ref[0, h] = v[:, sl].T
            kaug_ref[0, h] = jnp.concatenate([k[:, sl].astype(BF16), onehot], axis=1)
            vaug_ref[0, h] = jnp.concatenate([v[:, sl].astype(BF16), ones_col], axis=1)
            kmean_ref[0, h, pl.ds(blk, 1), :] = jnp.mean(k[:, sl], axis=0, keepdims=True)
        else:
            k_ref[0, h] = k[:, sl]
            v_ref[0, h] = v[:, sl]


def _moba_prep(qkv, pos, q_gain, k_gain, tm):
    bq, s, _ = qkv.shape
    c, su, sd = _rope_tables(pos)
    seg = _seg_ones(WIDTH_A, HEAD_DIM_A)
    tile8 = lambda g: jnp.tile(g.astype(F32), N_HEADS_A)[None, :]
    hm = jax.ShapeDtypeStruct((bq, N_HEADS_A, s, HEAD_DIM_A), F32)
    hm_spec = pl.BlockSpec((1, N_HEADS_A, tm, HEAD_DIM_A), lambda b, j: (b, 0, j, 0))
    const = lambda b, j: (0, 0)
    tab = pl.BlockSpec((tm, 128), lambda b, j: (j, 0))
    return pl.pallas_call(
        _moba_prep_kernel,
        grid=(bq, s // tm),
        in_specs=[pl.BlockSpec((1, tm, 3 * WIDTH_A), lambda b, j: (b, j, 0)),
                  pl.BlockSpec((WIDTH_A, WIDTH_A), const),
                  pl.BlockSpec((1, WIDTH_A), const), pl.BlockSpec((1, WIDTH_A), const), tab, tab, tab],
        out_specs=[hm_spec, hm_spec, hm_spec],
        out_shape=[hm, hm, hm],
        compiler_params=_cparams(("parallel", "parallel")),
        name="moba_prep",
    )(qkv, seg, tile8(q_gain), tile8(k_gain), c, su, sd)


def _projection_and_moba_prep(x2, gain, ws, bq, pos, q_gain, k_gain):
    n, d = x2.shape
    tm = MOBA_BLOCK
    s = n // bq
    nb = s // tm
    assert nb <= HEAD_DIM_A
    c, su, sd = _rope_tables(pos)
    seg = _seg_ones(WIDTH_A, HEAD_DIM_A)
    tile8 = lambda g: jnp.tile(g.astype(F32), N_HEADS_A)[None, :]
    widths = [w.shape[1] for w in ws]
    const = lambda i: (0, 0)
    tok = lambda wd: pl.BlockSpec((tm, wd), lambda i: (i, 0))
    hm_spec = pl.BlockSpec((1, N_HEADS_A, tm, HEAD_DIM_A), lambda i: (i // nb, 0, i % nb, 0))
    tr_spec = pl.BlockSpec((1, N_HEADS_A, HEAD_DIM_A, tm), lambda i: (i // nb, 0, 0, i % nb))
    aug_spec = pl.BlockSpec((1, N_HEADS_A, tm, 128), lambda i: (i // nb, 0, i % nb, 0))
    km_spec = pl.BlockSpec((1, N_HEADS_A, HEAD_DIM_A, HEAD_DIM_A), lambda i: (i // nb, 0, 0, 0))
    tab = pl.BlockSpec((tm, 128), lambda i: (i % nb, 0))
    hm = jax.ShapeDtypeStruct((bq, N_HEADS_A, s, HEAD_DIM_A), F32)
    tr = jax.ShapeDtypeStruct((bq, N_HEADS_A, HEAD_DIM_A, s), F32)
    aug = jax.ShapeDtypeStruct((bq, N_HEADS_A, s, 128), BF16)
    return pl.pallas_call(
        functools.partial(_proj_prep_kernel, nb),
        grid=(n // tm,),
        in_specs=[tok(d), pl.BlockSpec((1, d), const)] + [pl.BlockSpec((d, wd), const) for wd in widths]
        + [pl.BlockSpec((WIDTH_A, WIDTH_A), const), pl.BlockSpec((1, WIDTH_A), const),
           pl.BlockSpec((1, WIDTH_A), const), tab, tab, tab],
        out_specs=[tok(widths[1]), tok(widths[2]), tok(widths[3]),
                   hm_spec, tr_spec, tr_spec, aug_spec, aug_spec, km_spec],
        out_shape=[jax.ShapeDtypeStruct((n, widths[1]), F32), jax.ShapeDtypeStruct((n, widths[2]), F32),
                   jax.ShapeDtypeStruct((n, widths[3]), BF16), hm, tr, tr, aug, aug,
                   jax.ShapeDtypeStruct((bq, N_HEADS_A, HEAD_DIM_A, HEAD_DIM_A), F32)],
        compiler_params=_cparams(("arbitrary",)),
        name="projection_moba_prep",
    )(x2, gain, *ws, seg, tile8(q_gain), tile8(k_gain), c, su, sd)


FLASH_HEADS = 4


def _moba_flash_kernel(q_ref, kaug_ref, vaug_ref, kmean_ref, o_ref, m_sc, acc_sc):
    i = pl.program_id(1)
    hp = pl.program_id(2)
    tq, dh = q_ref.shape[2], q_ref.shape[3]
    gs = range(FLASH_HEADS)
    q = [q_ref[0, g] for g in gs]
    gate_t = [_dot3(kmean_ref[0, g], q[g], 1, 1) for g in gs]
    n_idx = lax.broadcasted_iota(I32, gate_t[0].shape, 0)
    n_tot = gate_t[0].shape[0]
    gv = [jnp.where(n_idx < i, gate_t[g], -jnp.inf) for g in gs]
    sel = [n_idx == i for g in gs]
    for _ in range(MOBA_TOPK):
        mx = [jnp.max(gv[g], axis=0, keepdims=True) for g in gs]
        cand = [(gv[g] == mx[g]) & (mx[g] > -jnp.inf) for g in gs]
        first = [jnp.min(jnp.where(cand[g], n_idx, n_tot), axis=0, keepdims=True) for g in gs]
        pick = [n_idx == first[g] for g in gs]
        sel = [sel[g] | pick[g] for g in gs]
        gv = [jnp.where(pick[g], -jnp.inf, gv[g]) for g in gs]
    zeros_t = jnp.zeros((dh, tq), F32)
    zeros_q = jnp.zeros((tq, dh), F32)
    bias = [jnp.concatenate([zeros_t, jnp.where(sel[g], 0.0, NEG_BIG)], axis=0).T for g in gs]
    qaug = [(jnp.concatenate([q[g], zeros_q], axis=1) + bias[g]).astype(BF16) for g in gs]

    start = pl.multiple_of(i * tq, tq)
    row = lax.broadcasted_iota(I32, (tq, tq), 0)
    col = lax.broadcasted_iota(I32, (tq, tq), 1)
    s = [jnp.where(col <= row, _dg(qaug[g], kaug_ref[0, g, pl.ds(start, tq), :], 1, 1), -jnp.inf) for g in gs]
    m0 = [jnp.max(s[g], axis=1, keepdims=True) for g in gs]
    for g in gs:
        m_sc[g] = jnp.broadcast_to(m0[g], (tq, 128))
        acc_sc[g] = _dg(jnp.exp(s[g] - m0[g]).astype(BF16), vaug_ref[0, g, pl.ds(start, tq), :], 1, 0)

    def step(off, width):
        sj = [_dg(qaug[g], kaug_ref[0, g, pl.ds(off, width), :], 1, 1) for g in gs]
        m_old = [m_sc[g] for g in gs]
        m_new = [jnp.maximum(m_old[g], jnp.max(sj[g], axis=1, keepdims=True)) for g in gs]
        pj = [jnp.exp(sj[g] - jnp.concatenate([m_new[g]] * (width // 128), axis=1)) for g in gs]
        for g in gs:
            acc_sc[g] = (jnp.exp(m_old[g] - m_new[g]) * acc_sc[g]
                         + _dg(pj[g].astype(BF16), vaug_ref[0, g, pl.ds(off, width), :], 1, 0))
            m_sc[g] = m_new[g]

    def quad(j, carry):
        step(pl.multiple_of(j * (4 * tq), 4 * tq), 4 * tq)
        return carry

    lax.fori_loop(0, i // 4, quad, 0)

    @pl.when(i % 4 >= 2)
    def _():
        step(pl.multiple_of((i // 4) * (4 * tq), 2 * tq), 2 * tq)

    @pl.when(i % 2 == 1)
    def _():
        step(pl.multiple_of((i - 1) * tq, tq), tq)

    outs = []
    for g in gs:
        acc = acc_sc[g]
        outs.append(acc[:, :dh] / acc[:, dh:dh + 1])
    out = jnp.concatenate(outs, axis=1).astype(o_ref.dtype)
    wd = FLASH_HEADS * dh
    for pp in range(N_HEADS_A // FLASH_HEADS):
        @pl.when(hp == pp)
        def _():
            o_ref[0, :, pp * wd:(pp + 1) * wd] = out


def _moba_flash(q_s, kaug, vaug, kmean):
    b, nh, s, dh = q_s.shape
    tq = MOBA_BLOCK
    nb = s // tq
    g = FLASH_HEADS
    return pl.pallas_call(
        _moba_flash_kernel,
        grid=(b, nb, nh // g),
        in_specs=[pl.BlockSpec((1, g, tq, dh), lambda bi, i, h: (bi, h, i, 0)),
                  pl.BlockSpec((1, g, s, 128), lambda bi, i, h: (bi, h, 0, 0)),
                  pl.BlockSpec((1, g, s, 128), lambda bi, i, h: (bi, h, 0, 0)),
                  pl.BlockSpec((1, g, dh, dh), lambda bi, i, h: (bi, h, 0, 0))],
        out_specs=pl.BlockSpec((1, tq, nh * dh), lambda bi, i, h: (bi, i, 0)),
        out_shape=jax.ShapeDtypeStruct((b, s, nh * dh), BF16),
        scratch_shapes=[pltpu.VMEM((g, tq, 128), F32), pltpu.VMEM((g, tq, 128), F32)],
        compiler_params=_cparams(("parallel", "parallel", "arbitrary")),
        name="moba_flash",
    )(q_s, kaug, vaug, kmean)


PAGES_PER_STEP = 32


def _page_mean_kernel(pt_ref, *refs):
    pages, out_ref = refs[:PAGES_PER_STEP], refs[PAGES_PER_STEP]
    s = pl.program_id(1)
    ppb = MOBA_BLOCK // PAGE_SIZE
    bps = PAGES_PER_STEP // ppb

    @pl.when(s == 0)
    def _():
        out_ref[...] = jnp.zeros_like(out_ref)

    lane = lax.broadcasted_iota(I32, out_ref.shape[2:], 1)
    for h in range(N_HEADS_A):
        acc = out_ref[0, h]
        for j in range(bps):
            tot = pages[ppb * j][h]
            for u in range(1, ppb):
                tot = tot + pages[ppb * j + u][h]
            col = jnp.sum(tot, axis=1, keepdims=True) * (1.0 / MOBA_BLOCK)
            acc = jnp.where(lane == s * bps + j, col, acc)
        out_ref[0, h] = acc


def _page_means(cache_kt, page_table, layer):
    db, n_pages = page_table.shape
    ppb = MOBA_BLOCK // PAGE_SIZE
    n_full = n_pages // ppb
    steps = n_full * ppb // PAGES_PER_STEP
    _, _, nh, dh, pg = cache_kt.shape

    def page_spec(u):
        return pl.BlockSpec((None, None, nh, dh, pg),
                            lambda b, s, pt: (layer, pt[b, s * PAGES_PER_STEP + u], 0, 0, 0))

    return pl.pallas_call(
        _page_mean_kernel,
        grid_spec=pltpu.PrefetchScalarGridSpec(
            num_scalar_prefetch=1,
            grid=(db, steps),
            in_specs=[page_spec(u) for u in range(PAGES_PER_STEP)],
            out_specs=pl.BlockSpec((1, nh, dh, n_full), lambda b, s, pt: (b, 0, 0, 0)),
        ),
        out_shape=jax.ShapeDtypeStruct((db, nh, dh, n_full), F32),
        compiler_params=_cparams(("parallel", "arbitrary")),
        name="page_means",
    )(page_table, *([cache_kt] * PAGES_PER_STEP))


def _sample_select_kernel(q_ref, km_ref, idx_ref):
    tp = q_ref.shape[2]
    nb = km_ref.shape[3]
    n_idx = lax.broadcasted_iota(I32, (tp, nb), 1)
    for h in range(N_HEADS_A):
        g = _dot3(q_ref[0, h], km_ref[0, h])
        cols = []
        for _ in range(MOBA_TOPK):
            mx = jnp.max(g, axis=1, keepdims=True)
            first = jnp.min(jnp.where(g == mx, n_idx, nb), axis=1, keepdims=True)
            cols.append(first)
            g = jnp.where(n_idx == first, -jnp.inf, g)
        idx_ref[0, h] = jnp.concatenate(cols, axis=1)


def _sample_select(q_s, kmean_t):
    db, nh, tp, dh = q_s.shape
    nb = kmean_t.shape[3]
    return pl.pallas_call(
        _sample_select_kernel,
        grid=(db,),
        in_specs=[pl.BlockSpec((1, nh, tp, dh), lambda b: (b, 0, 0, 0)),
                  pl.BlockSpec((1, nh, dh, nb), lambda b: (b, 0, 0, 0))],
        out_specs=pl.BlockSpec((1, nh, tp, MOBA_TOPK), lambda b: (b, 0, 0, 0)),
        out_shape=jax.ShapeDtypeStruct((db, nh, tp, MOBA_TOPK), I32),
        compiler_params=_cparams(("parallel",)),
        name="sample_select",
    )(q_s, kmean_t)


def _sample_attend_kernel(t_valid, layer, phys_ref, q_ref, kn_ref, vn_ref, ck_ref, cv_ref, o_ref,
                          kbuf, vbuf, sem):
    b = pl.program_id(0)
    t = pl.program_id(1)
    ppb = MOBA_BLOCK // PAGE_SIZE
    n_slab = MOBA_TOPK * ppb
    tp = q_ref.shape[2]

    n = b * t_valid + t
    slot = n % 2

    def copies(step, sl, h, u):
        page = phys_ref[(step * N_HEADS_A + h) * n_slab + u]
        return (pltpu.make_async_copy(ck_ref.at[layer, page, h], kbuf.at[sl, h, u], sem.at[sl, 0]),
                pltpu.make_async_copy(cv_ref.at[layer, page, h], vbuf.at[sl, h, u], sem.at[sl, 1]))

    def issue(step, sl):
        for h in range(N_HEADS_A):
            for u in range(n_slab):
                ck, cv = copies(step, sl, h, u)
                ck.start()
                cv.start()

    @pl.when(n == 0)
    def _():
        issue(0, 0)

    @pl.when(n + 1 < pl.num_programs(0) * t_valid)
    def _():
        issue(n + 1, 1 - slot)

    pltpu.make_async_copy(kbuf.at[slot], kbuf.at[slot], sem.at[slot, 0]).wait()
    pltpu.make_async_copy(vbuf.at[slot], vbuf.at[slot], sem.at[slot, 1]).wait()

    row = lax.broadcasted_iota(I32, (tp, 1), 0)
    key = lax.broadcasted_iota(I32, (tp, tp), 1)
    for h in range(N_HEADS_A):
        qh = q_ref[0, h].astype(BF16)
        s_own = _dg(qh, kn_ref[0, h].astype(BF16), 1, 1)
        s_own = jnp.where((key <= t) & (key < t_valid), s_own, -jnp.inf)
        s_sel = [_dg(qh, kbuf[slot, h, u].astype(BF16), 1, 0) for u in range(n_slab)]
        m = jnp.max(s_own, axis=1, keepdims=True)
        for sj in s_sel:
            m = jnp.maximum(m, jnp.max(sj, axis=1, keepdims=True))
        p_own = jnp.exp(s_own - m)
        l = jnp.sum(p_own, axis=1, keepdims=True)
        acc = _dg(p_own.astype(BF16), vn_ref[0, h].astype(BF16), 1, 0)
        for u, sj in enumerate(s_sel):
            pj = jnp.exp(sj - m)
            l = l + jnp.sum(pj, axis=1, keepdims=True)
            acc = acc + _dg(pj.astype(BF16), vbuf[slot, h, u].astype(BF16), 1, 1)
        out = acc / l
        o_ref[0, 0, h] = jnp.sum(jnp.where(row == t, out, 0.0), axis=0, keepdims=True)


def _sample_attend(q_s, k_new, v_new, cache_kt, cache_vt, phys, t_valid, layer):
    db, nh, tp, dh = q_s.shape
    n_slab = MOBA_TOPK * (MOBA_BLOCK // PAGE_SIZE)
    hm = pl.BlockSpec((1, nh, tp, dh), lambda b, t, ph: (b, 0, 0, 0))
    return pl.pallas_call(
        functools.partial(_sample_attend_kernel, t_valid, layer),
        grid_spec=pltpu.PrefetchScalarGridSpec(
            num_scalar_prefetch=1,
            grid=(db, t_valid),
            in_specs=[hm, hm, hm, pl.BlockSpec(memory_space=pl.ANY), pl.BlockSpec(memory_space=pl.ANY)],
            out_specs=pl.BlockSpec((1, 1, nh, 1, dh), lambda b, t, ph: (b, t, 0, 0, 0)),
            scratch_shapes=[pltpu.VMEM((2, nh, n_slab, dh, PAGE_SIZE), F32),
                            pltpu.VMEM((2, nh, n_slab, dh, PAGE_SIZE), F32),
                            pltpu.SemaphoreType.DMA((2, 2))],
        ),
        out_shape=jax.ShapeDtypeStruct((db, t_valid, nh, 1, dh), F32),
        compiler_params=_cparams(("arbitrary", "arbitrary")),
        name="sample_attend",
    )(phys, q_s, k_new, v_new, cache_kt, cache_vt)


def _rwkv_feat_kernel(period, t_valid, tiles_per_seq, cur_ref, prev_ref, mu_ref, w0_ref, dup_ref, a0_ref,
                      aup_ref, gup_ref, kk_ref, ka_ref, rk_ref, seg_ref,
                      r_o, lw_o, k_o, v_o, kk_o, b_o, g_o, bonus_o):
    cur = cur_ref[...]
    if tiles_per_seq:
        edge = prev_ref[7:8, :]
        edge = jnp.where(pl.program_id(0) % tiles_per_seq == 0, 0.0, edge)
        first = lax.broadcasted_iota(I32, (cur.shape[0], 1), 0) == 0
        prev = jnp.where(first, edge, pltpu.roll(cur, 1, 0))
    else:
        prev = prev_ref[...]
    xs = cur + (prev - cur) * mu_ref[...]
    w = WIDTH_B
    r, k, v = xs[:, :w], xs[:, w:2 * w], xs[:, 2 * w:3 * w]
    dw = xs[:, 3 * w:3 * w + DECAY_LORA]
    da = xs[:, 3 * w + DECAY_LORA:3 * w + DECAY_LORA + AAA_LORA]
    dg = xs[:, 3 * w + DECAY_LORA + AAA_LORA:]
    lw = -DECAY_SCALE * jax.nn.sigmoid(w0_ref[...] + _dot3(jnp.tanh(dw), dup_ref[...]))
    a = jax.nn.sigmoid(a0_ref[...] + _dot3(da, aup_ref[...]))
    g = _dot3(jax.nn.sigmoid(dg), gup_ref[...])
    seg = seg_ref[...]
    kkr = k * kk_ref[...]
    kk = kkr / jnp.maximum(jnp.sqrt(_dot2_exact_rhs(kkr * kkr, seg)), 1e-12)
    k2 = k * (1.0 + (a - 1.0) * ka_ref[...])
    bonus = _dot2_exact_rhs(r * k2 * rk_ref[...], seg) * v
    if t_valid < period:
        rows = lax.broadcasted_iota(I32, (cur.shape[0], 1), 0)
        valid = (rows % period) < t_valid
        lw = jnp.where(valid, lw, 0.0)
        kk = jnp.where(valid, kk, 0.0)
        k2 = jnp.where(valid, k2, 0.0)
    r_o[...] = r
    lw_o[...] = lw
    k_o[...] = k2
    v_o[...] = v
    kk_o[...] = kk
    b_o[...] = kk * a
    g_o[...] = g
    bonus_o[...] = bonus


def _rwkv_features(p_cur, p_prev, prm, tm, period, t_valid, tiles_per_seq=0):
    n = p_cur.shape[0]
    row = lambda v: v.reshape(1, -1).astype(F32)
    const = lambda i: (0, 0)
    params = [row(prm['rw_mu']), row(prm['rw_w0']), prm['rw_decay_up'], row(prm['rw_a0']), prm['rw_a_up'],
              prm['rw_g_up'], row(prm['rw_k_k']), row(prm['rw_k_a']), row(prm['rw_r_k']),
              _seg_ones(WIDTH_B, HEAD_DIM_B)]
    tok = pl.BlockSpec((tm, RWKV_COLS), lambda i: (i, 0))
    if tiles_per_seq:
        prev_spec = pl.BlockSpec((8, RWKV_COLS), lambda i: (jnp.maximum(i * (tm // 8) - 1, 0), 0))
    else:
        prev_spec = tok
    out = pl.BlockSpec((tm, WIDTH_B), lambda i: (i, 0))
    return pl.pallas_call(
        functools.partial(_rwkv_feat_kernel, period, t_valid, tiles_per_seq),
        grid=(n // tm,),
        in_specs=[tok, prev_spec] + [pl.BlockSpec(p.shape, const) for p in params],
        out_specs=[out] * 8,
        out_shape=[jax.ShapeDtypeStruct((n, WIDTH_B), F32)] * 8,
        compiler_params=_cparams(("parallel",)),
        name="rwkv_features",
    )(p_cur, p_prev, *params)


def _rwkv_chunk_kernel(L, r_ref, lw_ref, k_ref, v_ref, kk_ref, b_ref, g_ref, bonus_ref, s0_ref,
                       lnw_ref, lnb_ref, seg_ref, o_ref, sT_ref, st_sc):
    c = pl.program_id(1)
    nsub = r_ref.shape[1] // L
    nh, dh = N_HEADS_B, HEAD_DIM_B

    @pl.when(c == 0)
    def _():
        st_sc[...] = s0_ref[0]

    ri = lax.broadcasted_iota(I32, (L, L), 0)
    ci = lax.broadcasted_iota(I32, (L, L), 1)
    strict = ri > ci
    tri = jnp.where(ri >= ci, 1.0, 0.0).astype(BF16)
    eye_l = jnp.where(ri == ci, 1.0, 0.0)
    rk = lax.broadcasted_iota(I32, (dh, dh), 0)
    ck = lax.broadcasted_iota(I32, (dh, dh), 1)
    ri2 = lax.broadcasted_iota(I32, (L, 2 * L), 0)
    ci2 = lax.broadcasted_iota(I32, (L, 2 * L), 1)
    incl2 = ri2 >= jnp.where(ci2 >= L, ci2 - L, ci2)

    alpha, beta, kappa, rho, vh, beta_t, kappa_t, g_last = [], [], [], [], [], [], [], []
    for sub in range(nsub):
        rows = slice(sub * L, (sub + 1) * L)
        lw = lw_ref[0, rows, :]
        cum = _dot2_exact_rhs_left(tri, lw)
        cum_last = cum[L - 1:L, :]
        e_neg = jnp.exp(-cum)
        e_tail = jnp.exp(cum_last - cum)
        gl_all = jnp.exp(cum_last)
        kk_all, b_all, k_all = kk_ref[0, rows, :], b_ref[0, rows, :], k_ref[0, rows, :]
        full = [kk_all * jnp.exp(cum - lw), b_all * e_neg, k_all * e_neg, r_ref[0, rows, :] * jnp.exp(cum),
                v_ref[0, rows, :], b_all * e_tail, k_all * e_tail]
        for dst, x in zip((alpha, beta, kappa, rho, vh, beta_t, kappa_t), full):
            dst.extend(x[:, h * dh:(h + 1) * dh] for h in range(nh))
        g_last.extend(gl_all[:, h * dh:(h + 1) * dh] for h in range(nh))
    its = range(nsub * nh)
    wcat = [jnp.concatenate([beta[i], kappa[i]], axis=0) for i in its]
    za = [_dot3(alpha[i], wcat[i], 1, 1) for i in its]
    zr = [_dotb(rho[i], wcat[i], 1, 1) for i in its]
    n_mat = [jnp.where(strict, za[i][:, :L], 0.0) for i in its]
    m_mat = [jnp.where(strict, za[i][:, L:], 0.0) for i in its]
    nrmr = [jnp.where(incl2, zr[i], 0.0) for i in its]
    mv = [_dot3(m_mat[i], vh[i]) for i in its]
    d = [eye_l - jnp.where(ri // 2 == ci // 2, n_mat[i], 0.0) for i in its]
    s = 2
    while s < L:
        lower_left = (ri // (2 * s) == ci // (2 * s)) & ((ri % (2 * s)) >= s) & ((ci % (2 * s)) < s)
        de = [_dot3(d[i], jnp.where(lower_left, n_mat[i], 0.0)) for i in its]
        d = [d[i] - _dot3(de[i], d[i]) for i in its]
        s *= 2
    ta = [_dot3(d[i], jnp.concatenate([alpha[i], mv[i]], axis=1)) for i in its]
    abar = [ta[i][:, :dh] for i in its]
    pv = [jnp.concatenate([-ta[i][:, dh:], vh[i]], axis=0) for i in its]
    rpp = [rho[i] - _dotb(nrmr[i][:, :L], abar[i]) for i in its]
    y0 = [_dotb(nrmr[i], pv[i]) for i in its]
    gt = [jnp.where(rk == ck, g_last[i], 0.0) - _dot3(beta_t[i], abar[i], 0, 0) for i in its]
    ht = [_dot3(jnp.concatenate([beta_t[i], kappa_t[i]], axis=0), pv[i], 0, 0) for i in its]
    ys = []
    for sub in range(nsub):
        base = sub * nh
        upd = [_dot3(jnp.concatenate([rpp[base + h], gt[base + h]], axis=0), st_sc[h])
               for h in range(nh)]
        for h in range(nh):
            st_sc[h] = upd[h][L:] + ht[base + h]
        ys.append(jnp.concatenate([y0[base + h] + upd[h][:L] for h in range(nh)], axis=1))
    y = ys[0] if nsub == 1 else jnp.concatenate(ys, axis=0)
    seg = seg_ref[...]
    mu = _dot2_exact_rhs(y, seg) * (1.0 / dh)
    yc = y - mu
    var = _dot2_exact_rhs(yc * yc, seg) * (1.0 / dh)
    yn = yc * lax.rsqrt(var + LN_X_EPS) * lnw_ref[...] + lnb_ref[...]
    o_ref[0] = ((yn + bonus_ref[0]) * g_ref[0]).astype(o_ref.dtype)
    sT_ref[0] = st_sc[...]


def _dot2_exact_rhs_left(m_bf16, x):
    xh, xl = _split(x)
    return _dg(m_bf16, xh, 1, 0) + _dg(m_bf16, xl, 1, 0)


def _rwkv_chunked(feats, s0_t, ln_w, ln_b, chunk, chunks_per_step):
    r, lw, k2, v, kk, b, g, bonus = feats
    bq, s, w = r.shape
    rows = chunk * chunks_per_step
    tok = pl.BlockSpec((1, rows, w), lambda bi, c: (bi, c, 0))
    st = pl.BlockSpec((1, N_HEADS_B, HEAD_DIM_B, HEAD_DIM_B), lambda bi, c: (bi, 0, 0, 0))
    const = lambda bi, c: (0, 0)
    return pl.pallas_call(
        functools.partial(_rwkv_chunk_kernel, chunk),
        grid=(bq, s // rows),
        in_specs=[tok] * 8 + [st, pl.BlockSpec((1, w), const), pl.BlockSpec((1, w), const),
                              pl.BlockSpec((w, w), const)],
        out_specs=[tok, st],
        out_shape=[jax.ShapeDtypeStruct((bq, s, w), BF16),
                   jax.ShapeDtypeStruct((bq, N_HEADS_B, HEAD_DIM_B, HEAD_DIM_B), F32)],
        scratch_shapes=[pltpu.VMEM((N_HEADS_B, HEAD_DIM_B, HEAD_DIM_B), F32)],
        compiler_params=_cparams(("parallel", "arbitrary")),
        name="rwkv_chunked",
    )(r, lw, k2, v, kk, b, g, bonus, s0_t, ln_w.reshape(1, w), ln_b.reshape(1, w),
      _seg_ones(w, HEAD_DIM_B))


def _mem_kv_kernel(mem_ref, g_ref, w_ref, kg_ref, mk_ref, mv_ref):
    h = _rms(mem_ref[0], g_ref[...]).astype(BF16)
    kv = _dg(h, w_ref[...], 1, 0)
    for hm in range(N_HEADS_M):
        sl = slice(hm * HEAD_DIM_M, (hm + 1) * HEAD_DIM_M)
        mk_ref[0, :, sl] = _rms(kv[:, sl], kg_ref[...])
    mv_ref[0] = kv[:, WIDTH_M:]


def _memory_kv(mem, norm_mem, w_mem_kv, k_norm_m):
    b, m, d = mem.shape
    const = lambda i: (0, 0)
    out = pl.BlockSpec((1, m, WIDTH_M), lambda i: (i, 0, 0))
    return pl.pallas_call(
        _mem_kv_kernel,
        grid=(b,),
        in_specs=[pl.BlockSpec((1, m, d), lambda i: (i, 0, 0)), pl.BlockSpec((1, d), const),
                  pl.BlockSpec((d, 2 * WIDTH_M), const), pl.BlockSpec((1, HEAD_DIM_M), const)],
        out_specs=[out, out],
        out_shape=[jax.ShapeDtypeStruct((b, m, WIDTH_M), F32)] * 2,
        compiler_params=_cparams(("parallel",)),
        name="memory_kv",
    )(mem, norm_mem.reshape(1, d), w_mem_kv.astype(BF16), k_norm_m.reshape(1, HEAD_DIM_M))


def _mem_attend_kernel(q_ref, mk_ref, mv_ref, g_ref, o_ref):
    q = q_ref[0]
    for hm in range(N_HEADS_M):
        sl = slice(hm * HEAD_DIM_M, (hm + 1) * HEAD_DIM_M)
        qh = (_rms(q[:, sl], g_ref[...]) * (HEAD_DIM_M ** -0.5)).astype(BF16)
        s = _dg(qh, mk_ref[0, :, sl].astype(BF16), 1, 1)
        p = jnp.exp(s - jnp.max(s, axis=1, keepdims=True))
        o = _dg(p.astype(BF16), mv_ref[0, :, sl].astype(BF16), 1, 0) / jnp.sum(p, axis=1, keepdims=True)
        o_ref[0, :, sl] = o.astype(o_ref.dtype)


def _memory_attend(q_m, mk, mv, q_norm_m, tq):
    b, s, w = q_m.shape
    m = mk.shape[1]
    kv = pl.BlockSpec((1, m, w), lambda bi, j: (bi, 0, 0))
    return pl.pallas_call(
        _mem_attend_kernel,
        grid=(b, s // tq),
        in_specs=[pl.BlockSpec((1, tq, w), lambda bi, j: (bi, j, 0)), kv, kv,
                  pl.BlockSpec((1, HEAD_DIM_M), lambda bi, j: (0, 0))],
        out_specs=pl.BlockSpec((1, tq, w), lambda bi, j: (bi, j, 0)),
        out_shape=jax.ShapeDtypeStruct((b, s, w), BF16),
        compiler_params=_cparams(("parallel", "parallel")),
        name="memory_attend",
    )(q_m, mk, mv, q_norm_m.reshape(1, HEAD_DIM_M))


def _merge_kernel(x_ref, oa_ref, ob_ref, om_ref, gl_ref, wb_ref, wo_ref, ng_ref, rw_ref, rb_ref,
                  x1_ref, h_ref, e_ref, gate_ref, rank_ref, cnt_ref, base_sc):
    d = x_ref.shape[1]
    gl = gl_ref[...]
    merged = jnp.zeros(x_ref.shape, F32)
    for n, o_ref in enumerate((oa_ref, ob_ref, om_ref)):
        y = _dg(o_ref[...], wb_ref[n], 1, 0)
        merged = merged + jax.nn.sigmoid(gl[:, n * d:(n + 1) * d].astype(F32)) * y
    x1 = x_ref[...] + _dg(merged.astype(BF16), wo_ref[...], 1, 0)
    x1_ref[...] = x1
    hn = _rms(x1, ng_ref[...])
    h_ref[...] = hn
    logits = _dot3(hn, rw_ref[...]) + rb_ref[...]
    tm, ne = logits.shape
    e_idx = lax.broadcasted_iota(I32, (tm, ne), 1)
    vals, idxs = [], []
    g = logits
    for _ in range(TOP_K):
        mx = jnp.max(g, axis=1, keepdims=True)
        first = jnp.min(jnp.where(g == mx, e_idx, ne), axis=1, keepdims=True)
        vals.append(mx)
        idxs.append(first)
        g = jnp.where(e_idx == first, -jnp.inf, g)
    top = jnp.concatenate(vals, axis=1)
    pe = jnp.exp(top - vals[0])
    gate_ref[...] = pe / jnp.sum(pe, axis=1, keepdims=True)
    e_ref[...] = jnp.concatenate(idxs, axis=1)

    @pl.when(pl.program_id(0) == 0)
    def _():
        base_sc[...] = jnp.zeros_like(base_sc)

    ohs = [jnp.where(e_idx == idx, 1.0, 0.0) for idx in idxs]
    cnt = ohs[0] + ohs[1] + ohs[2] + ohs[3]
    ri = lax.broadcasted_iota(I32, (tm, tm), 0)
    ci = lax.broadcasted_iota(I32, (tm, tm), 1)
    tri = jnp.where(ri > ci, 1.0, 0.0).astype(BF16)
    tot = _dg(tri, cnt.astype(BF16), 1, 0) + base_sc[...]
    rank_ref[...] = jnp.concatenate([jnp.sum(oh * tot, axis=1, keepdims=True) for oh in ohs],
                                    axis=1).astype(I32)
    base_sc[...] = base_sc[...] + jnp.sum(cnt, axis=0, keepdims=True)
    cnt_ref[...] = base_sc[...].astype(I32)


def _merge_and_route(x2, o_a, o_b, o_m, gl, prm, tm):
    n, d = x2.shape
    const2 = lambda i: (0, 0)
    tok = lambda wd: pl.BlockSpec((tm, wd), lambda i: (i, 0))
    return pl.pallas_call(
        _merge_kernel,
        grid=(n // tm,),
        in_specs=[tok(d), tok(512), tok(512), tok(512), tok(3 * d),
                  pl.BlockSpec((3, 512, d), lambda i: (0, 0, 0)), pl.BlockSpec((d, d), const2),
                  pl.BlockSpec((1, d), const2), pl.BlockSpec((d, N_EXPERTS), const2),
                  pl.BlockSpec((1, N_EXPERTS), const2)],
        out_specs=[tok(d), tok(d), tok(TOP_K), tok(TOP_K), tok(TOP_K), pl.BlockSpec((1, N_EXPERTS), const2)],
        out_shape=[jax.ShapeDtypeStruct((n, d), F32), jax.ShapeDtypeStruct((n, d), F32),
                   jax.ShapeDtypeStruct((n, TOP_K), I32), jax.ShapeDtypeStruct((n, TOP_K), F32),
                   jax.ShapeDtypeStruct((n, TOP_K), I32), jax.ShapeDtypeStruct((1, N_EXPERTS), I32)],
        scratch_shapes=[pltpu.VMEM((1, N_EXPERTS), F32)],
        compiler_params=_cparams(("arbitrary",)),
        name="merge_route",
    )(x2, o_a, o_b, o_m, gl, prm['w_branch'].astype(BF16), prm['w_out'].astype(BF16),
      prm['norm_ffn'].reshape(1, d), prm['router_w'], prm['router_b'].reshape(1, N_EXPERTS))


def _onehots(e):
    tm = e.shape[0]
    e_idx = lax.broadcasted_iota(I32, (tm, N_EXPERTS), 1)
    return [jnp.where(e[:, k:k + 1] == e_idx, 1.0, 0.0) for k in range(TOP_K)]


def _moe_rows_kernel(e_ref, rank_ref, start_ref, row_ref):
    ohs = _onehots(e_ref[...])
    st = start_ref[...].astype(F32)
    base = jnp.concatenate([jnp.sum(oh * st, axis=1, keepdims=True) for oh in ohs], axis=1)
    row_ref[...] = rank_ref[...] + base.astype(I32)


def _moe_rows(top_e, rank, starts, tm):
    n = top_e.shape[0]
    tok = pl.BlockSpec((tm, TOP_K), lambda i: (i, 0))
    return pl.pallas_call(
        _moe_rows_kernel,
        grid=(n // tm,),
        in_specs=[tok, tok, pl.BlockSpec((1, N_EXPERTS), lambda i: (0, 0))],
        out_specs=tok,
        out_shape=jax.ShapeDtypeStruct((n, TOP_K), I32),
        compiler_params=_cparams(("parallel",)),
        name="moe_rows",
    )(top_e, rank, starts)


def _dispatch_kernel(row_ref, h_ref, xs_in_ref, xs_ref, sem):
    del xs_in_ref
    tm = h_ref.shape[0]

    def start(t, carry):
        for k in range(TOP_K):
            r = row_ref[t * TOP_K + k]
            pltpu.make_async_copy(h_ref.at[pl.ds(t, 1)], xs_ref.at[pl.ds(r, 1)], sem).start()
        return carry

    lax.fori_loop(0, tm, start, 0, unroll=2)
    all_rows = xs_ref.at[pl.ds(0, tm * TOP_K)]
    pltpu.make_async_copy(all_rows, all_rows, sem).wait()


def _zero_rows_kernel(o_ref):
    o_ref[...] = jnp.zeros_like(o_ref)


def _zero_rows(n_rows, d):
    tile = next(t for t in (1024, 512, MOE_ROWS) if n_rows % t == 0)
    return pl.pallas_call(
        _zero_rows_kernel,
        grid=(n_rows // tile,),
        out_specs=pl.BlockSpec((tile, d), lambda i: (i, 0)),
        out_shape=jax.ShapeDtypeStruct((n_rows, d), F32),
        compiler_params=_cparams(("parallel",)),
        name="zero_rows",
    )()


def _moe_dispatch(h, row_flat, n_rows, tm):
    n, d = h.shape
    xs0 = _zero_rows(n_rows, d)
    return pl.pallas_call(
        _dispatch_kernel,
        grid=(n // tm,),
        in_specs=[pl.BlockSpec((tm * TOP_K,), lambda i: (i,), memory_space=pltpu.SMEM),
                  pl.BlockSpec((tm, d), lambda i: (i, 0)),
                  pl.BlockSpec(memory_space=pl.ANY)],
        out_specs=pl.BlockSpec(memory_space=pl.ANY),
        out_shape=jax.ShapeDtypeStruct((n_rows, d), F32),
        scratch_shapes=[pltpu.SemaphoreType.DMA(())],
        input_output_aliases={2: 0},
        compiler_params=_cparams(("arbitrary",)),
        name="moe_dispatch",
    )(row_flat, h, xs0)


def _swiglu(u, d_ff):
    u_glu = jnp.minimum(u[:, :d_ff], SWIGLU_LIMIT)
    u_lin = jnp.clip(u[:, d_ff:], -SWIGLU_LIMIT, SWIGLU_LIMIT)
    return u_glu * jax.nn.sigmoid(SWIGLU_ALPHA * u_glu) * (u_lin + 1.0)


def _moe_ffn_kernel(be_ref, nu_ref, xs_ref, w1_ref, b1_ref, w2_ref, b2_ref, y_ref, w1_sc, w2_sc):
    i = pl.program_id(0)
    prev = be_ref[jnp.maximum(i - 1, 0)]
    first = (i == 0) | (be_ref[i] != prev)

    @pl.when(first)
    def _():
        w1_sc[...] = w1_ref[...].astype(BF16)
        w2_sc[...] = w2_ref[...].astype(BF16)

    @pl.when(i < nu_ref[0])
    def _():
        u = _dg(xs_ref[...].astype(BF16), w1_sc[...], 1, 0) + b1_ref[...]
        act = _swiglu(u, w2_ref.shape[0])
        y_ref[...] = _dg(act.astype(BF16), w2_sc[...], 1, 0) + b2_ref[...]

    @pl.when(i >= nu_ref[0])
    def _():
        y_ref[...] = jnp.zeros_like(y_ref)


def _moe_ffn(xs, blk_e, n_used, w1, b1, w2, b2):
    n_rows, d = xs.shape
    ne, _, f2 = w1.shape
    d_ff = w2.shape[1]
    nblk = n_rows // MOE_ROWS
    return pl.pallas_call(
        _moe_ffn_kernel,
        grid_spec=pltpu.PrefetchScalarGridSpec(
            num_scalar_prefetch=2,
            grid=(nblk,),
            in_specs=[pl.BlockSpec((MOE_ROWS, d), lambda i, be, nu: (i, 0)),
                      pl.BlockSpec((None, d, f2), lambda i, be, nu: (be[i], 0, 0)),
                      pl.BlockSpec((None, 1, f2), lambda i, be, nu: (be[i], 0, 0)),
                      pl.BlockSpec((None, d_ff, d), lambda i, be, nu: (be[i], 0, 0)),
                      pl.BlockSpec((None, 1, d), lambda i, be, nu: (be[i], 0, 0))],
            out_specs=pl.BlockSpec((MOE_ROWS, d), lambda i, be, nu: (i, 0)),
            scratch_shapes=[pltpu.VMEM((d, f2), BF16), pltpu.VMEM((d_ff, d), BF16)],
        ),
        out_shape=jax.ShapeDtypeStruct((n_rows, d), F32),
        compiler_params=_cparams(("arbitrary",)),
        name="moe_ffn",
    )(blk_e, n_used, xs, w1, b1.reshape(ne, 1, f2), w2, b2.reshape(ne, 1, d))


def _combine_kernel(row_ref, yb_ref, x1_ref, gate_ref, y_ref, buf, sem):
    tm = x1_ref.shape[0]

    def start(t, carry):
        for k in range(TOP_K):
            r = row_ref[t * TOP_K + k]
            pltpu.make_async_copy(yb_ref.at[pl.ds(r, 1)], buf.at[pl.ds(k * tm + t, 1)], sem).start()
        return carry

    lax.fori_loop(0, tm, start, 0, unroll=2)
    pltpu.make_async_copy(yb_ref.at[pl.ds(0, tm * TOP_K)], buf, sem).wait()
    gates = gate_ref[...]
    acc = buf[0:tm] * gates[:, 0:1]
    for k in range(1, TOP_K):
        acc = acc + buf[k * tm:(k + 1) * tm] * gates[:, k:k + 1]
    y_ref[...] = x1_ref[...] + acc


def _moe_combine(yb, row_flat, x1, gates, tm):
    n, d = x1.shape
    return pl.pallas_call(
        _combine_kernel,
        grid=(n // tm,),
        in_specs=[pl.BlockSpec((tm * TOP_K,), lambda i: (i,), memory_space=pltpu.SMEM),
                  pl.BlockSpec(memory_space=pl.ANY),
                  pl.BlockSpec((tm, d), lambda i: (i, 0)),
                  pl.BlockSpec((tm, TOP_K), lambda i: (i, 0))],
        out_specs=pl.BlockSpec((tm, d), lambda i: (i, 0)),
        out_shape=jax.ShapeDtypeStruct((n, d), F32),
        scratch_shapes=[pltpu.VMEM((TOP_K * tm, d), F32), pltpu.SemaphoreType.DMA(())],
        compiler_params=_cparams(("arbitrary",)),
        name="moe_combine",
    )(row_flat, yb, x1, gates)


def _moe_block(x1, hn, top_e, gates, rank, counts, prm, tm):
    n, d = x1.shape
    counts = counts[0]
    padded = (counts + MOE_ROWS - 1) // MOE_ROWS * MOE_ROWS
    p_end = jnp.cumsum(padded)
    starts = (p_end - padded).astype(I32)
    nblk = -(-(n * TOP_K + N_EXPERTS * (MOE_ROWS - 1)) // MOE_ROWS)
    n_used = (p_end[-1] // MOE_ROWS).astype(I32)
    blk_i = jnp.minimum(jnp.arange(nblk, dtype=I32), n_used - 1)
    n_before = jnp.sum((p_end[None, :] <= (blk_i * MOE_ROWS)[:, None]).astype(I32), axis=1)
    blk_e = jnp.minimum(n_before, N_EXPERTS - 1).astype(I32)
    row = _moe_rows(top_e, rank, starts[None, :], tm)
    row_flat = row.reshape(-1)
    xs = _moe_dispatch(hn, row_flat, nblk * MOE_ROWS, tm)
    yb = _moe_ffn(xs, blk_e, n_used[None], prm['moe_w1'], prm['moe_b1'], prm['moe_w2'], prm['moe_b2'])
    return _moe_combine(yb, row_flat, x1, gates, tm)


def _split_w_in(w_in):
    d = w_in.shape[0]
    a = 3 * WIDTH_A
    b = a + RWKV_COLS
    c = b + WIDTH_M
    wb = w_in.astype(BF16)
    return [wb[:, :a], wb[:, a:b], wb[:, b:c], wb[:, c:]]


def _token_mix_tail(x2, o_a, o_b, o_m, gl, prm, tm):
    x1, hn, top_e, gates, rank, counts = _merge_and_route(x2, o_a, o_b, o_m, gl, prm, tm)
    return _moe_block(x1, hn, top_e, gates, rank, counts, prm, tm)


def _layer_prompt(x, mem, prm, ws):
    b, s, d = x.shape
    tm = 256
    x2 = x.reshape(b * s, d)
    pos = jnp.arange(s, dtype=I32)
    p_rw, q_m, gl, q_s, k_t, v_t, kaug, vaug, kmean = _projection_and_moba_prep(
        x2, prm['norm_mix'].reshape(1, d), ws, b, pos, prm['q_norm_a'], prm['k_norm_a'])
    k_o, v_o = jnp.swapaxes(k_t, 2, 3), jnp.swapaxes(v_t, 2, 3)
    o_a = _moba_flash(q_s, kaug, vaug, kmean).reshape(b * s, WIDTH_A)

    p3 = p_rw.reshape(b, s, RWKV_COLS)
    feats = _rwkv_features(p_rw, p_rw, prm, tm, 1, 1, tiles_per_seq=s // tm)
    feats = [f.reshape(b, s, WIDTH_B) for f in feats]
    s0_t = jnp.zeros((b, N_HEADS_B, HEAD_DIM_B, HEAD_DIM_B), F32)
    o_b, st = _rwkv_chunked(feats, s0_t, prm['ln_x_w'], prm['ln_x_b'], RWKV_CHUNK, RWKV_CHUNKS_PER_STEP)
    wkv = jnp.swapaxes(st, 2, 3)

    mk, mv = _memory_kv(mem, prm['norm_mem'], prm['w_mem_kv'], prm['k_norm_m'])
    o_m = _memory_attend(q_m.reshape(b, s, WIDTH_M), mk, mv, prm['q_norm_m'], 512)

    y = _token_mix_tail(x2, o_a, o_b.reshape(b * s, WIDTH_B), o_m.reshape(b * s, WIDTH_M), gl, prm, tm)
    m = mem.shape[1]
    return (y.reshape(b, s, d), k_o, v_o, wkv, p3[:, -1],
            mk.reshape(b, m, N_HEADS_M, HEAD_DIM_M), mv.reshape(b, m, N_HEADS_M, HEAD_DIM_M))


def _layer_sample(x, cache_k, cache_v, mem_k, mem_v, wkv0, shift0, page_table, layer, prm, ws):
    db, t, d = x.shape
    tp = SAMPLE_T_PAD
    past_len = page_table.shape[1] * PAGE_SIZE
    assert past_len % MOBA_BLOCK == 0 and t <= tp
    ppb = MOBA_BLOCK // PAGE_SIZE
    n = db * tp
    x2 = jnp.pad(x, ((0, 0), (0, tp - t), (0, 0))).reshape(n, d)
    qkv, p_rw, q_m, gl = _in_projection(x2, prm['norm_mix'].reshape(1, d), ws, n)
    pos = past_len + jnp.arange(tp, dtype=I32)
    q_s, k_o, v_o = _moba_prep(qkv.reshape(db, tp, -1), pos, prm['q_norm_a'], prm['k_norm_a'], tp)

    cache_kt, cache_vt = jnp.swapaxes(cache_k, 3, 4), jnp.swapaxes(cache_v, 3, 4)
    kmean_t = _page_means(cache_kt, page_table, layer)
    idx = _sample_select(q_s, kmean_t)
    idx = jnp.transpose(idx[:, :, :t], (0, 2, 1, 3))
    logical = idx[..., None] * ppb + jnp.arange(ppb, dtype=I32)
    phys = page_table[jnp.arange(db)[:, None, None, None, None], logical].reshape(-1).astype(I32)
    o_a = _sample_attend(q_s, k_o, v_o, cache_kt, cache_vt, phys, t, layer)
    o_a = jnp.pad(o_a.reshape(db, t, WIDTH_A), ((0, 0), (0, tp - t), (0, 0))).reshape(n, WIDTH_A).astype(BF16)

    p3 = p_rw.reshape(db, tp, RWKV_COLS)
    p_prev = jnp.concatenate([shift0[:, None, :], p3[:, :-1]], axis=1).reshape(n, RWKV_COLS)
    feats = _rwkv_features(p_rw, p_prev, prm, n, tp, t)
    feats = [f.reshape(db, tp, WIDTH_B) for f in feats]
    o_b, st = _rwkv_chunked(feats, jnp.swapaxes(wkv0, 2, 3), prm['ln_x_w'], prm['ln_x_b'], tp, 1)
    wkv = jnp.swapaxes(st, 2, 3)

    m = mem_k.shape[1]
    o_m = _memory_attend(q_m.reshape(db, tp, WIDTH_M), mem_k.reshape(db, m, WIDTH_M),
                         mem_v.reshape(db, m, WIDTH_M), prm['q_norm_m'], tp)

    y = _token_mix_tail(x2, o_a, o_b.reshape(n, WIDTH_B), o_m.reshape(n, WIDTH_M), gl, prm, n)
    return (y.reshape(db, tp, d)[:, :t], k_o[:, :, :t], v_o[:, :, :t], wkv, p3[:, t - 1])


def kernel(x_prompt, x_sample, mem_prompt, cache_k, cache_v, cache_mem_k, cache_mem_v, state_wkv, state_shift,
           page_table, norm_mix, norm_mem, norm_ffn, w_in, q_norm_a, k_norm_a, q_norm_m, k_norm_m, w_mem_kv,
           rw_mu, rw_w0, rw_decay_up, rw_a0, rw_a_up, rw_g_up, rw_k_k, rw_k_a, rw_r_k, ln_x_w, ln_x_b,
           w_branch, w_out, router_w, router_b, moe_w1, moe_b1, moe_w2, moe_b2):
    depth = w_in.shape[0]
    xp, xs = x_prompt, x_sample
    outs_p = [[] for _ in range(6)]
    outs_s = [[] for _ in range(4)]
    for l in range(depth):
        prm = dict(norm_mix=norm_mix[l], norm_mem=norm_mem[l], norm_ffn=norm_ffn[l], w_in=w_in[l],
                   q_norm_a=q_norm_a[l], k_norm_a=k_norm_a[l], q_norm_m=q_norm_m[l], k_norm_m=k_norm_m[l],
                   w_mem_kv=w_mem_kv[l], rw_mu=rw_mu[l], rw_w0=rw_w0[l], rw_decay_up=rw_decay_up[l],
                   rw_a0=rw_a0[l], rw_a_up=rw_a_up[l], rw_g_up=rw_g_up[l], rw_k_k=rw_k_k[l], rw_k_a=rw_k_a[l],
                   rw_r_k=rw_r_k[l].reshape(-1), ln_x_w=ln_x_w[l], ln_x_b=ln_x_b[l], w_branch=w_branch[l],
                   w_out=w_out[l], router_w=router_w[l], router_b=router_b[l], moe_w1=moe_w1[l],
                   moe_b1=moe_b1[l], moe_w2=moe_w2[l], moe_b2=moe_b2[l])
        ws = _split_w_in(prm['w_in'])
        xp, *rest_p = _layer_prompt(xp, mem_prompt, prm, ws)
        for acc, val in zip(outs_p, rest_p):
            acc.append(val)
        xs, *rest_s = _layer_sample(xs, cache_k, cache_v, cache_mem_k[l], cache_mem_v[l], state_wkv[l],
                                    state_shift[l], page_table, l, prm, ws)
        for acc, val in zip(outs_s, rest_s):
            acc.append(val)
    kp, vp, wkvp, shp, mkp, mvp = [jnp.stack(a) for a in outs_p]
    ksm, vsm, wkvs, shs = [jnp.stack(a) for a in outs_s]
    return (xp, xs, kp, vp, wkvp, shp, mkp, mvp, ksm, vsm, wkvs, shs)
```

```python
import functools
import math

import jax
import jax.numpy as jnp
from jax import lax
from jax.experimental import pallas as pl
from jax.experimental.pallas import tpu as pltpu

F32 = jnp.float32
BF16 = jnp.bfloat16
I32 = jnp.int32

N_HEADS_A = 8
HEAD_DIM_A = 64
WIDTH_A = 512
MOBA_BLOCK = 256
MOBA_TOPK = 3
ROT_DIM = 16
ROPE_THETA = 500000.0
PAGE_SIZE = 128
N_HEADS_B = 8
HEAD_DIM_B = 64
WIDTH_B = 512
DECAY_LORA = 64
AAA_LORA = 64
GATE_LORA = 128
RWKV_COLS = 1792
DECAY_SCALE = math.exp(-0.5)
LN_X_EPS = 64e-5
N_HEADS_M = 4
HEAD_DIM_M = 128
WIDTH_M = 512
N_EXPERTS = 32
TOP_K = 4
SWIGLU_ALPHA = 1.702
SWIGLU_LIMIT = 7.0
NORM_EPS = 1e-6

NEG_BIG = -1e30
SAMPLE_T_PAD = 8
RWKV_CHUNK = 64
RWKV_CHUNKS_PER_STEP = 4
MOE_ROWS = 512
MOE_ROWS_SMALL = 128
VMEM_LIMIT = 56 * 1024 * 1024


def _cparams(sem, vmem=None):
    return pltpu.CompilerParams(dimension_semantics=sem, vmem_limit_bytes=vmem or VMEM_LIMIT)


def _dg(a, b, ca, cb):
    return lax.dot_general(a, b, (((ca,), (cb,)), ((), ())), preferred_element_type=F32)


def _split(x):
    hi = x.astype(BF16)
    lo = (x - hi.astype(F32)).astype(BF16)
    return hi, lo


def _dot3(a, b, ca=1, cb=0):
    ah, al = _split(a)
    bh, bl = _split(b)
    return _dg(ah, bh, ca, cb) + (_dg(ah, bl, ca, cb) + _dg(al, bh, ca, cb))


def _dot2_exact_rhs(a, b_bf16):
    ah, al = _split(a)
    return _dg(ah, b_bf16, 1, 0) + _dg(al, b_bf16, 1, 0)


def _dotb(a, b, ca=1, cb=0):
    return _dg(a.astype(BF16), b.astype(BF16), ca, cb)


def _rms(x, gain_row):
    ms = jnp.mean(x * x, axis=-1, keepdims=True)
    return x * lax.rsqrt(ms + NORM_EPS) * gain_row


def _seg_ones(width, seg):
    r = lax.broadcasted_iota(I32, (width, width), 0) // seg
    c = lax.broadcasted_iota(I32, (width, width), 1) // seg
    return jnp.where(r == c, 1.0, 0.0).astype(BF16)


def _proj_kernel(x_ref, g_ref, w1, w2, w3, w4, o1, o2, o3, o4):
    h = _rms(x_ref[...], g_ref[...]).astype(BF16)
    o1[...] = _dg(h, w1[...], 1, 0)
    o2[...] = _dg(h, w2[...], 1, 0)
    o3[...] = _dg(h, w3[...], 1, 0)
    o4[...] = _dg(h, w4[...], 1, 0).astype(o4.dtype)


def _in_projection(x2, gain, ws, tm):
    n, d = x2.shape
    widths = [w.shape[1] for w in ws]
    dtypes = [F32, F32, F32, BF16]
    const = lambda i: (0, 0)
    return pl.pallas_call(
        _proj_kernel,
        grid=(n // tm,),
        in_specs=[pl.BlockSpec((tm, d), lambda i: (i, 0)), pl.BlockSpec((1, d), const)]
        + [pl.BlockSpec((d, wd), const) for wd in widths],
        out_specs=[pl.BlockSpec((tm, wd), lambda i: (i, 0)) for wd in widths],
        out_shape=[jax.ShapeDtypeStruct((n, wd), dt) for wd, dt in zip(widths, dtypes)],
        compiler_params=_cparams(("parallel",)),
        name="in_projection",
    )(x2, gain, *ws)


def _rope_tables(pos):
    half = ROT_DIM // 2
    inv_freq = 1.0 / (ROPE_THETA ** (jnp.arange(0, ROT_DIM, 2, dtype=F32) / ROT_DIM))
    ang = pos.astype(F32)[:, None] * inv_freq[None, :]
    cos, sin = jnp.cos(ang), jnp.sin(ang)
    n = pos.shape[0]
    rest = HEAD_DIM_A - ROT_DIM
    c = jnp.concatenate([cos, cos, jnp.ones((n, rest), F32)], axis=1)
    s_up = jnp.concatenate([-sin, jnp.zeros((n, half + rest), F32)], axis=1)
    s_dn = jnp.concatenate([jnp.zeros((n, half), F32), sin, jnp.zeros((n, rest), F32)], axis=1)
    two = lambda t: jnp.concatenate([t, t], axis=1)
    return two(c), two(s_up), two(s_dn)


def _norm_rope(x, seg, gain, c, s_up, s_dn):
    ss = _dot2_exact_rhs(x * x, seg)
    y = x * lax.rsqrt(ss * (1.0 / HEAD_DIM_A) + NORM_EPS) * gain
    half = ROT_DIM // 2
    up = pltpu.roll(y, WIDTH_A - half, 1)
    dn = pltpu.roll(y, half, 1)
    return y * c + up * s_up + dn * s_dn


def _moba_prep_kernel(qkv_ref, seg_ref, qg_ref, kg_ref, c_ref, su_ref, sd_ref, *outs):
    _prep_body(False, qkv_ref[0], None, seg_ref, qg_ref, kg_ref, c_ref, su_ref, sd_ref, outs)


def _proj_prep_kernel(nb, x_ref, g_ref, w1, w2, w3, w4, seg_ref, qg_ref, kg_ref, c_ref, su_ref, sd_ref,
                      o_rw, o_qm, o_gl, *outs):
    h = _rms(x_ref[...], g_ref[...]).astype(BF16)
    _prep_body(True, _dg(h, w1[...], 1, 0), pl.program_id(0) % nb, seg_ref, qg_ref, kg_ref,
               c_ref, su_ref, sd_ref, outs)
    o_rw[...] = _dg(h, w2[...], 1, 0)
    o_qm[...] = _dg(h, w3[...], 1, 0)
    o_gl[...] = _dg(h, w4[...], 1, 0).astype(o_gl.dtype)


def _prep_body(with_blocks, x, blk, seg_ref, qg_ref, kg_ref, c_ref, su_ref, sd_ref, outs):
    if with_blocks:
        qs_ref, kt_ref, vt_ref, kaug_ref, vaug_ref, kmean_ref = outs
    else:
        qs_ref, k_ref, v_ref = outs
    tm = x.shape[0]
    rep = lambda r: jnp.concatenate([r[...]] * (WIDTH_A // 128), axis=1)
    c, su, sd = rep(c_ref), rep(su_ref), rep(sd_ref)
    seg = seg_ref[...]
    q = _norm_rope(x[:, :WIDTH_A], seg, qg_ref[...], c, su, sd) * (HEAD_DIM_A ** -0.5)
    k = _norm_rope(x[:, WIDTH_A:2 * WIDTH_A], seg, kg_ref[...], c, su, sd)
    v = x[:, 2 * WIDTH_A:]
    if with_blocks:
        lane = lax.broadcasted_iota(I32, (tm, HEAD_DIM_A), 1)
        onehot = jnp.where(lane == blk, 1.0, 0.0).astype(BF16)
        ones_col = jnp.where(lane == 0, 1.0, 0.0).astype(BF16)

        @pl.when(blk == 0)
        def _():
            kmean_ref[...] = jnp.zeros_like(kmean_ref)

    for h in range(N_HEADS_A):
        sl = slice(h * HEAD_DIM_A, (h + 1) * HEAD_DIM_A)
        qs_ref[0, h] = q[:, sl]
        if with_blocks:
            kt_ref[0, h] = k[:, sl].T
            vt_ref[0, h] = v[:, sl].T
            kaug_ref[0, h] = jnp.concatenate([k[:, sl].astype(BF16), onehot], axis=1)
            vaug_ref[0, h] = jnp.concatenate([v[:, sl].astype(BF16), ones_col], axis=1)
            kmean_ref[0, h, pl.ds(blk, 1), :] = jnp.mean(k[:, sl], axis=0, keepdims=True)
        else:
            k_ref[0, h] = k[:, sl]
            v_ref[0, h] = v[:, sl]


def _moba_prep(qkv, pos, q_gain, k_gain, tm):
    bq, s, _ = qkv.shape
    c, su, sd = _rope_tables(pos)
    seg = _seg_ones(WIDTH_A, HEAD_DIM_A)
    tile8 = lambda g: jnp.tile(g.astype(F32), N_HEADS_A)[None, :]
    hm = jax.ShapeDtypeStruct((bq, N_HEADS_A, s, HEAD_DIM_A), F32)
    hm_spec = pl.BlockSpec((1, N_HEADS_A, tm, HEAD_DIM_A), lambda b, j: (b, 0, j, 0))
    const = lambda b, j: (0, 0)
    tab = pl.BlockSpec((tm, 128), lambda b, j: (j, 0))
    return pl.pallas_call(
        _moba_prep_kernel,
        grid=(bq, s // tm),
        in_specs=[pl.BlockSpec((1, tm, 3 * WIDTH_A), lambda b, j: (b, j, 0)),
                  pl.BlockSpec((WIDTH_A, WIDTH_A), const),
                  pl.BlockSpec((1, WIDTH_A), const), pl.BlockSpec((1, WIDTH_A), const), tab, tab, tab],
        out_specs=[hm_spec, hm_spec, hm_spec],
        out_shape=[hm, hm, hm],
        compiler_params=_cparams(("parallel", "parallel")),
        name="moba_prep",
    )(qkv, seg, tile8(q_gain), tile8(k_gain), c, su, sd)


def _projection_and_moba_prep(x2, gain, ws, bq, pos, q_gain, k_gain):
    n, d = x2.shape
    tm = MOBA_BLOCK
    s = n // bq
    nb = s // tm
    assert nb <= HEAD_DIM_A
    c, su, sd = _rope_tables(pos)
    seg = _seg_ones(WIDTH_A, HEAD_DIM_A)
    tile8 = lambda g: jnp.tile(g.astype(F32), N_HEADS_A)[None, :]
    widths = [w.shape[1] for w in ws]
    const = lambda i: (0, 0)
    tok = lambda wd: pl.BlockSpec((tm, wd), lambda i: (i, 0))
    hm_spec = pl.BlockSpec((1, N_HEADS_A, tm, HEAD_DIM_A), lambda i: (i // nb, 0, i % nb, 0))
    tr_spec = pl.BlockSpec((1, N_HEADS_A, HEAD_DIM_A, tm), lambda i: (i // nb, 0, 0, i % nb))
    aug_spec = pl.BlockSpec((1, N_HEADS_A, tm, 128), lambda i: (i // nb, 0, i % nb, 0))
    km_spec = pl.BlockSpec((1, N_HEADS_A, HEAD_DIM_A, HEAD_DIM_A), lambda i: (i // nb, 0, 0, 0))
    tab = pl.BlockSpec((tm, 128), lambda i: (i % nb, 0))
    hm = jax.ShapeDtypeStruct((bq, N_HEADS_A, s, HEAD_DIM_A), F32)
    tr = jax.ShapeDtypeStruct((bq, N_HEADS_A, HEAD_DIM_A, s), F32)
    aug = jax.ShapeDtypeStruct((bq, N_HEADS_A, s, 128), BF16)
    return pl.pallas_call(
        functools.partial(_proj_prep_kernel, nb),
        grid=(n // tm,),
        in_specs=[tok(d), pl.BlockSpec((1, d), const)] + [pl.BlockSpec((d, wd), const) for wd in widths]
        + [pl.BlockSpec((WIDTH_A, WIDTH_A), const), pl.BlockSpec((1, WIDTH_A), const),
           pl.BlockSpec((1, WIDTH_A), const), tab, tab, tab],
        out_specs=[tok(widths[1]), tok(widths[2]), tok(widths[3]),
                   hm_spec, tr_spec, tr_spec, aug_spec, aug_spec, km_spec],
        out_shape=[jax.ShapeDtypeStruct((n, widths[1]), F32), jax.ShapeDtypeStruct((n, widths[2]), F32),
                   jax.ShapeDtypeStruct((n, widths[3]), BF16), hm, tr, tr, aug, aug,
                   jax.ShapeDtypeStruct((bq, N_HEADS_A, HEAD_DIM_A, HEAD_DIM_A), F32)],
        compiler_params=_cparams(("arbitrary",)),
        name="projection_moba_prep",
    )(x2, gain, *ws, seg, tile8(q_gain), tile8(k_gain), c, su, sd)


FLASH_HEADS = 4


def _moba_flash_kernel(q_ref, kaug_ref, vaug_ref, kmean_ref, o_ref, m_sc, acc_sc):
    i = pl.program_id(1)
    hp = pl.program_id(2)
    tq, dh = q_ref.shape[2], q_ref.shape[3]
    gs = range(FLASH_HEADS)
    q = [q_ref[0, g] for g in gs]
    gate_t = [_dot3(kmean_ref[0, g], q[g], 1, 1) for g in gs]
    n_idx = lax.broadcasted_iota(I32, gate_t[0].shape, 0)
    n_tot = gate_t[0].shape[0]
    gv = [jnp.where(n_idx < i, gate_t[g], -jnp.inf) for g in gs]
    sel = [n_idx == i for g in gs]
    for _ in range(MOBA_TOPK):
        mx = [jnp.max(gv[g], axis=0, keepdims=True) for g in gs]
        cand = [(gv[g] == mx[g]) & (mx[g] > -jnp.inf) for g in gs]
        first = [jnp.min(jnp.where(cand[g], n_idx, n_tot), axis=0, keepdims=True) for g in gs]
        pick = [n_idx == first[g] for g in gs]
        sel = [sel[g] | pick[g] for g in gs]
        gv = [jnp.where(pick[g], -jnp.inf, gv[g]) for g in gs]
    zeros_t = jnp.zeros((dh, tq), F32)
    zeros_q = jnp.zeros((tq, dh), F32)
    bias = [jnp.concatenate([zeros_t, jnp.where(sel[g], 0.0, NEG_BIG)], axis=0).T for g in gs]
    qaug = [(jnp.concatenate([q[g], zeros_q], axis=1) + bias[g]).astype(BF16) for g in gs]

    start = pl.multiple_of(i * tq, tq)
    row = lax.broadcasted_iota(I32, (tq, tq), 0)
    col = lax.broadcasted_iota(I32, (tq, tq), 1)
    s = [jnp.where(col <= row, _dg(qaug[g], kaug_ref[0, g, pl.ds(start, tq), :], 1, 1), -jnp.inf) for g in gs]
    m0 = [jnp.max(s[g], axis=1, keepdims=True) for g in gs]
    for g in gs:
        m_sc[g] = jnp.broadcast_to(m0[g], (tq, 128))
        acc_sc[g] = _dg(jnp.exp(s[g] - m0[g]).astype(BF16), vaug_ref[0, g, pl.ds(start, tq), :], 1, 0)

    def step(off, width):
        sj = [_dg(qaug[g], kaug_ref[0, g, pl.ds(off, width), :], 1, 1) for g in gs]
        m_old = [m_sc[g] for g in gs]
        m_new = [jnp.maximum(m_old[g], jnp.max(sj[g], axis=1, keepdims=True)) for g in gs]
        pj = [jnp.exp(sj[g] - jnp.concatenate([m_new[g]] * (width // 128), axis=1)) for g in gs]
        for g in gs:
            acc_sc[g] = (jnp.exp(m_old[g] - m_new[g]) * acc_sc[g]
                         + _dg(pj[g].astype(BF16), vaug_ref[0, g, pl.ds(off, width), :], 1, 0))
            m_sc[g] = m_new[g]

    def quad(j, carry):
        step(pl.multiple_of(j * (4 * tq), 4 * tq), 4 * tq)
        return carry

    lax.fori_loop(0, i // 4, quad, 0)

    @pl.when(i % 4 >= 2)
    def _():
        step(pl.multiple_of((i // 4) * (4 * tq), 2 * tq), 2 * tq)

    @pl.when(i % 2 == 1)
    def _():
        step(pl.multiple_of((i - 1) * tq, tq), tq)

    outs = []
    for g in gs:
        acc = acc_sc[g]
        outs.append(acc[:, :dh] / acc[:, dh:dh + 1])
    out = jnp.concatenate(outs, axis=1).astype(o_ref.dtype)
    wd = FLASH_HEADS * dh
    for pp in range(N_HEADS_A // FLASH_HEADS):
        @pl.when(hp == pp)
        def _():
            o_ref[0, :, pp * wd:(pp + 1) * wd] = out


def _moba_flash(q_s, kaug, vaug, kmean):
    b, nh, s, dh = q_s.shape
    tq = MOBA_BLOCK
    nb = s // tq
    g = FLASH_HEADS
    return pl.pallas_call(
        _moba_flash_kernel,
        grid=(b, nb, nh // g),
        in_specs=[pl.BlockSpec((1, g, tq, dh), lambda bi, i, h: (bi, h, i, 0)),
                  pl.BlockSpec((1, g, s, 128), lambda bi, i, h: (bi, h, 0, 0)),
                  pl.BlockSpec((1, g, s, 128), lambda bi, i, h: (bi, h, 0, 0)),
                  pl.BlockSpec((1, g, dh, dh), lambda bi, i, h: (bi, h, 0, 0))],
        out_specs=pl.BlockSpec((1, tq, nh * dh), lambda bi, i, h: (bi, i, 0)),
        out_shape=jax.ShapeDtypeStruct((b, s, nh * dh), BF16),
        scratch_shapes=[pltpu.VMEM((g, tq, 128), F32), pltpu.VMEM((g, tq, 128), F32)],
        compiler_params=_cparams(("parallel", "parallel", "arbitrary")),
        name="moba_flash",
    )(q_s, kaug, vaug, kmean)


PAGES_PER_STEP = 32


def _page_mean_kernel(pt_ref, *refs):
    pages, out_ref = refs[:PAGES_PER_STEP], refs[PAGES_PER_STEP]
    s = pl.program_id(1)
    ppb = MOBA_BLOCK // PAGE_SIZE
    bps = PAGES_PER_STEP // ppb

    @pl.when(s == 0)
    def _():
        out_ref[...] = jnp.zeros_like(out_ref)

    lane = lax.broadcasted_iota(I32, out_ref.shape[2:], 1)
    for h in range(N_HEADS_A):
        acc = out_ref[0, h]
        for j in range(bps):
            tot = pages[ppb * j][h]
            for u in range(1, ppb):
                tot = tot + pages[ppb * j + u][h]
            col = jnp.sum(tot, axis=1, keepdims=True) * (1.0 / MOBA_BLOCK)
            acc = jnp.where(lane == s * bps + j, col, acc)
        out_ref[0, h] = acc


def _page_means(cache_kt, page_table, layer):
    db, n_pages = page_table.shape
    ppb = MOBA_BLOCK // PAGE_SIZE
    n_full = n_pages // ppb
    steps = n_full * ppb // PAGES_PER_STEP
    _, _, nh, dh, pg = cache_kt.shape

    def page_spec(u):
        return pl.BlockSpec((None, None, nh, dh, pg),
                            lambda b, s, pt: (layer, pt[b, s * PAGES_PER_STEP + u], 0, 0, 0))

    return pl.pallas_call(
        _page_mean_kernel,
        grid_spec=pltpu.PrefetchScalarGridSpec(
            num_scalar_prefetch=1,
            grid=(db, steps),
            in_specs=[page_spec(u) for u in range(PAGES_PER_STEP)],
            out_specs=pl.BlockSpec((1, nh, dh, n_full), lambda b, s, pt: (b, 0, 0, 0)),
        ),
        out_shape=jax.ShapeDtypeStruct((db, nh, dh, n_full), F32),
        compiler_params=_cparams(("parallel", "arbitrary")),
        name="page_means",
    )(page_table, *([cache_kt] * PAGES_PER_STEP))


def _sample_select_kernel(q_ref, km_ref, idx_ref):
    tp = q_ref.shape[2]
    nb = km_ref.shape[3]
    n_idx = lax.broadcasted_iota(I32, (tp, nb), 1)
    for h in range(N_HEADS_A):
        g = _dot3(q_ref[0, h], km_ref[0, h])
        cols = []
        for _ in range(MOBA_TOPK):
            mx = jnp.max(g, axis=1, keepdims=True)
            first = jnp.min(jnp.where(g == mx, n_idx, nb), axis=1, keepdims=True)
            cols.append(first)
            g = jnp.where(n_idx == first, -jnp.inf, g)
        idx_ref[0, h] = jnp.concatenate(cols, axis=1)


def _sample_select(q_s, kmean_t):
    db, nh, tp, dh = q_s.shape
    nb = kmean_t.shape[3]
    return pl.pallas_call(
        _sample_select_kernel,
        grid=(db,),
        in_specs=[pl.BlockSpec((1, nh, tp, dh), lambda b: (b, 0, 0, 0)),
                  pl.BlockSpec((1, nh, dh, nb), lambda b: (b, 0, 0, 0))],
        out_specs=pl.BlockSpec((1, nh, tp, MOBA_TOPK), lambda b: (b, 0, 0, 0)),
        out_shape=jax.ShapeDtypeStruct((db, nh, tp, MOBA_TOPK), I32),
        compiler_params=_cparams(("parallel",)),
        name="sample_select",
    )(q_s, kmean_t)


def _sample_attend_kernel(t_valid, layer, phys_ref, q_ref, kn_ref, vn_ref, ck_ref, cv_ref, o_ref,
                          kbuf, vbuf, sem):
    b = pl.program_id(0)
    t = pl.program_id(1)
    ppb = MOBA_BLOCK // PAGE_SIZE
    n_slab = MOBA_TOPK * ppb
    tp = q_ref.shape[2]

    n = b * t_valid + t
    slot = n % 2

    def copies(step, sl, h, u):
        page = phys_ref[(step * N_HEADS_A + h) * n_slab + u]
        return (pltpu.make_async_copy(ck_ref.at[layer, page, h], kbuf.at[sl, h, u], sem.at[sl, 0]),
                pltpu.make_async_copy(cv_ref.at[layer, page, h], vbuf.at[sl, h, u], sem.at[sl, 1]))

    def issue(step, sl):
        for h in range(N_HEADS_A):
            for u in range(n_slab):
                ck, cv = copies(step, sl, h, u)
                ck.start()
                cv.start()

    @pl.when(n == 0)
    def _():
        issue(0, 0)

    @pl.when(n + 1 < pl.num_programs(0) * t_valid)
    def _():
        issue(n + 1, 1 - slot)

    pltpu.make_async_copy(kbuf.at[slot], kbuf.at[slot], sem.at[slot, 0]).wait()
    pltpu.make_async_copy(vbuf.at[slot], vbuf.at[slot], sem.at[slot, 1]).wait()

    row = lax.broadcasted_iota(I32, (tp, 1), 0)
    key = lax.broadcasted_iota(I32, (tp, tp), 1)
    for h in range(N_HEADS_A):
        qh = q_ref[0, h].astype(BF16)
        s_own = _dg(qh, kn_ref[0, h].astype(BF16), 1, 1)
        s_own = jnp.where((key <= t) & (key < t_valid), s_own, -jnp.inf)
        s_sel = [_dg(qh, kbuf[slot, h, u].astype(BF16), 1, 0) for u in range(n_slab)]
        m = jnp.max(s_own, axis=1, keepdims=True)
        for sj in s_sel:
            m = jnp.maximum(m, jnp.max(sj, axis=1, keepdims=True))
        p_own = jnp.exp(s_own - m)
        l = jnp.sum(p_own, axis=1, keepdims=True)
        acc = _dg(p_own.astype(BF16), vn_ref[0, h].astype(BF16), 1, 0)
        for u, sj in enumerate(s_sel):
            pj = jnp.exp(sj - m)
            l = l + jnp.sum(pj, axis=1, keepdims=True)
            acc = acc + _dg(pj.astype(BF16), vbuf[slot, h, u].astype(BF16), 1, 1)
        out = acc / l
        o_ref[0, 0, h] = jnp.sum(jnp.where(row == t, out, 0.0), axis=0, keepdims=True)


def _sample_attend(q_s, k_new, v_new, cache_kt, cache_vt, phys, t_valid, layer):
    db, nh, tp, dh = q_s.shape
    n_slab = MOBA_TOPK * (MOBA_BLOCK // PAGE_SIZE)
    hm = pl.BlockSpec((1, nh, tp, dh), lambda b, t, ph: (b, 0, 0, 0))
    return pl.pallas_call(
        functools.partial(_sample_attend_kernel, t_valid, layer),
        grid_spec=pltpu.PrefetchScalarGridSpec(
            num_scalar_prefetch=1,
            grid=(db, t_valid),
            in_specs=[hm, hm, hm, pl.BlockSpec(memory_space=pl.ANY), pl.BlockSpec(memory_space=pl.ANY)],
            out_specs=pl.BlockSpec((1, 1, nh, 1, dh), lambda b, t, ph: (b, t, 0, 0, 0)),
            scratch_shapes=[pltpu.VMEM((2, nh, n_slab, dh, PAGE_SIZE), F32),
                            pltpu.VMEM((2, nh, n_slab, dh, PAGE_SIZE), F32),
                            pltpu.SemaphoreType.DMA((2, 2))],
        ),
        out_shape=jax.ShapeDtypeStruct((db, t_valid, nh, 1, dh), F32),
        compiler_params=_cparams(("arbitrary", "arbitrary")),
        name="sample_attend",
    )(phys, q_s, k_new, v_new, cache_kt, cache_vt)


def _rwkv_features(t_valid, cur, edge, mu_ref, w0_ref, dup_ref, a0_ref, aup_ref, gup_ref, kk_ref, ka_ref,
                   rk_ref, seg_ref):
    first = lax.broadcasted_iota(I32, (cur.shape[0], 1), 0) == 0
    prev = jnp.where(first, edge, pltpu.roll(cur, 1, 0))
    xs = cur + (prev - cur) * mu_ref[...]
    w = WIDTH_B
    r, k, v = xs[:, :w], xs[:, w:2 * w], xs[:, 2 * w:3 * w]
    dw = xs[:, 3 * w:3 * w + DECAY_LORA]
    da = xs[:, 3 * w + DECAY_LORA:3 * w + DECAY_LORA + AAA_LORA]
    dg = xs[:, 3 * w + DECAY_LORA + AAA_LORA:]
    lw = -DECAY_SCALE * jax.nn.sigmoid(w0_ref[...] + _dot3(jnp.tanh(dw), dup_ref[...]))
    a = jax.nn.sigmoid(a0_ref[...] + _dot3(da, aup_ref[...]))
    g = _dot3(jax.nn.sigmoid(dg), gup_ref[...])
    seg = seg_ref[...]
    kkr = k * kk_ref[...]
    kk = kkr / jnp.maximum(jnp.sqrt(_dot2_exact_rhs(kkr * kkr, seg)), 1e-12)
    k2 = k * (1.0 + (a - 1.0) * ka_ref[...])
    bonus = _dot2_exact_rhs(r * k2 * rk_ref[...], seg) * v
    if t_valid < cur.shape[0]:
        valid = lax.broadcasted_iota(I32, (cur.shape[0], 1), 0) < t_valid
        lw = jnp.where(valid, lw, 0.0)
        kk = jnp.where(valid, kk, 0.0)
        k2 = jnp.where(valid, k2, 0.0)
    return r, lw, k2, v, kk, kk * a, g, bonus


def _rwkv_chunk_kernel(L, t_valid, p_ref, sh_ref, mu_ref, w0_ref, dup_ref, a0_ref, aup_ref, gup_ref, kkp_ref,
                       ka_ref, rk_ref, s0_ref, lnw_ref, lnb_ref, seg_ref, o_ref, sT_ref, st_sc, last_sc):
    c = pl.program_id(1)
    n_rows = p_ref.shape[1]
    nsub = n_rows // L
    nh, dh = N_HEADS_B, HEAD_DIM_B

    @pl.when(c == 0)
    def _():
        st_sc[...] = s0_ref[0]
        last_sc[...] = sh_ref[0]

    cur = p_ref[0]
    r_f, lw_f, k_f, v_f, kk_f, b_f, g_f, bonus_f = _rwkv_features(
        t_valid, cur, last_sc[...], mu_ref, w0_ref, dup_ref, a0_ref, aup_ref, gup_ref, kkp_ref, ka_ref, rk_ref,
        seg_ref)
    last_sc[...] = cur[n_rows - 1:n_rows, :]

    ri = lax.broadcasted_iota(I32, (L, L), 0)
    ci = lax.broadcasted_iota(I32, (L, L), 1)
    strict = ri > ci
    tri = jnp.where(ri >= ci, 1.0, 0.0).astype(BF16)
    eye_l = jnp.where(ri == ci, 1.0, 0.0)
    rk = lax.broadcasted_iota(I32, (dh, dh), 0)
    ck = lax.broadcasted_iota(I32, (dh, dh), 1)
    ri2 = lax.broadcasted_iota(I32, (L, 2 * L), 0)
    ci2 = lax.broadcasted_iota(I32, (L, 2 * L), 1)
    incl2 = ri2 >= jnp.where(ci2 >= L, ci2 - L, ci2)

    alpha, beta, kappa, rho, vh, beta_t, kappa_t, g_last = [], [], [], [], [], [], [], []
    for sub in range(nsub):
        rows = slice(sub * L, (sub + 1) * L)
        lw = lw_f[rows]
        cum = _dot2_exact_rhs_left(tri, lw)
        cum_last = cum[L - 1:L, :]
        e_neg = jnp.exp(-cum)
        e_tail = jnp.exp(cum_last - cum)
        gl_all = jnp.exp(cum_last)
        kk_all, b_all, k_all = kk_f[rows], b_f[rows], k_f[rows]
        full = [kk_all * jnp.exp(cum - lw), b_all * e_neg, k_all * e_neg, r_f[rows] * jnp.exp(cum),
                v_f[rows], b_all * e_tail, k_all * e_tail]
        for dst, x in zip((alpha, beta, kappa, rho, vh, beta_t, kappa_t), full):
            dst.extend(x[:, h * dh:(h + 1) * dh] for h in range(nh))
        g_last.extend(gl_all[:, h * dh:(h + 1) * dh] for h in range(nh))
    its = range(nsub * nh)
    wcat = [jnp.concatenate([beta[i], kappa[i]], axis=0) for i in its]
    za = [_dot3(alpha[i], wcat[i], 1, 1) for i in its]
    zr = [_dotb(rho[i], wcat[i], 1, 1) for i in its]
    n_mat = [jnp.where(strict, za[i][:, :L], 0.0) for i in its]
    m_mat = [jnp.where(strict, za[i][:, L:], 0.0) for i in its]
    nrmr = [jnp.where(incl2, zr[i], 0.0) for i in its]
    mv = [_dot3(m_mat[i], vh[i]) for i in its]
    d = [eye_l - jnp.where(ri // 2 == ci // 2, n_mat[i], 0.0) for i in its]
    s = 2
    while s < L:
        lower_left = (ri // (2 * s) == ci // (2 * s)) & ((ri % (2 * s)) >= s) & ((ci % (2 * s)) < s)
        de = [_dot3(d[i], jnp.where(lower_left, n_mat[i], 0.0)) for i in its]
        d = [d[i] - _dot3(de[i], d[i]) for i in its]
        s *= 2
    ta = [_dot3(d[i], jnp.concatenate([alpha[i], mv[i]], axis=1)) for i in its]
    abar = [ta[i][:, :dh] for i in its]
    pv = [jnp.concatenate([-ta[i][:, dh:], vh[i]], axis=0) for i in its]
    rpp = [rho[i] - _dotb(nrmr[i][:, :L], abar[i]) for i in its]
    y0 = [_dotb(nrmr[i], pv[i]) for i in its]
    gt = [jnp.where(rk == ck, g_last[i], 0.0) - _dot3(beta_t[i], abar[i], 0, 0) for i in its]
    ht = [_dot3(jnp.concatenate([beta_t[i], kappa_t[i]], axis=0), pv[i], 0, 0) for i in its]
    ys = []
    for sub in range(nsub):
        base = sub * nh
        upd = [_dot3(jnp.concatenate([rpp[base + h], gt[base + h]], axis=0), st_sc[h])
               for h in range(nh)]
        for h in range(nh):
            st_sc[h] = upd[h][L:] + ht[base + h]
        ys.append(jnp.concatenate([y0[base + h] + upd[h][:L] for h in range(nh)], axis=1))
    y = ys[0] if nsub == 1 else jnp.concatenate(ys, axis=0)
    seg = seg_ref[...]
    mu = _dot2_exact_rhs(y, seg) * (1.0 / dh)
    yc = y - mu
    var = _dot2_exact_rhs(yc * yc, seg) * (1.0 / dh)
    yn = yc * lax.rsqrt(var + LN_X_EPS) * lnw_ref[...] + lnb_ref[...]
    o_ref[0] = ((yn + bonus_f) * g_f).astype(o_ref.dtype)
    sT_ref[0] = st_sc[...]


def _dot2_exact_rhs_left(m_bf16, x):
    xh, xl = _split(x)
    return _dg(m_bf16, xh, 1, 0) + _dg(m_bf16, xl, 1, 0)


def _rwkv_mix(p_rw, shift0, s0_t, prm, chunk, chunks_per_step, t_valid):
    bq, s, cols = p_rw.shape
    w = WIDTH_B
    rows = chunk * chunks_per_step
    row = lambda v: v.reshape(1, -1).astype(F32)
    params = [row(prm['rw_mu']), row(prm['rw_w0']), prm['rw_decay_up'], row(prm['rw_a0']), prm['rw_a_up'],
              prm['rw_g_up'], row(prm['rw_k_k']), row(prm['rw_k_a']), row(prm['rw_r_k'])]
    st = pl.BlockSpec((1, N_HEADS_B, HEAD_DIM_B, HEAD_DIM_B), lambda bi, c: (bi, 0, 0, 0))
    const = lambda bi, c: (0, 0)
    return pl.pallas_call(
        functools.partial(_rwkv_chunk_kernel, chunk, t_valid),
        grid=(bq, s // rows),
        in_specs=[pl.BlockSpec((1, rows, cols), lambda bi, c: (bi, c, 0)),
                  pl.BlockSpec((1, 1, cols), lambda bi, c: (bi, 0, 0))]
        + [pl.BlockSpec(p.shape, const) for p in params]
        + [st, pl.BlockSpec((1, w), const), pl.BlockSpec((1, w), const), pl.BlockSpec((w, w), const)],
        out_specs=[pl.BlockSpec((1, rows, w), lambda bi, c: (bi, c, 0)), st],
        out_shape=[jax.ShapeDtypeStruct((bq, s, w), BF16),
                   jax.ShapeDtypeStruct((bq, N_HEADS_B, HEAD_DIM_B, HEAD_DIM_B), F32)],
        scratch_shapes=[pltpu.VMEM((N_HEADS_B, HEAD_DIM_B, HEAD_DIM_B), F32), pltpu.VMEM((1, cols), F32)],
        compiler_params=_cparams(("parallel", "arbitrary")),
        name="rwkv_mix",
    )(p_rw, shift0, *params, s0_t, row(prm['ln_x_w']), row(prm['ln_x_b']), _seg_ones(w, HEAD_DIM_B))


def _mem_kv_kernel(mem_ref, g_ref, w_ref, kg_ref, mk_ref, mv_ref):
    h = _rms(mem_ref[0], g_ref[...]).astype(BF16)
    kv = _dg(h, w_ref[...], 1, 0)
    for hm in range(N_HEADS_M):
        sl = slice(hm * HEAD_DIM_M, (hm + 1) * HEAD_DIM_M)
        mk_ref[0, :, sl] = _rms(kv[:, sl], kg_ref[...])
    mv_ref[0] = kv[:, WIDTH_M:]


def _memory_kv(mem, norm_mem, w_mem_kv, k_norm_m):
    b, m, d = mem.shape
    const = lambda i: (0, 0)
    out = pl.BlockSpec((1, m, WIDTH_M), lambda i: (i, 0, 0))
    return pl.pallas_call(
        _mem_kv_kernel,
        grid=(b,),
        in_specs=[pl.BlockSpec((1, m, d), lambda i: (i, 0, 0)), pl.BlockSpec((1, d), const),
                  pl.BlockSpec((d, 2 * WIDTH_M), const), pl.BlockSpec((1, HEAD_DIM_M), const)],
        out_specs=[out, out],
        out_shape=[jax.ShapeDtypeStruct((b, m, WIDTH_M), F32)] * 2,
        compiler_params=_cparams(("parallel",)),
        name="memory_kv",
    )(mem, norm_mem.reshape(1, d), w_mem_kv.astype(BF16), k_norm_m.reshape(1, HEAD_DIM_M))


def _mem_attend_kernel(q_ref, mk_ref, mv_ref, g_ref, o_ref):
    q = q_ref[0]
    for hm in range(N_HEADS_M):
        sl = slice(hm * HEAD_DIM_M, (hm + 1) * HEAD_DIM_M)
        qh = (_rms(q[:, sl], g_ref[...]) * (HEAD_DIM_M ** -0.5)).astype(BF16)
        s = _dg(qh, mk_ref[0, :, sl].astype(BF16), 1, 1)
        p = jnp.exp(s - jnp.max(s, axis=1, keepdims=True))
        o = _dg(p.astype(BF16), mv_ref[0, :, sl].astype(BF16), 1, 0) / jnp.sum(p, axis=1, keepdims=True)
        o_ref[0, :, sl] = o.astype(o_ref.dtype)


def _memory_attend(q_m, mk, mv, q_norm_m, tq):
    b, s, w = q_m.shape
    m = mk.shape[1]
    kv = pl.BlockSpec((1, m, w), lambda bi, j: (bi, 0, 0))
    return pl.pallas_call(
        _mem_attend_kernel,
        grid=(b, s // tq),
        in_specs=[pl.BlockSpec((1, tq, w), lambda bi, j: (bi, j, 0)), kv, kv,
                  pl.BlockSpec((1, HEAD_DIM_M), lambda bi, j: (0, 0))],
        out_specs=pl.BlockSpec((1, tq, w), lambda bi, j: (bi, j, 0)),
        out_shape=jax.ShapeDtypeStruct((b, s, w), BF16),
        compiler_params=_cparams(("parallel", "parallel")),
        name="memory_attend",
    )(q_m, mk, mv, q_norm_m.reshape(1, HEAD_DIM_M))


def _merge_kernel(x_ref, oa_ref, ob_ref, om_ref, gl_ref, wb_ref, wo_ref, ng_ref, rw_ref, rb_ref,
                  x1_ref, h_ref, e_ref, gate_ref, rank_ref, cnt_ref, base_sc):
    d = x_ref.shape[1]
    gl = gl_ref[...]
    merged = jnp.zeros(x_ref.shape, F32)
    for n, o_ref in enumerate((oa_ref, ob_ref, om_ref)):
        y = _dg(o_ref[...], wb_ref[n], 1, 0)
        merged = merged + jax.nn.sigmoid(gl[:, n * d:(n + 1) * d].astype(F32)) * y
    x1 = x_ref[...] + _dg(merged.astype(BF16), wo_ref[...], 1, 0)
    x1_ref[...] = x1
    hn = _rms(x1, ng_ref[...])
    h_ref[...] = hn
    logits = _dot3(hn, rw_ref[...]) + rb_ref[...]
    tm, ne = logits.shape
    e_idx = lax.broadcasted_iota(I32, (tm, ne), 1)
    vals, idxs = [], []
    g = logits
    for _ in range(TOP_K):
        mx = jnp.max(g, axis=1, keepdims=True)
        first = jnp.min(jnp.where(g == mx, e_idx, ne), axis=1, keepdims=True)
        vals.append(mx)
        idxs.append(first)
        g = jnp.where(e_idx == first, -jnp.inf, g)
    top = jnp.concatenate(vals, axis=1)
    pe = jnp.exp(top - vals[0])
    gate_ref[...] = pe / jnp.sum(pe, axis=1, keepdims=True)
    e_ref[...] = jnp.concatenate(idxs, axis=1)

    @pl.when(pl.program_id(0) == 0)
    def _():
        base_sc[...] = jnp.zeros_like(base_sc)

    ohs = [jnp.where(e_idx == idx, 1.0, 0.0) for idx in idxs]
    cnt = ohs[0] + ohs[1] + ohs[2] + ohs[3]
    ri = lax.broadcasted_iota(I32, (tm, tm), 0)
    ci = lax.broadcasted_iota(I32, (tm, tm), 1)
    tri = jnp.where(ri > ci, 1.0, 0.0).astype(BF16)
    tot = _dg(tri, cnt.astype(BF16), 1, 0) + base_sc[...]
    rank_ref[...] = jnp.concatenate([jnp.sum(oh * tot, axis=1, keepdims=True) for oh in ohs],
                                    axis=1).astype(I32)
    base_sc[...] = base_sc[...] + jnp.sum(cnt, axis=0, keepdims=True)
    cnt_ref[...] = base_sc[...].astype(I32)


def _merge_and_route(x2, o_a, o_b, o_m, gl, prm, tm):
    n, d = x2.shape
    const2 = lambda i: (0, 0)
    tok = lambda wd: pl.BlockSpec((tm, wd), lambda i: (i, 0))
    return pl.pallas_call(
        _merge_kernel,
        grid=(n // tm,),
        in_specs=[tok(d), tok(512), tok(512), tok(512), tok(3 * d),
                  pl.BlockSpec((3, 512, d), lambda i: (0, 0, 0)), pl.BlockSpec((d, d), const2),
                  pl.BlockSpec((1, d), const2), pl.BlockSpec((d, N_EXPERTS), const2),
                  pl.BlockSpec((1, N_EXPERTS), const2)],
        out_specs=[tok(d), tok(d), tok(TOP_K), tok(TOP_K), tok(TOP_K), pl.BlockSpec((1, N_EXPERTS), const2)],
        out_shape=[jax.ShapeDtypeStruct((n, d), F32), jax.ShapeDtypeStruct((n, d), F32),
                   jax.ShapeDtypeStruct((n, TOP_K), I32), jax.ShapeDtypeStruct((n, TOP_K), F32),
                   jax.ShapeDtypeStruct((n, TOP_K), I32), jax.ShapeDtypeStruct((1, N_EXPERTS), I32)],
        scratch_shapes=[pltpu.VMEM((1, N_EXPERTS), F32)],
        compiler_params=_cparams(("arbitrary",)),
        name="merge_route",
    )(x2, o_a, o_b, o_m, gl, prm['w_branch'].astype(BF16), prm['w_out'].astype(BF16),
      prm['norm_ffn'].reshape(1, d), prm['router_w'], prm['router_b'].reshape(1, N_EXPERTS))


def _onehots(e):
    tm = e.shape[0]
    e_idx = lax.broadcasted_iota(I32, (tm, N_EXPERTS), 1)
    return [jnp.where(e[:, k:k + 1] == e_idx, 1.0, 0.0) for k in range(TOP_K)]


def _moe_rows_kernel(e_ref, rank_ref, start_ref, row_ref):
    ohs = _onehots(e_ref[...])
    st = start_ref[...].astype(F32)
    base = jnp.concatenate([jnp.sum(oh * st, axis=1, keepdims=True) for oh in ohs], axis=1)
    row_ref[...] = rank_ref[...] + base.astype(I32)


def _moe_rows(top_e, rank, starts, tm):
    n = top_e.shape[0]
    tok = pl.BlockSpec((tm, TOP_K), lambda i: (i, 0))
    return pl.pallas_call(
        _moe_rows_kernel,
        grid=(n // tm,),
        in_specs=[tok, tok, pl.BlockSpec((1, N_EXPERTS), lambda i: (0, 0))],
        out_specs=tok,
        out_shape=jax.ShapeDtypeStruct((n, TOP_K), I32),
        compiler_params=_cparams(("parallel",)),
        name="moe_rows",
    )(top_e, rank, starts)


def _dispatch_kernel(row_ref, h_ref, xs_in_ref, xs_ref, sem):
    del xs_in_ref
    tm = h_ref.shape[0]

    def start(t, carry):
        for k in range(TOP_K):
            r = row_ref[t * TOP_K + k]
            pltpu.make_async_copy(h_ref.at[pl.ds(t, 1)], xs_ref.at[pl.ds(r, 1)], sem).start()
        return carry

    lax.fori_loop(0, tm, start, 0, unroll=2)
    all_rows = xs_ref.at[pl.ds(0, tm * TOP_K)]
    pltpu.make_async_copy(all_rows, all_rows, sem).wait()


def _zero_rows_kernel(o_ref):
    o_ref[...] = jnp.zeros_like(o_ref)


def _zero_rows(n_rows, d):
    tile = next(t for t in (1024, 512, 256, MOE_ROWS_SMALL) if n_rows % t == 0)
    return pl.pallas_call(
        _zero_rows_kernel,
        grid=(n_rows // tile,),
        out_specs=pl.BlockSpec((tile, d), lambda i: (i, 0)),
        out_shape=jax.ShapeDtypeStruct((n_rows, d), F32),
        compiler_params=_cparams(("parallel",)),
        name="zero_rows",
    )()


def _moe_dispatch(h, row_flat, n_rows, tm):
    n, d = h.shape
    xs0 = _zero_rows(n_rows, d)
    return pl.pallas_call(
        _dispatch_kernel,
        grid=(n // tm,),
        in_specs=[pl.BlockSpec((tm * TOP_K,), lambda i: (i,), memory_space=pltpu.SMEM),
                  pl.BlockSpec((tm, d), lambda i: (i, 0)),
                  pl.BlockSpec(memory_space=pl.ANY)],
        out_specs=pl.BlockSpec(memory_space=pl.ANY),
        out_shape=jax.ShapeDtypeStruct((n_rows, d), F32),
        scratch_shapes=[pltpu.SemaphoreType.DMA(())],
        input_output_aliases={2: 0},
        compiler_params=_cparams(("arbitrary",)),
        name="moe_dispatch",
    )(row_flat, h, xs0)


def _swiglu(u, d_ff):
    u_glu = jnp.minimum(u[:, :d_ff], SWIGLU_LIMIT)
    u_lin = jnp.clip(u[:, d_ff:], -SWIGLU_LIMIT, SWIGLU_LIMIT)
    return u_glu * jax.nn.sigmoid(SWIGLU_ALPHA * u_glu) * (u_lin + 1.0)


def _moe_ffn_kernel(be_ref, nu_ref, xs_ref, w1_ref, b1_ref, w2_ref, b2_ref, y_ref, w1_sc, w2_sc):
    i = pl.program_id(0)
    prev = be_ref[jnp.maximum(i - 1, 0)]
    first = (i == 0) | (be_ref[i] != prev)

    @pl.when(first)
    def _():
        w1_sc[...] = w1_ref[...].astype(BF16)
        w2_sc[...] = w2_ref[...].astype(BF16)

    @pl.when(i < nu_ref[0])
    def _():
        u = _dg(xs_ref[...].astype(BF16), w1_sc[...], 1, 0) + b1_ref[...]
        act = _swiglu(u, w2_ref.shape[0])
        y_ref[...] = _dg(act.astype(BF16), w2_sc[...], 1, 0) + b2_ref[...]

    @pl.when(i >= nu_ref[0])
    def _():
        y_ref[...] = jnp.zeros_like(y_ref)


def _moe_ffn(xs, blk_e, n_used, w1, b1, w2, b2, rows):
    n_rows, d = xs.shape
    ne, _, f2 = w1.shape
    d_ff = w2.shape[1]
    nblk = n_rows // rows
    return pl.pallas_call(
        _moe_ffn_kernel,
        grid_spec=pltpu.PrefetchScalarGridSpec(
            num_scalar_prefetch=2,
            grid=(nblk,),
            in_specs=[pl.BlockSpec((rows, d), lambda i, be, nu: (i, 0)),
                      pl.BlockSpec((None, d, f2), lambda i, be, nu: (be[i], 0, 0)),
                      pl.BlockSpec((None, 1, f2), lambda i, be, nu: (be[i], 0, 0)),
                      pl.BlockSpec((None, d_ff, d), lambda i, be, nu: (be[i], 0, 0)),
                      pl.BlockSpec((None, 1, d), lambda i, be, nu: (be[i], 0, 0))],
            out_specs=pl.BlockSpec((rows, d), lambda i, be, nu: (i, 0)),
            scratch_shapes=[pltpu.VMEM((d, f2), BF16), pltpu.VMEM((d_ff, d), BF16)],
        ),
        out_shape=jax.ShapeDtypeStruct((n_rows, d), F32),
        compiler_params=_cparams(("arbitrary",)),
        name="moe_ffn",
    )(blk_e, n_used, xs, w1, b1.reshape(ne, 1, f2), w2, b2.reshape(ne, 1, d))


def _combine_kernel(row_ref, yb_ref, x1_ref, gate_ref, y_ref, buf, sem):
    tm = x1_ref.shape[0]

    def start(t, carry):
        for k in range(TOP_K):
            r = row_ref[t * TOP_K + k]
            pltpu.make_async_copy(yb_ref.at[pl.ds(r, 1)], buf.at[pl.ds(k * tm + t, 1)], sem).start()
        return carry

    lax.fori_loop(0, tm, start, 0, unroll=2)
    pltpu.make_async_copy(yb_ref.at[pl.ds(0, tm * TOP_K)], buf, sem).wait()
    gates = gate_ref[...]
    acc = buf[0:tm] * gates[:, 0:1]
    for k in range(1, TOP_K):
        acc = acc + buf[k * tm:(k + 1) * tm] * gates[:, k:k + 1]
    y_ref[...] = x1_ref[...] + acc


def _moe_combine(yb, row_flat, x1, gates, tm):
    n, d = x1.shape
    return pl.pallas_call(
        _combine_kernel,
        grid=(n // tm,),
        in_specs=[pl.BlockSpec((tm * TOP_K,), lambda i: (i,), memory_space=pltpu.SMEM),
                  pl.BlockSpec(memory_space=pl.ANY),
                  pl.BlockSpec((tm, d), lambda i: (i, 0)),
                  pl.BlockSpec((tm, TOP_K), lambda i: (i, 0))],
        out_specs=pl.BlockSpec((tm, d), lambda i: (i, 0)),
        out_shape=jax.ShapeDtypeStruct((n, d), F32),
        scratch_shapes=[pltpu.VMEM((TOP_K * tm, d), F32), pltpu.SemaphoreType.DMA(())],
        compiler_params=_cparams(("arbitrary",)),
        name="moe_combine",
    )(row_flat, yb, x1, gates)


def _moe_block(x1, hn, top_e, gates, rank, counts, prm, tm):
    n, d = x1.shape
    rows = MOE_ROWS if n * TOP_K >= N_EXPERTS * MOE_ROWS else MOE_ROWS_SMALL
    counts = counts[0]
    padded = (counts + rows - 1) // rows * rows
    p_end = jnp.cumsum(padded)
    starts = (p_end - padded).astype(I32)
    nblk = -(-(n * TOP_K + N_EXPERTS * (rows - 1)) // rows)
    n_used = (p_end[-1] // rows).astype(I32)
    blk_i = jnp.minimum(jnp.arange(nblk, dtype=I32), n_used - 1)
    n_before = jnp.sum((p_end[None, :] <= (blk_i * rows)[:, None]).astype(I32), axis=1)
    blk_e = jnp.minimum(n_before, N_EXPERTS - 1).astype(I32)
    row = _moe_rows(top_e, rank, starts[None, :], tm)
    row_flat = row.reshape(-1)
    xs = _moe_dispatch(hn, row_flat, nblk * rows, tm)
    yb = _moe_ffn(xs, blk_e, n_used[None], prm['moe_w1'], prm['moe_b1'], prm['moe_w2'], prm['moe_b2'], rows)
    return _moe_combine(yb, row_flat, x1, gates, tm)


def _split_w_in(w_in):
    d = w_in.shape[0]
    a = 3 * WIDTH_A
    b = a + RWKV_COLS
    c = b + WIDTH_M
    wb = w_in.astype(BF16)
    return [wb[:, :a], wb[:, a:b], wb[:, b:c], wb[:, c:]]


def _token_mix_tail(x2, o_a, o_b, o_m, gl, prm, tm):
    x1, hn, top_e, gates, rank, counts = _merge_and_route(x2, o_a, o_b, o_m, gl, prm, tm)
    return _moe_block(x1, hn, top_e, gates, rank, counts, prm, tm)


def _layer_prompt(x, mem, prm, ws):
    b, s, d = x.shape
    tm = 256
    x2 = x.reshape(b * s, d)
    pos = jnp.arange(s, dtype=I32)
    p_rw, q_m, gl, q_s, k_t, v_t, kaug, vaug, kmean = _projection_and_moba_prep(
        x2, prm['norm_mix'].reshape(1, d), ws, b, pos, prm['q_norm_a'], prm['k_norm_a'])
    k_o, v_o = jnp.swapaxes(k_t, 2, 3), jnp.swapaxes(v_t, 2, 3)
    o_a = _moba_flash(q_s, kaug, vaug, kmean).reshape(b * s, WIDTH_A)

    p3 = p_rw.reshape(b, s, RWKV_COLS)
    s0_t = jnp.zeros((b, N_HEADS_B, HEAD_DIM_B, HEAD_DIM_B), F32)
    o_b, st = _rwkv_mix(p3, jnp.zeros((b, 1, RWKV_COLS), F32), s0_t, prm, RWKV_CHUNK, RWKV_CHUNKS_PER_STEP,
                        RWKV_CHUNK * RWKV_CHUNKS_PER_STEP)
    wkv = jnp.swapaxes(st, 2, 3)

    mk, mv = _memory_kv(mem, prm['norm_mem'], prm['w_mem_kv'], prm['k_norm_m'])
    o_m = _memory_attend(q_m.reshape(b, s, WIDTH_M), mk, mv, prm['q_norm_m'], 512)

    y = _token_mix_tail(x2, o_a, o_b.reshape(b * s, WIDTH_B), o_m.reshape(b * s, WIDTH_M), gl, prm, tm)
    m = mem.shape[1]
    return (y.reshape(b, s, d), k_o, v_o, wkv, p3[:, -1],
            mk.reshape(b, m, N_HEADS_M, HEAD_DIM_M), mv.reshape(b, m, N_HEADS_M, HEAD_DIM_M))


def _layer_sample(x, cache_k, cache_v, mem_k, mem_v, wkv0, shift0, page_table, layer, prm, ws):
    db, t, d = x.shape
    tp = SAMPLE_T_PAD
    past_len = page_table.shape[1] * PAGE_SIZE
    assert past_len % MOBA_BLOCK == 0 and t <= tp
    ppb = MOBA_BLOCK // PAGE_SIZE
    n = db * tp
    x2 = jnp.pad(x, ((0, 0), (0, tp - t), (0, 0))).reshape(n, d)
    qkv, p_rw, q_m, gl = _in_projection(x2, prm['norm_mix'].reshape(1, d), ws, n)
    pos = past_len + jnp.arange(tp, dtype=I32)
    q_s, k_o, v_o = _moba_prep(qkv.reshape(db, tp, -1), pos, prm['q_norm_a'], prm['k_norm_a'], tp)

    cache_kt, cache_vt = jnp.swapaxes(cache_k, 3, 4), jnp.swapaxes(cache_v, 3, 4)
    kmean_t = _page_means(cache_kt, page_table, layer)
    idx = _sample_select(q_s, kmean_t)
    idx = jnp.transpose(idx[:, :, :t], (0, 2, 1, 3))
    logical = idx[..., None] * ppb + jnp.arange(ppb, dtype=I32)
    phys = page_table[jnp.arange(db)[:, None, None, None, None], logical].reshape(-1).astype(I32)
    o_a = _sample_attend(q_s, k_o, v_o, cache_kt, cache_vt, phys, t, layer)
    o_a = jnp.pad(o_a.reshape(db, t, WIDTH_A), ((0, 0), (0, tp - t), (0, 0))).reshape(n, WIDTH_A).astype(BF16)

    p3 = p_rw.reshape(db, tp, RWKV_COLS)
    o_b, st = _rwkv_mix(p3, shift0[:, None, :], jnp.swapaxes(wkv0, 2, 3), prm, tp, 1, t)
    wkv = jnp.swapaxes(st, 2, 3)

    m = mem_k.shape[1]
    o_m = _memory_attend(q_m.reshape(db, tp, WIDTH_M), mem_k.reshape(db, m, WIDTH_M),
                         mem_v.reshape(db, m, WIDTH_M), prm['q_norm_m'], tp)

    y = _token_mix_tail(x2, o_a, o_b.reshape(n, WIDTH_B), o_m.reshape(n, WIDTH_M), gl, prm, n)
    return (y.reshape(db, tp, d)[:, :t], k_o[:, :, :t], v_o[:, :, :t], wkv, p3[:, t - 1])


def kernel(x_prompt, x_sample, mem_prompt, cache_k, cache_v, cache_mem_k, cache_mem_v, state_wkv, state_shift,
           page_table, norm_mix, norm_mem, norm_ffn, w_in, q_norm_a, k_norm_a, q_norm_m, k_norm_m, w_mem_kv,
           rw_mu, rw_w0, rw_decay_up, rw_a0, rw_a_up, rw_g_up, rw_k_k, rw_k_a, rw_r_k, ln_x_w, ln_x_b,
           w_branch, w_out, router_w, router_b, moe_w1, moe_b1, moe_w2, moe_b2):
    depth = w_in.shape[0]
    xp, xs = x_prompt, x_sample
    outs_p = [[] for _ in range(6)]
    outs_s = [[] for _ in range(4)]
    for l in range(depth):
        prm = dict(norm_mix=norm_mix[l], norm_mem=norm_mem[l], norm_ffn=norm_ffn[l], w_in=w_in[l],
                   q_norm_a=q_norm_a[l], k_norm_a=k_norm_a[l], q_norm_m=q_norm_m[l], k_norm_m=k_norm_m[l],
                   w_mem_kv=w_mem_kv[l], rw_mu=rw_mu[l], rw_w0=rw_w0[l], rw_decay_up=rw_decay_up[l],
                   rw_a0=rw_a0[l], rw_a_up=rw_a_up[l], rw_g_up=rw_g_up[l], rw_k_k=rw_k_k[l], rw_k_a=rw_k_a[l],
                   rw_r_k=rw_r_k[l].reshape(-1), ln_x_w=ln_x_w[l], ln_x_b=ln_x_b[l], w_branch=w_branch[l],
                   w_out=w_out[l], router_w=router_w[l], router_b=router_b[l], moe_w1=moe_w1[l],
                   moe_b1=moe_b1[l], moe_w2=moe_w2[l], moe_b2=moe_b2[l])
        ws = _split_w_in(prm['w_in'])
        xp, *rest_p = _layer_prompt(xp, mem_prompt, prm, ws)
        for acc, val in zip(outs_p, rest_p):
            acc.append(val)
        xs, *rest_s = _layer_sample(xs, cache_k, cache_v, cache_mem_k[l], cache_mem_v[l], state_wkv[l],
                                    state_shift[l], page_table, l, prm, ws)
        for acc, val in zip(outs_s, rest_s):
            acc.append(val)
    kp, vp, wkvp, shp, mkp, mvp = [jnp.stack(a) for a in outs_p]
    ksm, vsm, wkvs, shs = [jnp.stack(a) for a in outs_s]
    return (xp, xs, kp, vp, wkvp, shp, mkp, mvp, ksm, vsm, wkvs, shs)
```

```python
import functools
import math

import jax
import jax.numpy as jnp
from jax import lax
from jax.experimental import pallas as pl
from jax.experimental.pallas import tpu as pltpu

F32 = jnp.float32
BF16 = jnp.bfloat16
I32 = jnp.int32

N_HEADS_A = 8
HEAD_DIM_A = 64
WIDTH_A = 512
MOBA_BLOCK = 256
MOBA_TOPK = 3
ROT_DIM = 16
ROPE_THETA = 500000.0
PAGE_SIZE = 128
N_HEADS_B = 8
HEAD_DIM_B = 64
WIDTH_B = 512
DECAY_LORA = 64
AAA_LORA = 64
GATE_LORA = 128
RWKV_COLS = 1792
DECAY_SCALE = math.exp(-0.5)
LN_X_EPS = 64e-5
N_HEADS_M = 4
HEAD_DIM_M = 128
WIDTH_M = 512
N_EXPERTS = 32
TOP_K = 4
SWIGLU_ALPHA = 1.702
SWIGLU_LIMIT = 7.0
NORM_EPS = 1e-6

SUBLANES = 8
NEG_BIG = -1e30
SAMPLE_T_PAD = 8
RWKV_CHUNK = 64
RWKV_CHUNKS_PER_STEP = 4
MOE_ROWS = 512
MOE_ROWS_SMALL = 128
VMEM_LIMIT = 56 * 1024 * 1024


def _cparams(sem, vmem=None):
    return pltpu.CompilerParams(dimension_semantics=sem, vmem_limit_bytes=vmem or VMEM_LIMIT)


def _dg(a, b, ca, cb):
    return lax.dot_general(a, b, (((ca,), (cb,)), ((), ())), preferred_element_type=F32)


def _split(x):
    hi = x.astype(BF16)
    lo = (x - hi.astype(F32)).astype(BF16)
    return hi, lo


def _dot3(a, b, ca=1, cb=0):
    ah, al = _split(a)
    bh, bl = _split(b)
    return _dg(ah, bh, ca, cb) + (_dg(ah, bl, ca, cb) + _dg(al, bh, ca, cb))


def _dot2_exact_rhs(a, b_bf16):
    ah, al = _split(a)
    return _dg(ah, b_bf16, 1, 0) + _dg(al, b_bf16, 1, 0)


def _dotb(a, b, ca=1, cb=0):
    return _dg(a.astype(BF16), b.astype(BF16), ca, cb)


def _rms(x, gain_row):
    ms = jnp.mean(x * x, axis=-1, keepdims=True)
    return x * lax.rsqrt(ms + NORM_EPS) * gain_row


def _seg_ones(width, seg):
    r = lax.broadcasted_iota(I32, (width, width), 0) // seg
    c = lax.broadcasted_iota(I32, (width, width), 1) // seg
    return jnp.where(r == c, 1.0, 0.0).astype(BF16)


def _proj_kernel(x_ref, g_ref, w1, w2, w3, w4, o1, o2, o3, o4):
    h = _rms(x_ref[...], g_ref[...]).astype(BF16)
    o1[...] = _dg(h, w1[...], 1, 0)
    o2[...] = _dg(h, w2[...], 1, 0)
    o3[...] = _dg(h, w3[...], 1, 0)
    o4[...] = _dg(h, w4[...], 1, 0).astype(o4.dtype)


def _in_projection(x2, gain, ws, tm):
    n, d = x2.shape
    widths = [w.shape[1] for w in ws]
    dtypes = [F32, F32, F32, BF16]
    const = lambda i: (0, 0)
    return pl.pallas_call(
        _proj_kernel,
        grid=(n // tm,),
        in_specs=[pl.BlockSpec((tm, d), lambda i: (i, 0)), pl.BlockSpec((1, d), const)]
        + [pl.BlockSpec((d, wd), const) for wd in widths],
        out_specs=[pl.BlockSpec((tm, wd), lambda i: (i, 0)) for wd in widths],
        out_shape=[jax.ShapeDtypeStruct((n, wd), dt) for wd, dt in zip(widths, dtypes)],
        compiler_params=_cparams(("parallel",)),
        name="in_projection",
    )(x2, gain, *ws)


def _rope_tables(pos):
    half = ROT_DIM // 2
    inv_freq = 1.0 / (ROPE_THETA ** (jnp.arange(0, ROT_DIM, 2, dtype=F32) / ROT_DIM))
    lane = jnp.arange(128, dtype=I32) % HEAD_DIM_A
    ang = pos.astype(F32)[:, None] * inv_freq[lane % half][None, :]
    cos, sin = jnp.cos(ang), jnp.sin(ang)
    c = jnp.where(lane < ROT_DIM, cos, 1.0)
    s_up = jnp.where(lane < half, -sin, 0.0)
    s_dn = jnp.where((lane >= half) & (lane < ROT_DIM), sin, 0.0)
    return c, s_up, s_dn


def _norm_rope(x, seg, gain, c, s_up, s_dn):
    ss = _dot2_exact_rhs(x * x, seg)
    y = x * lax.rsqrt(ss * (1.0 / HEAD_DIM_A) + NORM_EPS) * gain
    half = ROT_DIM // 2
    up = pltpu.roll(y, WIDTH_A - half, 1)
    dn = pltpu.roll(y, half, 1)
    return y * c + up * s_up + dn * s_dn


def _moba_prep_kernel(qkv_ref, seg_ref, qg_ref, kg_ref, c_ref, su_ref, sd_ref, *outs):
    _prep_body(False, qkv_ref[0], None, seg_ref, qg_ref, kg_ref, c_ref, su_ref, sd_ref, outs)


def _proj_prep_kernel(nb, x_ref, g_ref, w1, w2, w3, w4, seg_ref, qg_ref, kg_ref, c_ref, su_ref, sd_ref,
                      o_rw, o_qm, o_gl, *outs):
    h = _rms(x_ref[...], g_ref[...]).astype(BF16)
    _prep_body(True, _dg(h, w1[...], 1, 0), pl.program_id(0) % nb, seg_ref, qg_ref, kg_ref,
               c_ref, su_ref, sd_ref, outs)
    o_rw[...] = _dg(h, w2[...], 1, 0)
    o_qm[...] = _dg(h, w3[...], 1, 0)
    o_gl[...] = _dg(h, w4[...], 1, 0).astype(o_gl.dtype)


def _prep_body(with_blocks, x, blk, seg_ref, qg_ref, kg_ref, c_ref, su_ref, sd_ref, outs):
    if with_blocks:
        qs_ref, kt_ref, vt_ref, kaug_ref, vaug_ref, kmean_ref = outs
    else:
        qs_ref, k_ref, v_ref = outs
    tm = x.shape[0]
    rep = lambda r: jnp.concatenate([r[...]] * (WIDTH_A // 128), axis=1)
    c, su, sd = rep(c_ref), rep(su_ref), rep(sd_ref)
    seg = seg_ref[...]
    q = _norm_rope(x[:, :WIDTH_A], seg, qg_ref[...], c, su, sd) * (HEAD_DIM_A ** -0.5)
    k = _norm_rope(x[:, WIDTH_A:2 * WIDTH_A], seg, kg_ref[...], c, su, sd)
    v = x[:, 2 * WIDTH_A:]
    if with_blocks:
        lane = lax.broadcasted_iota(I32, (tm, HEAD_DIM_A), 1)
        onehot = jnp.where(lane == blk, 1.0, 0.0).astype(BF16)
        ones_col = jnp.where(lane == 0, 1.0, 0.0).astype(BF16)

        @pl.when(blk == 0)
        def _():
            kmean_ref[...] = jnp.zeros_like(kmean_ref)

    for h in range(N_HEADS_A):
        sl = slice(h * HEAD_DIM_A, (h + 1) * HEAD_DIM_A)
        qs_ref[0, h] = q[:, sl]
        if with_blocks:
            kt_ref[0, h] = k[:, sl].T
            vt_ref[0, h] = v[:, sl].T
            kaug_ref[0, h] = jnp.concatenate([k[:, sl].astype(BF16), onehot], axis=1)
            vaug_ref[0, h] = jnp.concatenate([v[:, sl].astype(BF16), ones_col], axis=1)
            kmean_ref[0, h, pl.ds(blk, 1), :] = jnp.mean(k[:, sl], axis=0, keepdims=True)
        else:
            k_ref[0, h] = k[:, sl]
            v_ref[0, h] = v[:, sl]


def _moba_prep(qkv, pos, q_gain, k_gain, tm):
    bq, s, _ = qkv.shape
    c, su, sd = _rope_tables(pos)
    seg = _seg_ones(WIDTH_A, HEAD_DIM_A)
    tile8 = lambda g: jnp.tile(g.astype(F32), N_HEADS_A)[None, :]
    hm = jax.ShapeDtypeStruct((bq, N_HEADS_A, s, HEAD_DIM_A), F32)
    hm_spec = pl.BlockSpec((1, N_HEADS_A, tm, HEAD_DIM_A), lambda b, j: (b, 0, j, 0))
    const = lambda b, j: (0, 0)
    tab = pl.BlockSpec((tm, 128), lambda b, j: (j, 0))
    return pl.pallas_call(
        _moba_prep_kernel,
        grid=(bq, s // tm),
        in_specs=[pl.BlockSpec((1, tm, 3 * WIDTH_A), lambda b, j: (b, j, 0)),
                  pl.BlockSpec((WIDTH_A, WIDTH_A), const),
                  pl.BlockSpec((1, WIDTH_A), const), pl.BlockSpec((1, WIDTH_A), const), tab, tab, tab],
        out_specs=[hm_spec, hm_spec, hm_spec],
        out_shape=[hm, hm, hm],
        compiler_params=_cparams(("parallel", "parallel")),
        name="moba_prep",
    )(qkv, seg, tile8(q_gain), tile8(k_gain), c, su, sd)


def _projection_and_moba_prep(x2, gain, ws, bq, pos, q_gain, k_gain):
    n, d = x2.shape
    tm = MOBA_BLOCK
    s = n // bq
    nb = s // tm
    assert nb <= HEAD_DIM_A
    c, su, sd = _rope_tables(pos)
    seg = _seg_ones(WIDTH_A, HEAD_DIM_A)
    tile8 = lambda g: jnp.tile(g.astype(F32), N_HEADS_A)[None, :]
    widths = [w.shape[1] for w in ws]
    const = lambda i: (0, 0)
    tok = lambda wd: pl.BlockSpec((tm, wd), lambda i: (i, 0))
    hm_spec = pl.BlockSpec((1, N_HEADS_A, tm, HEAD_DIM_A), lambda i: (i // nb, 0, i % nb, 0))
    tr_spec = pl.BlockSpec((1, N_HEADS_A, HEAD_DIM_A, tm), lambda i: (i // nb, 0, 0, i % nb))
    aug_spec = pl.BlockSpec((1, N_HEADS_A, tm, 128), lambda i: (i // nb, 0, i % nb, 0))
    km_spec = pl.BlockSpec((1, N_HEADS_A, HEAD_DIM_A, HEAD_DIM_A), lambda i: (i // nb, 0, 0, 0))
    tab = pl.BlockSpec((tm, 128), lambda i: (i % nb, 0))
    hm = jax.ShapeDtypeStruct((bq, N_HEADS_A, s, HEAD_DIM_A), F32)
    tr = jax.ShapeDtypeStruct((bq, N_HEADS_A, HEAD_DIM_A, s), F32)
    aug = jax.ShapeDtypeStruct((bq, N_HEADS_A, s, 128), BF16)
    return pl.pallas_call(
        functools.partial(_proj_prep_kernel, nb),
        grid=(n // tm,),
        in_specs=[tok(d), pl.BlockSpec((1, d), const)] + [pl.BlockSpec((d, wd), const) for wd in widths]
        + [pl.BlockSpec((WIDTH_A, WIDTH_A), const), pl.BlockSpec((1, WIDTH_A), const),
           pl.BlockSpec((1, WIDTH_A), const), tab, tab, tab],
        out_specs=[tok(widths[1]), tok(widths[2]), tok(widths[3]),
                   hm_spec, tr_spec, tr_spec, aug_spec, aug_spec, km_spec],
        out_shape=[jax.ShapeDtypeStruct((n, widths[1]), F32), jax.ShapeDtypeStruct((n, widths[2]), F32),
                   jax.ShapeDtypeStruct((n, widths[3]), BF16), hm, tr, tr, aug, aug,
                   jax.ShapeDtypeStruct((bq, N_HEADS_A, HEAD_DIM_A, HEAD_DIM_A), F32)],
        compiler_params=_cparams(("arbitrary",)),
        name="projection_moba_prep",
    )(x2, gain, *ws, seg, tile8(q_gain), tile8(k_gain), c, su, sd)


FLASH_HEADS = 4


def _moba_flash_kernel(q_ref, kaug_ref, vaug_ref, kmean_ref, o_ref, m_sc, acc_sc):
    i = pl.program_id(1)
    hp = pl.program_id(2)
    tq, dh = q_ref.shape[2], q_ref.shape[3]
    gs = range(FLASH_HEADS)
    q = [q_ref[0, g] for g in gs]
    gate_t = [_dot3(kmean_ref[0, g], q[g], 1, 1) for g in gs]
    n_idx = lax.broadcasted_iota(I32, gate_t[0].shape, 0)
    n_tot = gate_t[0].shape[0]
    gv = [jnp.where(n_idx < i, gate_t[g], -jnp.inf) for g in gs]
    sel = [n_idx == i for g in gs]
    for _ in range(MOBA_TOPK):
        mx = [jnp.max(gv[g], axis=0, keepdims=True) for g in gs]
        cand = [(gv[g] == mx[g]) & (mx[g] > -jnp.inf) for g in gs]
        first = [jnp.min(jnp.where(cand[g], n_idx, n_tot), axis=0, keepdims=True) for g in gs]
        pick = [n_idx == first[g] for g in gs]
        sel = [sel[g] | pick[g] for g in gs]
        gv = [jnp.where(pick[g], -jnp.inf, gv[g]) for g in gs]
    zeros_t = jnp.zeros((dh, tq), F32)
    zeros_q = jnp.zeros((tq, dh), F32)
    bias = [jnp.concatenate([zeros_t, jnp.where(sel[g], 0.0, NEG_BIG)], axis=0).T for g in gs]
    qaug = [(jnp.concatenate([q[g], zeros_q], axis=1) + bias[g]).astype(BF16) for g in gs]

    start = pl.multiple_of(i * tq, tq)
    row = lax.broadcasted_iota(I32, (tq, tq), 0)
    col = lax.broadcasted_iota(I32, (tq, tq), 1)
    s = [jnp.where(col <= row, _dg(qaug[g], kaug_ref[0, g, pl.ds(start, tq), :], 1, 1), -jnp.inf) for g in gs]
    m0 = [jnp.max(s[g], axis=1, keepdims=True) for g in gs]
    for g in gs:
        m_sc[g] = jnp.broadcast_to(m0[g], (tq, 128))
        acc_sc[g] = _dg(jnp.exp(s[g] - m0[g]).astype(BF16), vaug_ref[0, g, pl.ds(start, tq), :], 1, 0)

    def step(off, width):
        sj = [_dg(qaug[g], kaug_ref[0, g, pl.ds(off, width), :], 1, 1) for g in gs]
        m_old = [m_sc[g] for g in gs]
        m_new = [jnp.maximum(m_old[g], jnp.max(sj[g], axis=1, keepdims=True)) for g in gs]
        pj = [jnp.exp(sj[g] - jnp.concatenate([m_new[g]] * (width // 128), axis=1)) for g in gs]
        for g in gs:
            acc_sc[g] = (jnp.exp(m_old[g] - m_new[g]) * acc_sc[g]
                         + _dg(pj[g].astype(BF16), vaug_ref[0, g, pl.ds(off, width), :], 1, 0))
            m_sc[g] = m_new[g]

    def quad(j, carry):
        step(pl.multiple_of(j * (4 * tq), 4 * tq), 4 * tq)
        return carry

    lax.fori_loop(0, i // 4, quad, 0)

    @pl.when(i % 4 >= 2)
    def _():
        step(pl.multiple_of((i // 4) * (4 * tq), 2 * tq), 2 * tq)

    @pl.when(i % 2 == 1)
    def _():
        step(pl.multiple_of((i - 1) * tq, tq), tq)

    outs = []
    for g in gs:
        acc = acc_sc[g]
        outs.append(acc[:, :dh] / acc[:, dh:dh + 1])
    out = jnp.concatenate(outs, axis=1).astype(o_ref.dtype)
    wd = FLASH_HEADS * dh
    for pp in range(N_HEADS_A // FLASH_HEADS):
        @pl.when(hp == pp)
        def _():
            o_ref[0, :, pp * wd:(pp + 1) * wd] = out


def _moba_flash(q_s, kaug, vaug, kmean):
    b, nh, s, dh = q_s.shape
    tq = MOBA_BLOCK
    nb = s // tq
    g = FLASH_HEADS
    return pl.pallas_call(
        _moba_flash_kernel,
        grid=(b, nb, nh // g),
        in_specs=[pl.BlockSpec((1, g, tq, dh), lambda bi, i, h: (bi, h, i, 0)),
                  pl.BlockSpec((1, g, s, 128), lambda bi, i, h: (bi, h, 0, 0)),
                  pl.BlockSpec((1, g, s, 128), lambda bi, i, h: (bi, h, 0, 0)),
                  pl.BlockSpec((1, g, dh, dh), lambda bi, i, h: (bi, h, 0, 0))],
        out_specs=pl.BlockSpec((1, tq, nh * dh), lambda bi, i, h: (bi, i, 0)),
        out_shape=jax.ShapeDtypeStruct((b, s, nh * dh), BF16),
        scratch_shapes=[pltpu.VMEM((g, tq, 128), F32), pltpu.VMEM((g, tq, 128), F32)],
        compiler_params=_cparams(("parallel", "parallel", "arbitrary")),
        name="moba_flash",
    )(q_s, kaug, vaug, kmean)


PAGES_PER_STEP = 32


def _page_mean_kernel(pt_ref, *refs):
    pages, out_ref = refs[:PAGES_PER_STEP], refs[PAGES_PER_STEP]
    s = pl.program_id(1)
    ppb = MOBA_BLOCK // PAGE_SIZE
    bps = PAGES_PER_STEP // ppb

    @pl.when(s == 0)
    def _():
        out_ref[...] = jnp.zeros_like(out_ref)

    lane = lax.broadcasted_iota(I32, out_ref.shape[2:], 1)
    for h in range(N_HEADS_A):
        acc = out_ref[0, h]
        for j in range(bps):
            tot = pages[ppb * j][h]
            for u in range(1, ppb):
                tot = tot + pages[ppb * j + u][h]
            col = jnp.sum(tot, axis=1, keepdims=True) * (1.0 / MOBA_BLOCK)
            acc = jnp.where(lane == s * bps + j, col, acc)
        out_ref[0, h] = acc


def _page_means(cache_kt, page_table, layer):
    db, n_pages = page_table.shape
    ppb = MOBA_BLOCK // PAGE_SIZE
    n_full = n_pages // ppb
    steps = n_full * ppb // PAGES_PER_STEP
    _, _, nh, dh, pg = cache_kt.shape

    def page_spec(u):
        return pl.BlockSpec((None, None, nh, dh, pg),
                            lambda b, s, pt: (layer, pt[b, s * PAGES_PER_STEP + u], 0, 0, 0))

    return pl.pallas_call(
        _page_mean_kernel,
        grid_spec=pltpu.PrefetchScalarGridSpec(
            num_scalar_prefetch=1,
            grid=(db, steps),
            in_specs=[page_spec(u) for u in range(PAGES_PER_STEP)],
            out_specs=pl.BlockSpec((1, nh, dh, n_full), lambda b, s, pt: (b, 0, 0, 0)),
        ),
        out_shape=jax.ShapeDtypeStruct((db, nh, dh, n_full), F32),
        compiler_params=_cparams(("parallel", "arbitrary")),
        name="page_means",
    )(page_table, *([cache_kt] * PAGES_PER_STEP))


def _sample_select_kernel(q_ref, km_ref, idx_ref):
    tp = q_ref.shape[2]
    nb = km_ref.shape[3]
    n_idx = lax.broadcasted_iota(I32, (tp, nb), 1)
    for h in range(N_HEADS_A):
        g = _dot3(q_ref[0, h], km_ref[0, h])
        cols = []
        for _ in range(MOBA_TOPK):
            mx = jnp.max(g, axis=1, keepdims=True)
            first = jnp.min(jnp.where(g == mx, n_idx, nb), axis=1, keepdims=True)
            cols.append(first)
            g = jnp.where(n_idx == first, -jnp.inf, g)
        idx_ref[0, h] = jnp.concatenate(cols, axis=1)


def _sample_select(q_s, kmean_t):
    db, nh, tp, dh = q_s.shape
    nb = kmean_t.shape[3]
    return pl.pallas_call(
        _sample_select_kernel,
        grid=(db,),
        in_specs=[pl.BlockSpec((1, nh, tp, dh), lambda b: (b, 0, 0, 0)),
                  pl.BlockSpec((1, nh, dh, nb), lambda b: (b, 0, 0, 0))],
        out_specs=pl.BlockSpec((1, nh, tp, MOBA_TOPK), lambda b: (b, 0, 0, 0)),
        out_shape=jax.ShapeDtypeStruct((db, nh, tp, MOBA_TOPK), I32),
        compiler_params=_cparams(("parallel",)),
        name="sample_select",
    )(q_s, kmean_t)


def _sample_attend_kernel(t_valid, layer, phys_ref, q_ref, kn_ref, vn_ref, ck_ref, cv_ref, o_ref,
                          kbuf, vbuf, sem):
    b = pl.program_id(0)
    t = pl.program_id(1)
    ppb = MOBA_BLOCK // PAGE_SIZE
    n_slab = MOBA_TOPK * ppb
    tp = q_ref.shape[2]

    n = b * t_valid + t
    slot = n % 2

    def copies(step, sl, h, u):
        page = phys_ref[(step * N_HEADS_A + h) * n_slab + u]
        return (pltpu.make_async_copy(ck_ref.at[layer, page, h], kbuf.at[sl, h, u], sem.at[sl, 0]),
                pltpu.make_async_copy(cv_ref.at[layer, page, h], vbuf.at[sl, h, u], sem.at[sl, 1]))

    def issue(step, sl):
        for h in range(N_HEADS_A):
            for u in range(n_slab):
                ck, cv = copies(step, sl, h, u)
                ck.start()
                cv.start()

    @pl.when(n == 0)
    def _():
        issue(0, 0)

    @pl.when(n + 1 < pl.num_programs(0) * t_valid)
    def _():
        issue(n + 1, 1 - slot)

    pltpu.make_async_copy(kbuf.at[slot], kbuf.at[slot], sem.at[slot, 0]).wait()
    pltpu.make_async_copy(vbuf.at[slot], vbuf.at[slot], sem.at[slot, 1]).wait()

    row = lax.broadcasted_iota(I32, (tp, 1), 0)
    key = lax.broadcasted_iota(I32, (tp, tp), 1)
    for h in range(N_HEADS_A):
        qh = q_ref[0, h].astype(BF16)
        s_own = _dg(qh, kn_ref[0, h].astype(BF16), 1, 1)
        s_own = jnp.where((key <= t) & (key < t_valid), s_own, -jnp.inf)
        s_sel = [_dg(qh, kbuf[slot, h, u].astype(BF16), 1, 0) for u in range(n_slab)]
        m = jnp.max(s_own, axis=1, keepdims=True)
        for sj in s_sel:
            m = jnp.maximum(m, jnp.max(sj, axis=1, keepdims=True))
        p_own = jnp.exp(s_own - m)
        l = jnp.sum(p_own, axis=1, keepdims=True)
        acc = _dg(p_own.astype(BF16), vn_ref[0, h].astype(BF16), 1, 0)
        for u, sj in enumerate(s_sel):
            pj = jnp.exp(sj - m)
            l = l + jnp.sum(pj, axis=1, keepdims=True)
            acc = acc + _dg(pj.astype(BF16), vbuf[slot, h, u].astype(BF16), 1, 1)
        out = acc / l
        o_ref[0, 0, h] = jnp.sum(jnp.where(row == t, out, 0.0), axis=0, keepdims=True)


def _sample_attend(q_s, k_new, v_new, cache_kt, cache_vt, phys, t_valid, layer):
    db, nh, tp, dh = q_s.shape
    n_slab = MOBA_TOPK * (MOBA_BLOCK // PAGE_SIZE)
    hm = pl.BlockSpec((1, nh, tp, dh), lambda b, t, ph: (b, 0, 0, 0))
    return pl.pallas_call(
        functools.partial(_sample_attend_kernel, t_valid, layer),
        grid_spec=pltpu.PrefetchScalarGridSpec(
            num_scalar_prefetch=1,
            grid=(db, t_valid),
            in_specs=[hm, hm, hm, pl.BlockSpec(memory_space=pl.ANY), pl.BlockSpec(memory_space=pl.ANY)],
            out_specs=pl.BlockSpec((1, 1, nh, 1, dh), lambda b, t, ph: (b, t, 0, 0, 0)),
            scratch_shapes=[pltpu.VMEM((2, nh, n_slab, dh, PAGE_SIZE), F32),
                            pltpu.VMEM((2, nh, n_slab, dh, PAGE_SIZE), F32),
                            pltpu.SemaphoreType.DMA((2, 2))],
        ),
        out_shape=jax.ShapeDtypeStruct((db, t_valid, nh, 1, dh), F32),
        compiler_params=_cparams(("arbitrary", "arbitrary")),
        name="sample_attend",
    )(phys, q_s, k_new, v_new, cache_kt, cache_vt)


def _rwkv_features(t_valid, cur, edge, mu_ref, w0_ref, dup_ref, a0_ref, aup_ref, gup_ref, kk_ref, ka_ref,
                   rk_ref, seg_ref):
    first = lax.broadcasted_iota(I32, (cur.shape[0], 1), 0) == 0
    prev = jnp.where(first, edge, pltpu.roll(cur, 1, 0))
    xs = cur + (prev - cur) * mu_ref[...]
    w = WIDTH_B
    r, k, v = xs[:, :w], xs[:, w:2 * w], xs[:, 2 * w:3 * w]
    dw = xs[:, 3 * w:3 * w + DECAY_LORA]
    da = xs[:, 3 * w + DECAY_LORA:3 * w + DECAY_LORA + AAA_LORA]
    dg = xs[:, 3 * w + DECAY_LORA + AAA_LORA:]
    lw = -DECAY_SCALE * jax.nn.sigmoid(w0_ref[...] + _dot3(jnp.tanh(dw), dup_ref[...]))
    a = jax.nn.sigmoid(a0_ref[...] + _dot3(da, aup_ref[...]))
    g = _dot3(jax.nn.sigmoid(dg), gup_ref[...])
    seg = seg_ref[...]
    kkr = k * kk_ref[...]
    kk = kkr / jnp.maximum(jnp.sqrt(_dot2_exact_rhs(kkr * kkr, seg)), 1e-12)
    k2 = k * (1.0 + (a - 1.0) * ka_ref[...])
    bonus = _dot2_exact_rhs(r * k2 * rk_ref[...], seg) * v
    if t_valid < cur.shape[0]:
        valid = lax.broadcasted_iota(I32, (cur.shape[0], 1), 0) < t_valid
        lw = jnp.where(valid, lw, 0.0)
        kk = jnp.where(valid, kk, 0.0)
        k2 = jnp.where(valid, k2, 0.0)
    return r, lw, k2, v, kk, kk * a, g, bonus


def _rwkv_chunk_kernel(L, t_valid, p_ref, sh_ref, mu_ref, w0_ref, dup_ref, a0_ref, aup_ref, gup_ref, kkp_ref,
                       ka_ref, rk_ref, s0_ref, lnw_ref, lnb_ref, seg_ref, o_ref, sT_ref, st_sc, last_sc):
    c = pl.program_id(1)
    n_rows = p_ref.shape[1]
    nsub = n_rows // L
    nh, dh = N_HEADS_B, HEAD_DIM_B

    @pl.when(c == 0)
    def _():
        st_sc[...] = s0_ref[0]
        last_sc[...] = sh_ref[0]

    cur = p_ref[0]
    r_f, lw_f, k_f, v_f, kk_f, b_f, g_f, bonus_f = _rwkv_features(
        t_valid, cur, last_sc[...], mu_ref, w0_ref, dup_ref, a0_ref, aup_ref, gup_ref, kkp_ref, ka_ref, rk_ref,
        seg_ref)
    last_sc[...] = cur[n_rows - 1:n_rows, :]

    ri = lax.broadcasted_iota(I32, (L, L), 0)
    ci = lax.broadcasted_iota(I32, (L, L), 1)
    strict = ri > ci
    tri = jnp.where(ri >= ci, 1.0, 0.0).astype(BF16)
    eye_l = jnp.where(ri == ci, 1.0, 0.0)
    rk = lax.broadcasted_iota(I32, (dh, dh), 0)
    ck = lax.broadcasted_iota(I32, (dh, dh), 1)
    ri2 = lax.broadcasted_iota(I32, (L, 2 * L), 0)
    ci2 = lax.broadcasted_iota(I32, (L, 2 * L), 1)
    incl2 = ri2 >= jnp.where(ci2 >= L, ci2 - L, ci2)

    alpha, beta, kappa, rho, vh, beta_t, kappa_t, g_last = [], [], [], [], [], [], [], []
    for sub in range(nsub):
        rows = slice(sub * L, (sub + 1) * L)
        lw = lw_f[rows]
        cum = _dot2_exact_rhs_left(tri, lw)
        cum_last = cum[L - 1:L, :]
        e_neg = jnp.exp(-cum)
        e_tail = jnp.exp(cum_last - cum)
        gl_all = jnp.exp(cum_last)
        kk_all, b_all, k_all = kk_f[rows], b_f[rows], k_f[rows]
        full = [kk_all * jnp.exp(cum - lw), b_all * e_neg, k_all * e_neg, r_f[rows] * jnp.exp(cum),
                v_f[rows], b_all * e_tail, k_all * e_tail]
        for dst, x in zip((alpha, beta, kappa, rho, vh, beta_t, kappa_t), full):
            dst.extend(x[:, h * dh:(h + 1) * dh] for h in range(nh))
        g_last.extend(gl_all[:, h * dh:(h + 1) * dh] for h in range(nh))
    its = range(nsub * nh)
    wcat = [jnp.concatenate([beta[i], kappa[i]], axis=0) for i in its]
    za = [_dot3(alpha[i], wcat[i], 1, 1) for i in its]
    zr = [_dotb(rho[i], wcat[i], 1, 1) for i in its]
    n_mat = [jnp.where(strict, za[i][:, :L], 0.0) for i in its]
    m_mat = [jnp.where(strict, za[i][:, L:], 0.0) for i in its]
    nrmr = [jnp.where(incl2, zr[i], 0.0) for i in its]
    mv = [_dot3(m_mat[i], vh[i]) for i in its]
    d = [eye_l - jnp.where(ri // 2 == ci // 2, n_mat[i], 0.0) for i in its]
    s = 2
    while s < L:
        lower_left = (ri // (2 * s) == ci // (2 * s)) & ((ri % (2 * s)) >= s) & ((ci % (2 * s)) < s)
        de = [_dot3(d[i], jnp.where(lower_left, n_mat[i], 0.0)) for i in its]
        d = [d[i] - _dot3(de[i], d[i]) for i in its]
        s *= 2
    ta = [_dot3(d[i], jnp.concatenate([alpha[i], mv[i]], axis=1)) for i in its]
    abar = [ta[i][:, :dh] for i in its]
    pv = [jnp.concatenate([-ta[i][:, dh:], vh[i]], axis=0) for i in its]
    rpp = [rho[i] - _dotb(nrmr[i][:, :L], abar[i]) for i in its]
    y0 = [_dotb(nrmr[i], pv[i]) for i in its]
    gt = [jnp.where(rk == ck, g_last[i], 0.0) - _dot3(beta_t[i], abar[i], 0, 0) for i in its]
    ht = [_dot3(jnp.concatenate([beta_t[i], kappa_t[i]], axis=0), pv[i], 0, 0) for i in its]
    ys = []
    for sub in range(nsub):
        base = sub * nh
        upd = [_dot3(jnp.concatenate([rpp[base + h], gt[base + h]], axis=0), st_sc[h])
               for h in range(nh)]
        for h in range(nh):
            st_sc[h] = upd[h][L:] + ht[base + h]
        ys.append(jnp.concatenate([y0[base + h] + upd[h][:L] for h in range(nh)], axis=1))
    y = ys[0] if nsub == 1 else jnp.concatenate(ys, axis=0)
    seg = seg_ref[...]
    mu = _dot2_exact_rhs(y, seg) * (1.0 / dh)
    yc = y - mu
    var = _dot2_exact_rhs(yc * yc, seg) * (1.0 / dh)
    yn = yc * lax.rsqrt(var + LN_X_EPS) * lnw_ref[...] + lnb_ref[...]
    o_ref[0] = ((yn + bonus_f) * g_f).astype(o_ref.dtype)
    sT_ref[0] = st_sc[...]


def _dot2_exact_rhs_left(m_bf16, x):
    xh, xl = _split(x)
    return _dg(m_bf16, xh, 1, 0) + _dg(m_bf16, xl, 1, 0)


def _rwkv_mix(p_rw, shift0, s0_t, prm, chunk, chunks_per_step, t_valid):
    bq, s, cols = p_rw.shape
    w = WIDTH_B
    rows = chunk * chunks_per_step
    row = lambda v: v.reshape(1, -1).astype(F32)
    params = [row(prm['rw_mu']), row(prm['rw_w0']), prm['rw_decay_up'], row(prm['rw_a0']), prm['rw_a_up'],
              prm['rw_g_up'], row(prm['rw_k_k']), row(prm['rw_k_a']), row(prm['rw_r_k'])]
    st = pl.BlockSpec((1, N_HEADS_B, HEAD_DIM_B, HEAD_DIM_B), lambda bi, c: (bi, 0, 0, 0))
    const = lambda bi, c: (0, 0)
    return pl.pallas_call(
        functools.partial(_rwkv_chunk_kernel, chunk, t_valid),
        grid=(bq, s // rows),
        in_specs=[pl.BlockSpec((1, rows, cols), lambda bi, c: (bi, c, 0)),
                  pl.BlockSpec((1, 1, cols), lambda bi, c: (bi, 0, 0))]
        + [pl.BlockSpec(p.shape, const) for p in params]
        + [st, pl.BlockSpec((1, w), const), pl.BlockSpec((1, w), const), pl.BlockSpec((w, w), const)],
        out_specs=[pl.BlockSpec((1, rows, w), lambda bi, c: (bi, c, 0)), st],
        out_shape=[jax.ShapeDtypeStruct((bq, s, w), BF16),
                   jax.ShapeDtypeStruct((bq, N_HEADS_B, HEAD_DIM_B, HEAD_DIM_B), F32)],
        scratch_shapes=[pltpu.VMEM((N_HEADS_B, HEAD_DIM_B, HEAD_DIM_B), F32), pltpu.VMEM((1, cols), F32)],
        compiler_params=_cparams(("parallel", "arbitrary")),
        name="rwkv_mix",
    )(p_rw, shift0, *params, s0_t, row(prm['ln_x_w']), row(prm['ln_x_b']), _seg_ones(w, HEAD_DIM_B))


def _mem_kv_kernel(mem_ref, g_ref, w_ref, kg_ref, mk_ref, mv_ref):
    h = _rms(mem_ref[0], g_ref[...]).astype(BF16)
    kv = _dg(h, w_ref[...], 1, 0)
    for hm in range(N_HEADS_M):
        sl = slice(hm * HEAD_DIM_M, (hm + 1) * HEAD_DIM_M)
        mk_ref[0, :, sl] = _rms(kv[:, sl], kg_ref[...])
    mv_ref[0] = kv[:, WIDTH_M:]


def _memory_kv(mem, norm_mem, w_mem_kv, k_norm_m):
    b, m, d = mem.shape
    const = lambda i: (0, 0)
    out = pl.BlockSpec((1, m, WIDTH_M), lambda i: (i, 0, 0))
    return pl.pallas_call(
        _mem_kv_kernel,
        grid=(b,),
        in_specs=[pl.BlockSpec((1, m, d), lambda i: (i, 0, 0)), pl.BlockSpec((1, d), const),
                  pl.BlockSpec((d, 2 * WIDTH_M), const), pl.BlockSpec((1, HEAD_DIM_M), const)],
        out_specs=[out, out],
        out_shape=[jax.ShapeDtypeStruct((b, m, WIDTH_M), F32)] * 2,
        compiler_params=_cparams(("parallel",)),
        name="memory_kv",
    )(mem, norm_mem.reshape(1, d), w_mem_kv.astype(BF16), k_norm_m.reshape(1, HEAD_DIM_M))


def _mem_attend_kernel(head_rows, q_ref, mk_ref, mv_ref, g_ref, o_ref):
    q = q_ref[0]
    m = mk_ref.shape[1] // N_HEADS_M if head_rows else mk_ref.shape[1]
    for hm in range(N_HEADS_M):
        sl = slice(hm * HEAD_DIM_M, (hm + 1) * HEAD_DIM_M)
        if head_rows:
            mk_h = mk_ref[0, pl.ds(hm, m, stride=N_HEADS_M), :]
            mv_h = mv_ref[0, pl.ds(hm, m, stride=N_HEADS_M), :]
        else:
            mk_h, mv_h = mk_ref[0, :, sl], mv_ref[0, :, sl]
        qh = (_rms(q[:, sl], g_ref[...]) * (HEAD_DIM_M ** -0.5)).astype(BF16)
        s = _dg(qh, mk_h.astype(BF16), 1, 1)
        p = jnp.exp(s - jnp.max(s, axis=1, keepdims=True))
        o = _dg(p.astype(BF16), mv_h.astype(BF16), 1, 0) / jnp.sum(p, axis=1, keepdims=True)
        o_ref[0, :, sl] = o.astype(o_ref.dtype)


def _memory_attend(q_m, mk, mv, q_norm_m, tq):
    b, s, w = q_m.shape
    head_rows = mk.shape[2] == HEAD_DIM_M
    kv = pl.BlockSpec((1,) + mk.shape[1:], lambda bi, j: (bi, 0, 0))
    return pl.pallas_call(
        functools.partial(_mem_attend_kernel, head_rows),
        grid=(b, s // tq),
        in_specs=[pl.BlockSpec((1, tq, w), lambda bi, j: (bi, j, 0)), kv, kv,
                  pl.BlockSpec((1, HEAD_DIM_M), lambda bi, j: (0, 0))],
        out_specs=pl.BlockSpec((1, tq, w), lambda bi, j: (bi, j, 0)),
        out_shape=jax.ShapeDtypeStruct((b, s, w), BF16),
        compiler_params=_cparams(("parallel", "parallel")),
        name="memory_attend",
    )(q_m, mk, mv, q_norm_m.reshape(1, HEAD_DIM_M))


def _merge_kernel(x_ref, oa_ref, ob_ref, om_ref, gl_ref, wb_ref, wo_ref, ng_ref, rw_ref, rb_ref,
                  x1_ref, h_ref, e_ref, gate_ref, rank_ref, cnt_ref, base_sc):
    d = x_ref.shape[1]
    gl = gl_ref[...]
    merged = jnp.zeros(x_ref.shape, F32)
    for n, o_ref in enumerate((oa_ref, ob_ref, om_ref)):
        y = _dg(o_ref[...], wb_ref[n], 1, 0)
        merged = merged + jax.nn.sigmoid(gl[:, n * d:(n + 1) * d].astype(F32)) * y
    x1 = x_ref[...] + _dg(merged.astype(BF16), wo_ref[...], 1, 0)
    x1_ref[...] = x1
    hn = _rms(x1, ng_ref[...])
    h_ref[...] = hn
    logits = _dot3(hn, rw_ref[...]) + rb_ref[...]
    tm, ne = logits.shape
    e_idx = lax.broadcasted_iota(I32, (tm, ne), 1)
    vals, idxs = [], []
    g = logits
    for _ in range(TOP_K):
        mx = jnp.max(g, axis=1, keepdims=True)
        first = jnp.min(jnp.where(g == mx, e_idx, ne), axis=1, keepdims=True)
        vals.append(mx)
        idxs.append(first)
        g = jnp.where(e_idx == first, -jnp.inf, g)
    top = jnp.concatenate(vals, axis=1)
    pe = jnp.exp(top - vals[0])
    gate_ref[...] = pe / jnp.sum(pe, axis=1, keepdims=True)
    e_ref[...] = jnp.concatenate(idxs, axis=1)

    @pl.when(pl.program_id(0) == 0)
    def _():
        base_sc[...] = jnp.zeros_like(base_sc)

    ohs = [jnp.where(e_idx == idx, 1.0, 0.0) for idx in idxs]
    cnt = ohs[0] + ohs[1] + ohs[2] + ohs[3]
    ri = lax.broadcasted_iota(I32, (tm, tm), 0)
    ci = lax.broadcasted_iota(I32, (tm, tm), 1)
    tri = jnp.where(ri > ci, 1.0, 0.0).astype(BF16)
    tot = _dg(tri, cnt.astype(BF16), 1, 0) + base_sc[...]
    rank_ref[...] = jnp.concatenate([jnp.sum(oh * tot, axis=1, keepdims=True) for oh in ohs],
                                    axis=1).astype(I32)
    base_sc[...] = base_sc[...] + jnp.sum(cnt, axis=0, keepdims=True)
    cnt_ref[...] = base_sc[...].astype(I32)


def _merge_and_route(x2, o_a, o_b, o_m, gl, prm, tm):
    n, d = x2.shape
    const2 = lambda i: (0, 0)
    tok = lambda wd: pl.BlockSpec((tm, wd), lambda i: (i, 0))
    return pl.pallas_call(
        _merge_kernel,
        grid=(n // tm,),
        in_specs=[tok(d), tok(512), tok(512), tok(512), tok(3 * d),
                  pl.BlockSpec((3, 512, d), lambda i: (0, 0, 0)), pl.BlockSpec((d, d), const2),
                  pl.BlockSpec((1, d), const2), pl.BlockSpec((d, N_EXPERTS), const2),
                  pl.BlockSpec((1, N_EXPERTS), const2)],
        out_specs=[tok(d), tok(d), tok(TOP_K), tok(TOP_K), tok(TOP_K), pl.BlockSpec((1, N_EXPERTS), const2)],
        out_shape=[jax.ShapeDtypeStruct((n, d), F32), jax.ShapeDtypeStruct((n, d), F32),
                   jax.ShapeDtypeStruct((n, TOP_K), I32), jax.ShapeDtypeStruct((n, TOP_K), F32),
                   jax.ShapeDtypeStruct((n, TOP_K), I32), jax.ShapeDtypeStruct((1, N_EXPERTS), I32)],
        scratch_shapes=[pltpu.VMEM((1, N_EXPERTS), F32)],
        compiler_params=_cparams(("arbitrary",)),
        name="merge_route",
    )(x2, o_a, o_b, o_m, gl, prm['w_branch'].astype(BF16), prm['w_out'].astype(BF16),
      prm['norm_ffn'].reshape(1, d), prm['router_w'], prm['router_b'].reshape(1, N_EXPERTS))


def _onehots(e):
    tm = e.shape[0]
    e_idx = lax.broadcasted_iota(I32, (tm, N_EXPERTS), 1)
    return [jnp.where(e[:, k:k + 1] == e_idx, 1.0, 0.0) for k in range(TOP_K)]


def _moe_rows_kernel(e_ref, rank_ref, start_ref, row_ref):
    ohs = _onehots(e_ref[...])
    st = start_ref[...].astype(F32)
    base = jnp.concatenate([jnp.sum(oh * st, axis=1, keepdims=True) for oh in ohs], axis=1)
    row_ref[...] = rank_ref[...] + base.astype(I32)


def _moe_rows(top_e, rank, starts, tm):
    n = top_e.shape[0]
    tok = pl.BlockSpec((tm, TOP_K), lambda i: (i, 0))
    return pl.pallas_call(
        _moe_rows_kernel,
        grid=(n // tm,),
        in_specs=[tok, tok, pl.BlockSpec((1, N_EXPERTS), lambda i: (0, 0))],
        out_specs=tok,
        out_shape=jax.ShapeDtypeStruct((n, TOP_K), I32),
        compiler_params=_cparams(("parallel",)),
        name="moe_rows",
    )(top_e, rank, starts)


def _dispatch_kernel(row_ref, h_ref, xs_in_ref, xs_ref, sem):
    del xs_in_ref
    groups = h_ref.shape[0]

    def start(g, carry):
        for u in range(SUBLANES):
            for k in range(TOP_K):
                r = row_ref[(g * SUBLANES + u) * TOP_K + k]
                pltpu.make_async_copy(h_ref.at[g, pl.ds(u, 1)], xs_ref.at[pl.ds(r, 1)], sem).start()
        return carry

    lax.fori_loop(0, groups, start, 0)
    all_rows = xs_ref.at[pl.ds(0, groups * SUBLANES * TOP_K)]
    pltpu.make_async_copy(all_rows, all_rows, sem).wait()


def _zero_rows_kernel(o_ref):
    o_ref[...] = jnp.zeros_like(o_ref)


def _zero_rows(n_rows, d):
    tile = next(t for t in (1024, 512, 256, MOE_ROWS_SMALL) if n_rows % t == 0)
    return pl.pallas_call(
        _zero_rows_kernel,
        grid=(n_rows // tile,),
        out_specs=pl.BlockSpec((tile, d), lambda i: (i, 0)),
        out_shape=jax.ShapeDtypeStruct((n_rows, d), F32),
        compiler_params=_cparams(("parallel",)),
        name="zero_rows",
    )()


def _moe_dispatch(h, row_flat, n_rows, tm):
    n, d = h.shape
    xs0 = _zero_rows(n_rows, d)
    return pl.pallas_call(
        _dispatch_kernel,
        grid=(n // tm,),
        in_specs=[pl.BlockSpec((tm * TOP_K,), lambda i: (i,), memory_space=pltpu.SMEM),
                  pl.BlockSpec((tm // SUBLANES, SUBLANES, d), lambda i: (i, 0, 0)),
                  pl.BlockSpec(memory_space=pl.ANY)],
        out_specs=pl.BlockSpec(memory_space=pl.ANY),
        out_shape=jax.ShapeDtypeStruct((n_rows, d), F32),
        scratch_shapes=[pltpu.SemaphoreType.DMA(())],
        input_output_aliases={2: 0},
        compiler_params=_cparams(("arbitrary",)),
        name="moe_dispatch",
    )(row_flat, h.reshape(n // SUBLANES, SUBLANES, d), xs0)


def _swiglu(u, d_ff):
    u_glu = jnp.minimum(u[:, :d_ff], SWIGLU_LIMIT)
    u_lin = jnp.clip(u[:, d_ff:], -SWIGLU_LIMIT, SWIGLU_LIMIT)
    return u_glu * jax.nn.sigmoid(SWIGLU_ALPHA * u_glu) * (u_lin + 1.0)


def _moe_ffn_kernel(be_ref, nu_ref, xs_ref, w1_ref, b1_ref, w2_ref, b2_ref, y_ref, w1_sc, w2_sc):
    i = pl.program_id(0)
    prev = be_ref[jnp.maximum(i - 1, 0)]
    first = (i == 0) | (be_ref[i] != prev)

    @pl.when(first)
    def _():
        w1_sc[...] = w1_ref[...].astype(BF16)
        w2_sc[...] = w2_ref[...].astype(BF16)

    @pl.when(i < nu_ref[0])
    def _():
        u = _dg(xs_ref[...].astype(BF16), w1_sc[...], 1, 0) + b1_ref[...]
        act = _swiglu(u, w2_ref.shape[0])
        y_ref[...] = _dg(act.astype(BF16), w2_sc[...], 1, 0) + b2_ref[...]

    @pl.when(i >= nu_ref[0])
    def _():
        y_ref[...] = jnp.zeros_like(y_ref)


def _moe_ffn(xs, blk_e, n_used, w1, b1, w2, b2, rows):
    n_rows, d = xs.shape
    ne, _, f2 = w1.shape
    d_ff = w2.shape[1]
    nblk = n_rows // rows
    return pl.pallas_call(
        _moe_ffn_kernel,
        grid_spec=pltpu.PrefetchScalarGridSpec(
            num_scalar_prefetch=2,
            grid=(nblk,),
            in_specs=[pl.BlockSpec((rows, d), lambda i, be, nu: (i, 0)),
                      pl.BlockSpec((None, d, f2), lambda i, be, nu: (be[i], 0, 0)),
                      pl.BlockSpec((None, 1, f2), lambda i, be, nu: (be[i], 0, 0)),
                      pl.BlockSpec((None, d_ff, d), lambda i, be, nu: (be[i], 0, 0)),
                      pl.BlockSpec((None, 1, d), lambda i, be, nu: (be[i], 0, 0))],
            out_specs=pl.BlockSpec((rows, d), lambda i, be, nu: (i, 0)),
            scratch_shapes=[pltpu.VMEM((d, f2), BF16), pltpu.VMEM((d_ff, d), BF16)],
        ),
        out_shape=jax.ShapeDtypeStruct((n_rows, d), F32),
        compiler_params=_cparams(("arbitrary",)),
        name="moe_ffn",
    )(blk_e, n_used, xs, w1, b1.reshape(ne, 1, f2), w2, b2.reshape(ne, 1, d))


def _combine_kernel(row_ref, yb_ref, x1_ref, gate_ref, y_ref, buf, sem):
    tm = x1_ref.shape[0]

    half = tm // 2
    per_tile = 2 * TOP_K

    def start(g, carry):
        for c in range(2):
            for k in range(TOP_K):
                r = row_ref[(c * half + g) * TOP_K + k]
                pltpu.make_async_copy(yb_ref.at[pl.ds(r, 1)], buf.at[g, pl.ds(TOP_K * c + k, 1)], sem).start()
        return carry

    lax.fori_loop(0, half, start, 0)
    pltpu.make_async_copy(buf, buf, sem).wait()
    gates = gate_ref[...]
    for c in range(2):
        tok = slice(c * half, (c + 1) * half)
        acc = buf[:, TOP_K * c, :] * gates[tok, 0:1]
        for k in range(1, TOP_K):
            acc = acc + buf[:, TOP_K * c + k, :] * gates[tok, k:k + 1]
        y_ref[tok, :] = x1_ref[tok, :] + acc


def _moe_combine(yb, row_flat, x1, gates, tm):
    n, d = x1.shape
    return pl.pallas_call(
        _combine_kernel,
        grid=(n // tm,),
        in_specs=[pl.BlockSpec((tm * TOP_K,), lambda i: (i,), memory_space=pltpu.SMEM),
                  pl.BlockSpec(memory_space=pl.ANY),
                  pl.BlockSpec((tm, d), lambda i: (i, 0)),
                  pl.BlockSpec((tm, TOP_K), lambda i: (i, 0))],
        out_specs=pl.BlockSpec((tm, d), lambda i: (i, 0)),
        out_shape=jax.ShapeDtypeStruct((n, d), F32),
        scratch_shapes=[pltpu.VMEM((tm // 2, 2 * TOP_K, d), F32), pltpu.SemaphoreType.DMA(())],
        compiler_params=_cparams(("arbitrary",)),
        name="moe_combine",
    )(row_flat, yb, x1, gates)


def _moe_block(x1, hn, top_e, gates, rank, counts, prm, tm):
    n, d = x1.shape
    rows = MOE_ROWS if n * TOP_K >= N_EXPERTS * MOE_ROWS else MOE_ROWS_SMALL
    counts = counts[0]
    padded = (counts + rows - 1) // rows * rows
    p_end = jnp.cumsum(padded)
    starts = (p_end - padded).astype(I32)
    nblk = -(-(n * TOP_K + N_EXPERTS * (rows - 1)) // rows)
    n_used = (p_end[-1] // rows).astype(I32)
    blk_i = jnp.minimum(jnp.arange(nblk, dtype=I32), n_used - 1)
    n_before = jnp.sum((p_end[None, :] <= (blk_i * rows)[:, None]).astype(I32), axis=1)
    blk_e = jnp.minimum(n_before, N_EXPERTS - 1).astype(I32)
    row = _moe_rows(top_e, rank, starts[None, :], tm)
    row_flat = row.reshape(-1)
    xs = _moe_dispatch(hn, row_flat, nblk * rows, tm)
    yb = _moe_ffn(xs, blk_e, n_used[None], prm['moe_w1'], prm['moe_b1'], prm['moe_w2'], prm['moe_b2'], rows)
    return _moe_combine(yb, row_flat, x1, gates, tm)


def _split_w_in(w_in):
    d = w_in.shape[0]
    a = 3 * WIDTH_A
    b = a + RWKV_COLS
    c = b + WIDTH_M
    wb = w_in.astype(BF16)
    return [wb[:, :a], wb[:, a:b], wb[:, b:c], wb[:, c:]]


def _token_mix_tail(x2, o_a, o_b, o_m, gl, prm, tm):
    x1, hn, top_e, gates, rank, counts = _merge_and_route(x2, o_a, o_b, o_m, gl, prm, tm)
    return _moe_block(x1, hn, top_e, gates, rank, counts, prm, tm)


def _layer_prompt(x, mem, prm, ws):
    b, s, d = x.shape
    tm = 256
    x2 = x.reshape(b * s, d)
    pos = jnp.arange(s, dtype=I32)
    p_rw, q_m, gl, q_s, k_t, v_t, kaug, vaug, kmean = _projection_and_moba_prep(
        x2, prm['norm_mix'].reshape(1, d), ws, b, pos, prm['q_norm_a'], prm['k_norm_a'])
    k_o, v_o = jnp.swapaxes(k_t, 2, 3), jnp.swapaxes(v_t, 2, 3)
    o_a = _moba_flash(q_s, kaug, vaug, kmean).reshape(b * s, WIDTH_A)

    p3 = p_rw.reshape(b, s, RWKV_COLS)
    s0_t = jnp.zeros((b, N_HEADS_B, HEAD_DIM_B, HEAD_DIM_B), F32)
    o_b, st = _rwkv_mix(p3, jnp.zeros((b, 1, RWKV_COLS), F32), s0_t, prm, RWKV_CHUNK, RWKV_CHUNKS_PER_STEP,
                        RWKV_CHUNK * RWKV_CHUNKS_PER_STEP)
    wkv = jnp.swapaxes(st, 2, 3)

    mk, mv = _memory_kv(mem, prm['norm_mem'], prm['w_mem_kv'], prm['k_norm_m'])
    o_m = _memory_attend(q_m.reshape(b, s, WIDTH_M), mk, mv, prm['q_norm_m'], 512)

    y = _token_mix_tail(x2, o_a, o_b.reshape(b * s, WIDTH_B), o_m.reshape(b * s, WIDTH_M), gl, prm, tm)
    m = mem.shape[1]
    return (y.reshape(b, s, d), k_o, v_o, wkv, p3[:, -1],
            mk.reshape(b, m, N_HEADS_M, HEAD_DIM_M), mv.reshape(b, m, N_HEADS_M, HEAD_DIM_M))


def _layer_sample(x, cache_k, cache_v, mem_k, mem_v, wkv0, shift0, page_table, layer, prm, ws):
    db, t, d = x.shape
    tp = SAMPLE_T_PAD
    past_len = page_table.shape[1] * PAGE_SIZE
    assert past_len % MOBA_BLOCK == 0 and t <= tp
    ppb = MOBA_BLOCK // PAGE_SIZE
    n = db * tp
    x2 = jnp.pad(x, ((0, 0), (0, tp - t), (0, 0))).reshape(n, d)
    qkv, p_rw, q_m, gl = _in_projection(x2, prm['norm_mix'].reshape(1, d), ws, n)
    pos = past_len + jnp.arange(tp, dtype=I32)
    q_s, k_o, v_o = _moba_prep(qkv.reshape(db, tp, -1), pos, prm['q_norm_a'], prm['k_norm_a'], tp)

    cache_kt, cache_vt = jnp.swapaxes(cache_k, 3, 4), jnp.swapaxes(cache_v, 3, 4)
    kmean_t = _page_means(cache_kt, page_table, layer)
    idx = _sample_select(q_s, kmean_t)
    idx = jnp.transpose(idx[:, :, :t], (0, 2, 1, 3))
    logical = idx[..., None] * ppb + jnp.arange(ppb, dtype=I32)
    phys = page_table[jnp.arange(db)[:, None, None, None, None], logical].reshape(-1).astype(I32)
    o_a = _sample_attend(q_s, k_o, v_o, cache_kt, cache_vt, phys, t, layer)
    o_a = jnp.pad(o_a.reshape(db, t, WIDTH_A), ((0, 0), (0, tp - t), (0, 0))).reshape(n, WIDTH_A).astype(BF16)

    p3 = p_rw.reshape(db, tp, RWKV_COLS)
    o_b, st = _rwkv_mix(p3, shift0[:, None, :], jnp.swapaxes(wkv0, 2, 3), prm, tp, 1, t)
    wkv = jnp.swapaxes(st, 2, 3)

    m = mem_k.shape[1]
    o_m = _memory_attend(q_m.reshape(db, tp, WIDTH_M), mem_k.reshape(db, m * N_HEADS_M, HEAD_DIM_M),
                         mem_v.reshape(db, m * N_HEADS_M, HEAD_DIM_M), prm['q_norm_m'], tp)

    y = _token_mix_tail(x2, o_a, o_b.reshape(n, WIDTH_B), o_m.reshape(n, WIDTH_M), gl, prm, n)
    return (y.reshape(db, tp, d)[:, :t], k_o[:, :, :t], v_o[:, :, :t], wkv, p3[:, t - 1])


def kernel(x_prompt, x_sample, mem_prompt, cache_k, cache_v, cache_mem_k, cache_mem_v, state_wkv, state_shift,
           page_table, norm_mix, norm_mem, norm_ffn, w_in, q_norm_a, k_norm_a, q_norm_m, k_norm_m, w_mem_kv,
           rw_mu, rw_w0, rw_decay_up, rw_a0, rw_a_up, rw_g_up, rw_k_k, rw_k_a, rw_r_k, ln_x_w, ln_x_b,
           w_branch, w_out, router_w, router_b, moe_w1, moe_b1, moe_w2, moe_b2):
    depth = w_in.shape[0]
    xp, xs = x_prompt, x_sample
    outs_p = [[] for _ in range(6)]
    outs_s = [[] for _ in range(4)]
    for l in range(depth):
        prm = dict(norm_mix=norm_mix[l], norm_mem=norm_mem[l], norm_ffn=norm_ffn[l], w_in=w_in[l],
                   q_norm_a=q_norm_a[l], k_norm_a=k_norm_a[l], q_norm_m=q_norm_m[l], k_norm_m=k_norm_m[l],
                   w_mem_kv=w_mem_kv[l], rw_mu=rw_mu[l], rw_w0=rw_w0[l], rw_decay_up=rw_decay_up[l],
                   rw_a0=rw_a0[l], rw_a_up=rw_a_up[l], rw_g_up=rw_g_up[l], rw_k_k=rw_k_k[l], rw_k_a=rw_k_a[l],
                   rw_r_k=rw_r_k[l].reshape(-1), ln_x_w=ln_x_w[l], ln_x_b=ln_x_b[l], w_branch=w_branch[l],
                   w_out=w_out[l], router_w=router_w[l], router_b=router_b[l], moe_w1=moe_w1[l],
                   moe_b1=moe_b1[l], moe_w2=moe_w2[l], moe_b2=moe_b2[l])
        ws = _split_w_in(prm['w_in'])
        xp, *rest_p = _layer_prompt(xp, mem_prompt, prm, ws)
        for acc, val in zip(outs_p, rest_p):
            acc.append(val)
        xs, *rest_s = _layer_sample(xs, cache_k, cache_v, cache_mem_k[l], cache_mem_v[l], state_wkv[l],
                                    state_shift[l], page_table, l, prm, ws)
        for acc, val in zip(outs_s, rest_s):
            acc.append(val)
    kp, vp, wkvp, shp, mkp, mvp = [jnp.stack(a) for a in outs_p]
    ksm, vsm, wkvs, shs = [jnp.stack(a) for a in outs_s]
    return (xp, xs, kp, vp, wkvp, shp, mkp, mvp, ksm, vsm, wkvs, shs)
```

```python
import functools
import math

import jax
import jax.numpy as jnp
from jax import lax
from jax.experimental import pallas as pl
from jax.experimental.pallas import tpu as pltpu

F32 = jnp.float32
BF16 = jnp.bfloat16
I32 = jnp.int32

N_HEADS_A = 8
HEAD_DIM_A = 64
WIDTH_A = 512
MOBA_BLOCK = 256
MOBA_TOPK = 3
ROT_DIM = 16
ROPE_THETA = 500000.0
PAGE_SIZE = 128
N_HEADS_B = 8
HEAD_DIM_B = 64
WIDTH_B = 512
DECAY_LORA = 64
AAA_LORA = 64
GATE_LORA = 128
RWKV_COLS = 1792
DECAY_SCALE = math.exp(-0.5)
LN_X_EPS = 64e-5
N_HEADS_M = 4
HEAD_DIM_M = 128
WIDTH_M = 512
N_EXPERTS = 32
TOP_K = 4
SWIGLU_ALPHA = 1.702
SWIGLU_LIMIT = 7.0
NORM_EPS = 1e-6

SUBLANES = 8
NEG_BIG = -1e30
SAMPLE_T_PAD = 8
RWKV_CHUNK = 64
RWKV_CHUNKS_PER_STEP = 4
MOE_ROWS = 512
MOE_ROWS_SMALL = 128
MOE_ROWS_TILE = 2048
VMEM_LIMIT = 56 * 1024 * 1024


def _cparams(sem, vmem=None):
    return pltpu.CompilerParams(dimension_semantics=sem, vmem_limit_bytes=vmem or VMEM_LIMIT)


def _dg(a, b, ca, cb):
    return lax.dot_general(a, b, (((ca,), (cb,)), ((), ())), preferred_element_type=F32)


def _split(x):
    hi = x.astype(BF16)
    lo = (x - hi.astype(F32)).astype(BF16)
    return hi, lo


def _dot3(a, b, ca=1, cb=0):
    ah, al = _split(a)
    bh, bl = _split(b)
    return _dg(ah, bh, ca, cb) + (_dg(ah, bl, ca, cb) + _dg(al, bh, ca, cb))


def _dot2_exact_rhs(a, b_bf16):
    ah, al = _split(a)
    return _dg(ah, b_bf16, 1, 0) + _dg(al, b_bf16, 1, 0)


def _dotb(a, b, ca=1, cb=0):
    return _dg(a.astype(BF16), b.astype(BF16), ca, cb)


def _rms(x, gain_row):
    ms = jnp.mean(x * x, axis=-1, keepdims=True)
    return x * lax.rsqrt(ms + NORM_EPS) * gain_row


def _seg_ones(width, seg):
    r = lax.broadcasted_iota(I32, (width, width), 0) // seg
    c = lax.broadcasted_iota(I32, (width, width), 1) // seg
    return jnp.where(r == c, 1.0, 0.0).astype(BF16)


def _proj_kernel(x_ref, g_ref, w1, w2, w3, w4, o1, o2, o3, o4):
    h = _rms(x_ref[...], g_ref[...]).astype(BF16)
    o1[...] = _dg(h, w1[...], 1, 0)
    o2[...] = _dg(h, w2[...], 1, 0)
    o3[...] = _dg(h, w3[...], 1, 0)
    o4[...] = _dg(h, w4[...], 1, 0).astype(o4.dtype)


def _in_projection(x2, gain, ws, tm):
    n, d = x2.shape
    widths = [w.shape[1] for w in ws]
    dtypes = [F32, F32, F32, BF16]
    const = lambda i: (0, 0)
    return pl.pallas_call(
        _proj_kernel,
        grid=(n // tm,),
        in_specs=[pl.BlockSpec((tm, d), lambda i: (i, 0)), pl.BlockSpec((1, d), const)]
        + [pl.BlockSpec((d, wd), const) for wd in widths],
        out_specs=[pl.BlockSpec((tm, wd), lambda i: (i, 0)) for wd in widths],
        out_shape=[jax.ShapeDtypeStruct((n, wd), dt) for wd, dt in zip(widths, dtypes)],
        compiler_params=_cparams(("parallel",)),
        name="in_projection",
    )(x2, gain, *ws)


def _rope_tables(pos):
    half = ROT_DIM // 2
    inv_freq = 1.0 / (ROPE_THETA ** (jnp.arange(0, ROT_DIM, 2, dtype=F32) / ROT_DIM))
    lane = jnp.arange(128, dtype=I32) % HEAD_DIM_A
    ang = pos.astype(F32)[:, None] * inv_freq[lane % half][None, :]
    cos, sin = jnp.cos(ang), jnp.sin(ang)
    c = jnp.where(lane < ROT_DIM, cos, 1.0)
    s_up = jnp.where(lane < half, -sin, 0.0)
    s_dn = jnp.where((lane >= half) & (lane < ROT_DIM), sin, 0.0)
    return c, s_up, s_dn


def _norm_rope(x, seg, gain, c, s_up, s_dn):
    ss = _dot2_exact_rhs(x * x, seg)
    y = x * lax.rsqrt(ss * (1.0 / HEAD_DIM_A) + NORM_EPS) * gain
    half = ROT_DIM // 2
    up = pltpu.roll(y, WIDTH_A - half, 1)
    dn = pltpu.roll(y, half, 1)
    return y * c + up * s_up + dn * s_dn


def _moba_prep_kernel(qkv_ref, seg_ref, qg_ref, kg_ref, c_ref, su_ref, sd_ref, *outs):
    _prep_body(False, qkv_ref[0], None, seg_ref, qg_ref, kg_ref, c_ref, su_ref, sd_ref, outs)


def _proj_prep_kernel(nb, x_ref, g_ref, w1, w2, w3, w4, seg_ref, qg_ref, kg_ref, c_ref, su_ref, sd_ref,
                      o_rw, o_qm, o_gl, *outs):
    h = _rms(x_ref[...], g_ref[...]).astype(BF16)
    _prep_body(True, _dg(h, w1[...], 1, 0), pl.program_id(0) % nb, seg_ref, qg_ref, kg_ref,
               c_ref, su_ref, sd_ref, outs)
    o_rw[...] = _dg(h, w2[...], 1, 0)
    o_qm[...] = _dg(h, w3[...], 1, 0)
    o_gl[...] = _dg(h, w4[...], 1, 0).astype(o_gl.dtype)


def _prep_body(with_blocks, x, blk, seg_ref, qg_ref, kg_ref, c_ref, su_ref, sd_ref, outs):
    if with_blocks:
        qs_ref, kt_ref, vt_ref, kaug_ref, vaug_ref, kmean_ref = outs
    else:
        qs_ref, k_ref, v_ref = outs
    tm = x.shape[0]
    rep = lambda r: jnp.concatenate([r[...]] * (WIDTH_A // 128), axis=1)
    c, su, sd = rep(c_ref), rep(su_ref), rep(sd_ref)
    seg = seg_ref[...]
    q = _norm_rope(x[:, :WIDTH_A], seg, qg_ref[...], c, su, sd) * (HEAD_DIM_A ** -0.5)
    k = _norm_rope(x[:, WIDTH_A:2 * WIDTH_A], seg, kg_ref[...], c, su, sd)
    v = x[:, 2 * WIDTH_A:]
    if with_blocks:
        lane = lax.broadcasted_iota(I32, (tm, HEAD_DIM_A), 1)
        onehot = jnp.where(lane == blk, 1.0, 0.0).astype(BF16)
        ones_col = jnp.where(lane == 0, 1.0, 0.0).astype(BF16)

        @pl.when(blk == 0)
        def _():
            kmean_ref[...] = jnp.zeros_like(kmean_ref)

    for h in range(N_HEADS_A):
        sl = slice(h * HEAD_DIM_A, (h + 1) * HEAD_DIM_A)
        qs_ref[0, h] = q[:, sl]
        if with_blocks:
            kt_ref[0, h] = k[:, sl].T
            vt_ref[0, h] = v[:, sl].T
            kaug_ref[0, h] = jnp.concatenate([k[:, sl].astype(BF16), onehot], axis=1)
            vaug_ref[0, h] = jnp.concatenate([v[:, sl].astype(BF16), ones_col], axis=1)
            kmean_ref[0, h, pl.ds(blk, 1), :] = jnp.mean(k[:, sl], axis=0, keepdims=True)
        else:
            k_ref[0, h] = k[:, sl]
            v_ref[0, h] = v[:, sl]


def _moba_prep(qkv, pos, q_gain, k_gain, tm):
    bq, s, _ = qkv.shape
    c, su, sd = _rope_tables(pos)
    seg = _seg_ones(WIDTH_A, HEAD_DIM_A)
    tile8 = lambda g: jnp.tile(g.astype(F32), N_HEADS_A)[None, :]
    hm = jax.ShapeDtypeStruct((bq, N_HEADS_A, s, HEAD_DIM_A), F32)
    hm_spec = pl.BlockSpec((1, N_HEADS_A, tm, HEAD_DIM_A), lambda b, j: (b, 0, j, 0))
    const = lambda b, j: (0, 0)
    tab = pl.BlockSpec((tm, 128), lambda b, j: (j, 0))
    return pl.pallas_call(
        _moba_prep_kernel,
        grid=(bq, s // tm),
        in_specs=[pl.BlockSpec((1, tm, 3 * WIDTH_A), lambda b, j: (b, j, 0)),
                  pl.BlockSpec((WIDTH_A, WIDTH_A), const),
                  pl.BlockSpec((1, WIDTH_A), const), pl.BlockSpec((1, WIDTH_A), const), tab, tab, tab],
        out_specs=[hm_spec, hm_spec, hm_spec],
        out_shape=[hm, hm, hm],
        compiler_params=_cparams(("parallel", "parallel")),
        name="moba_prep",
    )(qkv, seg, tile8(q_gain), tile8(k_gain), c, su, sd)


def _projection_and_moba_prep(x2, gain, ws, bq, pos, q_gain, k_gain):
    n, d = x2.shape
    tm = MOBA_BLOCK
    s = n // bq
    nb = s // tm
    assert nb <= HEAD_DIM_A
    c, su, sd = _rope_tables(pos)
    seg = _seg_ones(WIDTH_A, HEAD_DIM_A)
    tile8 = lambda g: jnp.tile(g.astype(F32), N_HEADS_A)[None, :]
    widths = [w.shape[1] for w in ws]
    const = lambda i: (0, 0)
    tok = lambda wd: pl.BlockSpec((tm, wd), lambda i: (i, 0))
    hm_spec = pl.BlockSpec((1, N_HEADS_A, tm, HEAD_DIM_A), lambda i: (i // nb, 0, i % nb, 0))
    tr_spec = pl.BlockSpec((1, N_HEADS_A, HEAD_DIM_A, tm), lambda i: (i // nb, 0, 0, i % nb))
    aug_spec = pl.BlockSpec((1, N_HEADS_A, tm, 128), lambda i: (i // nb, 0, i % nb, 0))
    km_spec = pl.BlockSpec((1, N_HEADS_A, HEAD_DIM_A, HEAD_DIM_A), lambda i: (i // nb, 0, 0, 0))
    tab = pl.BlockSpec((tm, 128), lambda i: (i % nb, 0))
    hm = jax.ShapeDtypeStruct((bq, N_HEADS_A, s, HEAD_DIM_A), F32)
    tr = jax.ShapeDtypeStruct((bq, N_HEADS_A, HEAD_DIM_A, s), F32)
    aug = jax.ShapeDtypeStruct((bq, N_HEADS_A, s, 128), BF16)
    return pl.pallas_call(
        functools.partial(_proj_prep_kernel, nb),
        grid=(n // tm,),
        in_specs=[tok(d), pl.BlockSpec((1, d), const)] + [pl.BlockSpec((d, wd), const) for wd in widths]
        + [pl.BlockSpec((WIDTH_A, WIDTH_A), const), pl.BlockSpec((1, WIDTH_A), const),
           pl.BlockSpec((1, WIDTH_A), const), tab, tab, tab],
        out_specs=[tok(widths[1]), tok(widths[2]), tok(widths[3]),
                   hm_spec, tr_spec, tr_spec, aug_spec, aug_spec, km_spec],
        out_shape=[jax.ShapeDtypeStruct((n, widths[1]), F32), jax.ShapeDtypeStruct((n, widths[2]), F32),
                   jax.ShapeDtypeStruct((n, widths[3]), BF16), hm, tr, tr, aug, aug,
                   jax.ShapeDtypeStruct((bq, N_HEADS_A, HEAD_DIM_A, HEAD_DIM_A), F32)],
        compiler_params=_cparams(("arbitrary",)),
        name="projection_moba_prep",
    )(x2, gain, *ws, seg, tile8(q_gain), tile8(k_gain), c, su, sd)


FLASH_HEADS = 4


def _moba_flash_kernel(q_ref, kaug_ref, vaug_ref, kmean_ref, o_ref, m_sc, acc_sc):
    i = pl.program_id(1)
    hp = pl.program_id(2)
    tq, dh = q_ref.shape[2], q_ref.shape[3]
    gs = range(FLASH_HEADS)
    q = [q_ref[0, g] for g in gs]
    gate_t = [_dot3(kmean_ref[0, g], q[g], 1, 1) for g in gs]
    n_idx = lax.broadcasted_iota(I32, gate_t[0].shape, 0)
    n_tot = gate_t[0].shape[0]
    gv = [jnp.where(n_idx < i, gate_t[g], -jnp.inf) for g in gs]
    sel = [n_idx == i for g in gs]
    for _ in range(MOBA_TOPK):
        mx = [jnp.max(gv[g], axis=0, keepdims=True) for g in gs]
        cand = [(gv[g] == mx[g]) & (mx[g] > -jnp.inf) for g in gs]
        first = [jnp.min(jnp.where(cand[g], n_idx, n_tot), axis=0, keepdims=True) for g in gs]
        pick = [n_idx == first[g] for g in gs]
        sel = [sel[g] | pick[g] for g in gs]
        gv = [jnp.where(pick[g], -jnp.inf, gv[g]) for g in gs]
    zeros_t = jnp.zeros((dh, tq), F32)
    zeros_q = jnp.zeros((tq, dh), F32)
    bias = [jnp.concatenate([zeros_t, jnp.where(sel[g], 0.0, NEG_BIG)], axis=0).T for g in gs]
    qaug = [(jnp.concatenate([q[g], zeros_q], axis=1) + bias[g]).astype(BF16) for g in gs]

    start = pl.multiple_of(i * tq, tq)
    row = lax.broadcasted_iota(I32, (tq, tq), 0)
    col = lax.broadcasted_iota(I32, (tq, tq), 1)
    s = [jnp.where(col <= row, _dg(qaug[g], kaug_ref[0, g, pl.ds(start, tq), :], 1, 1), -jnp.inf) for g in gs]
    m0 = [jnp.max(s[g], axis=1, keepdims=True) for g in gs]
    for g in gs:
        m_sc[g] = jnp.broadcast_to(m0[g], (tq, 128))
        acc_sc[g] = _dg(jnp.exp(s[g] - m0[g]).astype(BF16), vaug_ref[0, g, pl.ds(start, tq), :], 1, 0)

    def step(off, width):
        sj = [_dg(qaug[g], kaug_ref[0, g, pl.ds(off, width), :], 1, 1) for g in gs]
        m_old = [m_sc[g] for g in gs]
        m_new = [jnp.maximum(m_old[g], jnp.max(sj[g], axis=1, keepdims=True)) for g in gs]
        pj = [jnp.exp(sj[g] - jnp.concatenate([m_new[g]] * (width // 128), axis=1)) for g in gs]
        for g in gs:
            acc_sc[g] = (jnp.exp(m_old[g] - m_new[g]) * acc_sc[g]
                         + _dg(pj[g].astype(BF16), vaug_ref[0, g, pl.ds(off, width), :], 1, 0))
            m_sc[g] = m_new[g]

    def quad(j, carry):
        step(pl.multiple_of(j * (4 * tq), 4 * tq), 4 * tq)
        return carry

    lax.fori_loop(0, i // 4, quad, 0)

    @pl.when(i % 4 >= 2)
    def _():
        step(pl.multiple_of((i // 4) * (4 * tq), 2 * tq), 2 * tq)

    @pl.when(i % 2 == 1)
    def _():
        step(pl.multiple_of((i - 1) * tq, tq), tq)

    outs = []
    for g in gs:
        acc = acc_sc[g]
        outs.append(acc[:, :dh] / acc[:, dh:dh + 1])
    out = jnp.concatenate(outs, axis=1).astype(o_ref.dtype)
    wd = FLASH_HEADS * dh
    for pp in range(N_HEADS_A // FLASH_HEADS):
        @pl.when(hp == pp)
        def _():
            o_ref[0, :, pp * wd:(pp + 1) * wd] = out


def _moba_flash(q_s, kaug, vaug, kmean):
    b, nh, s, dh = q_s.shape
    tq = MOBA_BLOCK
    nb = s // tq
    g = FLASH_HEADS
    return pl.pallas_call(
        _moba_flash_kernel,
        grid=(b, nb, nh // g),
        in_specs=[pl.BlockSpec((1, g, tq, dh), lambda bi, i, h: (bi, h, i, 0)),
                  pl.BlockSpec((1, g, s, 128), lambda bi, i, h: (bi, h, 0, 0)),
                  pl.BlockSpec((1, g, s, 128), lambda bi, i, h: (bi, h, 0, 0)),
                  pl.BlockSpec((1, g, dh, dh), lambda bi, i, h: (bi, h, 0, 0))],
        out_specs=pl.BlockSpec((1, tq, nh * dh), lambda bi, i, h: (bi, i, 0)),
        out_shape=jax.ShapeDtypeStruct((b, s, nh * dh), BF16),
        scratch_shapes=[pltpu.VMEM((g, tq, 128), F32), pltpu.VMEM((g, tq, 128), F32)],
        compiler_params=_cparams(("parallel", "parallel", "arbitrary")),
        name="moba_flash",
    )(q_s, kaug, vaug, kmean)


PAGES_PER_STEP = 64


def _page_mean_kernel(pt_ref, *refs):
    pages, out_ref = refs[:PAGES_PER_STEP], refs[PAGES_PER_STEP]
    s = pl.program_id(1)
    ppb = MOBA_BLOCK // PAGE_SIZE
    bps = PAGES_PER_STEP // ppb

    @pl.when(s == 0)
    def _():
        out_ref[...] = jnp.zeros_like(out_ref)

    lane = lax.broadcasted_iota(I32, out_ref.shape[2:], 1)
    for h in range(N_HEADS_A):
        acc = out_ref[0, h]
        for j in range(bps):
            tot = pages[ppb * j][h]
            for u in range(1, ppb):
                tot = tot + pages[ppb * j + u][h]
            col = jnp.sum(tot, axis=1, keepdims=True) * (1.0 / MOBA_BLOCK)
            acc = jnp.where(lane == s * bps + j, col, acc)
        out_ref[0, h] = acc


def _page_means(cache_kt, page_table, layer):
    db, n_pages = page_table.shape
    ppb = MOBA_BLOCK // PAGE_SIZE
    n_full = n_pages // ppb
    steps = n_full * ppb // PAGES_PER_STEP
    _, _, nh, dh, pg = cache_kt.shape

    def page_spec(u):
        return pl.BlockSpec((None, None, nh, dh, pg),
                            lambda b, s, pt: (layer, pt[b, s * PAGES_PER_STEP + u], 0, 0, 0))

    return pl.pallas_call(
        _page_mean_kernel,
        grid_spec=pltpu.PrefetchScalarGridSpec(
            num_scalar_prefetch=1,
            grid=(db, steps),
            in_specs=[page_spec(u) for u in range(PAGES_PER_STEP)],
            out_specs=pl.BlockSpec((1, nh, dh, n_full), lambda b, s, pt: (b, 0, 0, 0)),
        ),
        out_shape=jax.ShapeDtypeStruct((db, nh, dh, n_full), F32),
        compiler_params=_cparams(("parallel", "arbitrary")),
        name="page_means",
    )(page_table, *([cache_kt] * PAGES_PER_STEP))


def _sample_select_kernel(q_ref, km_ref, idx_ref):
    tp = q_ref.shape[2]
    nb = km_ref.shape[3]
    n_idx = lax.broadcasted_iota(I32, (tp, nb), 1)
    for h in range(N_HEADS_A):
        g = _dot3(q_ref[0, h], km_ref[0, h])
        cols = []
        for _ in range(MOBA_TOPK):
            mx = jnp.max(g, axis=1, keepdims=True)
            first = jnp.min(jnp.where(g == mx, n_idx, nb), axis=1, keepdims=True)
            cols.append(first)
            g = jnp.where(n_idx == first, -jnp.inf, g)
        idx_ref[0, h] = jnp.concatenate(cols, axis=1)


def _sample_select(q_s, kmean_t):
    db, nh, tp, dh = q_s.shape
    nb = kmean_t.shape[3]
    return pl.pallas_call(
        _sample_select_kernel,
        grid=(db,),
        in_specs=[pl.BlockSpec((1, nh, tp, dh), lambda b: (b, 0, 0, 0)),
                  pl.BlockSpec((1, nh, dh, nb), lambda b: (b, 0, 0, 0))],
        out_specs=pl.BlockSpec((1, nh, tp, MOBA_TOPK), lambda b: (b, 0, 0, 0)),
        out_shape=jax.ShapeDtypeStruct((db, nh, tp, MOBA_TOPK), I32),
        compiler_params=_cparams(("parallel",)),
        name="sample_select",
    )(q_s, kmean_t)


def _sample_attend_kernel(t_valid, layer, phys_ref, q_ref, kn_ref, vn_ref, ck_ref, cv_ref, o_ref,
                          kbuf, vbuf, sem):
    b = pl.program_id(0)
    t = pl.program_id(1)
    ppb = MOBA_BLOCK // PAGE_SIZE
    n_slab = MOBA_TOPK * ppb
    tp = q_ref.shape[2]

    n = b * t_valid + t
    slot = n % 2

    def copies(step, sl, h, u):
        page = phys_ref[(step * N_HEADS_A + h) * n_slab + u]
        return (pltpu.make_async_copy(ck_ref.at[layer, page, h], kbuf.at[sl, h, u], sem.at[sl, 0]),
                pltpu.make_async_copy(cv_ref.at[layer, page, h], vbuf.at[sl, h, u], sem.at[sl, 1]))

    def issue(step, sl):
        for h in range(N_HEADS_A):
            for u in range(n_slab):
                ck, cv = copies(step, sl, h, u)
                ck.start()
                cv.start()

    @pl.when(n == 0)
    def _():
        issue(0, 0)

    @pl.when(n + 1 < pl.num_programs(0) * t_valid)
    def _():
        issue(n + 1, 1 - slot)

    pltpu.make_async_copy(kbuf.at[slot], kbuf.at[slot], sem.at[slot, 0]).wait()
    pltpu.make_async_copy(vbuf.at[slot], vbuf.at[slot], sem.at[slot, 1]).wait()

    row = lax.broadcasted_iota(I32, (tp, 1), 0)
    key = lax.broadcasted_iota(I32, (tp, tp), 1)
    for h in range(N_HEADS_A):
        qh = q_ref[0, h].astype(BF16)
        s_own = _dg(qh, kn_ref[0, h].astype(BF16), 1, 1)
        s_own = jnp.where((key <= t) & (key < t_valid), s_own, -jnp.inf)
        s_sel = [_dg(qh, kbuf[slot, h, u].astype(BF16), 1, 0) for u in range(n_slab)]
        m = jnp.max(s_own, axis=1, keepdims=True)
        for sj in s_sel:
            m = jnp.maximum(m, jnp.max(sj, axis=1, keepdims=True))
        p_own = jnp.exp(s_own - m)
        l = jnp.sum(p_own, axis=1, keepdims=True)
        acc = _dg(p_own.astype(BF16), vn_ref[0, h].astype(BF16), 1, 0)
        for u, sj in enumerate(s_sel):
            pj = jnp.exp(sj - m)
            l = l + jnp.sum(pj, axis=1, keepdims=True)
            acc = acc + _dg(pj.astype(BF16), vbuf[slot, h, u].astype(BF16), 1, 1)
        out = acc / l
        o_ref[0, 0, h] = jnp.sum(jnp.where(row == t, out, 0.0), axis=0, keepdims=True)


def _sample_attend(q_s, k_new, v_new, cache_kt, cache_vt, phys, t_valid, layer):
    db, nh, tp, dh = q_s.shape
    n_slab = MOBA_TOPK * (MOBA_BLOCK // PAGE_SIZE)
    hm = pl.BlockSpec((1, nh, tp, dh), lambda b, t, ph: (b, 0, 0, 0))
    return pl.pallas_call(
        functools.partial(_sample_attend_kernel, t_valid, layer),
        grid_spec=pltpu.PrefetchScalarGridSpec(
            num_scalar_prefetch=1,
            grid=(db, t_valid),
            in_specs=[hm, hm, hm, pl.BlockSpec(memory_space=pl.ANY), pl.BlockSpec(memory_space=pl.ANY)],
            out_specs=pl.BlockSpec((1, 1, nh, 1, dh), lambda b, t, ph: (b, t, 0, 0, 0)),
            scratch_shapes=[pltpu.VMEM((2, nh, n_slab, dh, PAGE_SIZE), F32),
                            pltpu.VMEM((2, nh, n_slab, dh, PAGE_SIZE), F32),
                            pltpu.SemaphoreType.DMA((2, 2))],
        ),
        out_shape=jax.ShapeDtypeStruct((db, t_valid, nh, 1, dh), F32),
        compiler_params=_cparams(("arbitrary", "arbitrary")),
        name="sample_attend",
    )(phys, q_s, k_new, v_new, cache_kt, cache_vt)


def _rwkv_features(t_valid, cur, edge, mu_ref, w0_ref, dup_ref, a0_ref, aup_ref, gup_ref, kk_ref, ka_ref,
                   rk_ref, seg_ref):
    first = lax.broadcasted_iota(I32, (cur.shape[0], 1), 0) == 0
    prev = jnp.where(first, edge, pltpu.roll(cur, 1, 0))
    xs = cur + (prev - cur) * mu_ref[...]
    w = WIDTH_B
    r, k, v = xs[:, :w], xs[:, w:2 * w], xs[:, 2 * w:3 * w]
    dw = xs[:, 3 * w:3 * w + DECAY_LORA]
    da = xs[:, 3 * w + DECAY_LORA:3 * w + DECAY_LORA + AAA_LORA]
    dg = xs[:, 3 * w + DECAY_LORA + AAA_LORA:]
    lw = -DECAY_SCALE * jax.nn.sigmoid(w0_ref[...] + _dot3(jnp.tanh(dw), dup_ref[...]))
    a = jax.nn.sigmoid(a0_ref[...] + _dot3(da, aup_ref[...]))
    g = _dot3(jax.nn.sigmoid(dg), gup_ref[...])
    seg = seg_ref[...]
    kkr = k * kk_ref[...]
    kk = kkr / jnp.maximum(jnp.sqrt(_dot2_exact_rhs(kkr * kkr, seg)), 1e-12)
    k2 = k * (1.0 + (a - 1.0) * ka_ref[...])
    bonus = _dot2_exact_rhs(r * k2 * rk_ref[...], seg) * v
    if t_valid < cur.shape[0]:
        valid = lax.broadcasted_iota(I32, (cur.shape[0], 1), 0) < t_valid
        lw = jnp.where(valid, lw, 0.0)
        kk = jnp.where(valid, kk, 0.0)
        k2 = jnp.where(valid, k2, 0.0)
    return r, lw, k2, v, kk, kk * a, g, bonus


def _rwkv_chunk_kernel(L, t_valid, p_ref, sh_ref, mu_ref, w0_ref, dup_ref, a0_ref, aup_ref, gup_ref, kkp_ref,
                       ka_ref, rk_ref, s0_ref, lnw_ref, lnb_ref, seg_ref, o_ref, sT_ref, st_sc, last_sc):
    c = pl.program_id(1)
    n_rows = p_ref.shape[1]
    nsub = n_rows // L
    nh, dh = N_HEADS_B, HEAD_DIM_B

    @pl.when(c == 0)
    def _():
        st_sc[...] = s0_ref[0]
        last_sc[...] = sh_ref[0]

    cur = p_ref[0]
    r_f, lw_f, k_f, v_f, kk_f, b_f, g_f, bonus_f = _rwkv_features(
        t_valid, cur, last_sc[...], mu_ref, w0_ref, dup_ref, a0_ref, aup_ref, gup_ref, kkp_ref, ka_ref, rk_ref,
        seg_ref)
    last_sc[...] = cur[n_rows - 1:n_rows, :]

    ri = lax.broadcasted_iota(I32, (L, L), 0)
    ci = lax.broadcasted_iota(I32, (L, L), 1)
    strict = ri > ci
    tri = jnp.where(ri >= ci, 1.0, 0.0).astype(BF16)
    eye_l = jnp.where(ri == ci, 1.0, 0.0)
    rk = lax.broadcasted_iota(I32, (dh, dh), 0)
    ck = lax.broadcasted_iota(I32, (dh, dh), 1)
    ri2 = lax.broadcasted_iota(I32, (L, 2 * L), 0)
    ci2 = lax.broadcasted_iota(I32, (L, 2 * L), 1)
    incl2 = ri2 >= jnp.where(ci2 >= L, ci2 - L, ci2)

    alpha, beta, kappa, rho, vh, beta_t, kappa_t, g_last = [], [], [], [], [], [], [], []
    for sub in range(nsub):
        rows = slice(sub * L, (sub + 1) * L)
        lw = lw_f[rows]
        cum = _dot2_exact_rhs_left(tri, lw)
        cum_last = cum[L - 1:L, :]
        e_neg = jnp.exp(-cum)
        e_tail = jnp.exp(cum_last - cum)
        gl_all = jnp.exp(cum_last)
        kk_all, b_all, k_all = kk_f[rows], b_f[rows], k_f[rows]
        full = [kk_all * jnp.exp(cum - lw), b_all * e_neg, k_all * e_neg, r_f[rows] * jnp.exp(cum),
                v_f[rows], b_all * e_tail, k_all * e_tail]
        for dst, x in zip((alpha, beta, kappa, rho, vh, beta_t, kappa_t), full):
            dst.extend(x[:, h * dh:(h + 1) * dh] for h in range(nh))
        g_last.extend(gl_all[:, h * dh:(h + 1) * dh] for h in range(nh))
    its = range(nsub * nh)
    wcat = [jnp.concatenate([beta[i], kappa[i]], axis=0) for i in its]
    za = [_dot3(alpha[i], wcat[i], 1, 1) for i in its]
    zr = [_dotb(rho[i], wcat[i], 1, 1) for i in its]
    n_mat = [jnp.where(strict, za[i][:, :L], 0.0) for i in its]
    m_mat = [jnp.where(strict, za[i][:, L:], 0.0) for i in its]
    nrmr = [jnp.where(incl2, zr[i], 0.0) for i in its]
    mv = [_dot3(m_mat[i], vh[i]) for i in its]
    d = [eye_l - jnp.where(ri // 2 == ci // 2, n_mat[i], 0.0) for i in its]
    s = 2
    while s < L:
        lower_left = (ri // (2 * s) == ci // (2 * s)) & ((ri % (2 * s)) >= s) & ((ci % (2 * s)) < s)
        de = [_dot3(d[i], jnp.where(lower_left, n_mat[i], 0.0)) for i in its]
        d = [d[i] - _dot3(de[i], d[i]) for i in its]
        s *= 2
    ta = [_dot3(d[i], jnp.concatenate([alpha[i], mv[i]], axis=1)) for i in its]
    abar = [ta[i][:, :dh] for i in its]
    pv = [jnp.concatenate([-ta[i][:, dh:], vh[i]], axis=0) for i in its]
    rpp = [rho[i] - _dotb(nrmr[i][:, :L], abar[i]) for i in its]
    y0 = [_dotb(nrmr[i], pv[i]) for i in its]
    gt = [jnp.where(rk == ck, g_last[i], 0.0) - _dot3(beta_t[i], abar[i], 0, 0) for i in its]
    ht = [_dot3(jnp.concatenate([beta_t[i], kappa_t[i]], axis=0), pv[i], 0, 0) for i in its]
    ys = []
    for sub in range(nsub):
        base = sub * nh
        upd = [_dot3(jnp.concatenate([rpp[base + h], gt[base + h]], axis=0), st_sc[h])
               for h in range(nh)]
        for h in range(nh):
            st_sc[h] = upd[h][L:] + ht[base + h]
        ys.append(jnp.concatenate([y0[base + h] + upd[h][:L] for h in range(nh)], axis=1))
    y = ys[0] if nsub == 1 else jnp.concatenate(ys, axis=0)
    seg = seg_ref[...]
    mu = _dot2_exact_rhs(y, seg) * (1.0 / dh)
    yc = y - mu
    var = _dot2_exact_rhs(yc * yc, seg) * (1.0 / dh)
    yn = yc * lax.rsqrt(var + LN_X_EPS) * lnw_ref[...] + lnb_ref[...]
    o_ref[0] = ((yn + bonus_f) * g_f).astype(o_ref.dtype)
    sT_ref[0] = st_sc[...]


def _dot2_exact_rhs_left(m_bf16, x):
    xh, xl = _split(x)
    return _dg(m_bf16, xh, 1, 0) + _dg(m_bf16, xl, 1, 0)


def _rwkv_mix(p_rw, shift0, s0_t, prm, chunk, chunks_per_step, t_valid):
    bq, s, cols = p_rw.shape
    w = WIDTH_B
    rows = chunk * chunks_per_step
    row = lambda v: v.reshape(1, -1).astype(F32)
    params = [row(prm['rw_mu']), row(prm['rw_w0']), prm['rw_decay_up'], row(prm['rw_a0']), prm['rw_a_up'],
              prm['rw_g_up'], row(prm['rw_k_k']), row(prm['rw_k_a']), row(prm['rw_r_k'])]
    st = pl.BlockSpec((1, N_HEADS_B, HEAD_DIM_B, HEAD_DIM_B), lambda bi, c: (bi, 0, 0, 0))
    const = lambda bi, c: (0, 0)
    return pl.pallas_call(
        functools.partial(_rwkv_chunk_kernel, chunk, t_valid),
        grid=(bq, s // rows),
        in_specs=[pl.BlockSpec((1, rows, cols), lambda bi, c: (bi, c, 0)),
                  pl.BlockSpec((1, 1, cols), lambda bi, c: (bi, 0, 0))]
        + [pl.BlockSpec(p.shape, const) for p in params]
        + [st, pl.BlockSpec((1, w), const), pl.BlockSpec((1, w), const), pl.BlockSpec((w, w), const)],
        out_specs=[pl.BlockSpec((1, rows, w), lambda bi, c: (bi, c, 0)), st],
        out_shape=[jax.ShapeDtypeStruct((bq, s, w), BF16),
                   jax.ShapeDtypeStruct((bq, N_HEADS_B, HEAD_DIM_B, HEAD_DIM_B), F32)],
        scratch_shapes=[pltpu.VMEM((N_HEADS_B, HEAD_DIM_B, HEAD_DIM_B), F32), pltpu.VMEM((1, cols), F32)],
        compiler_params=_cparams(("parallel", "arbitrary")),
        name="rwkv_mix",
    )(p_rw, shift0, *params, s0_t, row(prm['ln_x_w']), row(prm['ln_x_b']), _seg_ones(w, HEAD_DIM_B))


def _mem_kv_kernel(mem_ref, g_ref, w_ref, kg_ref, mk_ref, mv_ref):
    h = _rms(mem_ref[0], g_ref[...]).astype(BF16)
    kv = _dg(h, w_ref[...], 1, 0)
    for hm in range(N_HEADS_M):
        sl = slice(hm * HEAD_DIM_M, (hm + 1) * HEAD_DIM_M)
        mk_ref[0, :, sl] = _rms(kv[:, sl], kg_ref[...])
    mv_ref[0] = kv[:, WIDTH_M:]


def _memory_kv(mem, norm_mem, w_mem_kv, k_norm_m):
    b, m, d = mem.shape
    const = lambda i: (0, 0)
    out = pl.BlockSpec((1, m, WIDTH_M), lambda i: (i, 0, 0))
    return pl.pallas_call(
        _mem_kv_kernel,
        grid=(b,),
        in_specs=[pl.BlockSpec((1, m, d), lambda i: (i, 0, 0)), pl.BlockSpec((1, d), const),
                  pl.BlockSpec((d, 2 * WIDTH_M), const), pl.BlockSpec((1, HEAD_DIM_M), const)],
        out_specs=[out, out],
        out_shape=[jax.ShapeDtypeStruct((b, m, WIDTH_M), F32)] * 2,
        compiler_params=_cparams(("parallel",)),
        name="memory_kv",
    )(mem, norm_mem.reshape(1, d), w_mem_kv.astype(BF16), k_norm_m.reshape(1, HEAD_DIM_M))


def _mem_attend_kernel(head_rows, q_ref, mk_ref, mv_ref, g_ref, o_ref):
    q = q_ref[0]
    m = mk_ref.shape[1] // N_HEADS_M if head_rows else mk_ref.shape[1]
    for hm in range(N_HEADS_M):
        sl = slice(hm * HEAD_DIM_M, (hm + 1) * HEAD_DIM_M)
        if head_rows:
            mk_h = mk_ref[0, pl.ds(hm, m, stride=N_HEADS_M), :]
            mv_h = mv_ref[0, pl.ds(hm, m, stride=N_HEADS_M), :]
        else:
            mk_h, mv_h = mk_ref[0, :, sl], mv_ref[0, :, sl]
        qh = (_rms(q[:, sl], g_ref[...]) * (HEAD_DIM_M ** -0.5)).astype(BF16)
        s = _dg(qh, mk_h.astype(BF16), 1, 1)
        p = jnp.exp(s - jnp.max(s, axis=1, keepdims=True))
        o = _dg(p.astype(BF16), mv_h.astype(BF16), 1, 0) / jnp.sum(p, axis=1, keepdims=True)
        o_ref[0, :, sl] = o.astype(o_ref.dtype)


def _memory_attend(q_m, mk, mv, q_norm_m, tq):
    b, s, w = q_m.shape
    head_rows = mk.shape[2] == HEAD_DIM_M
    kv = pl.BlockSpec((1,) + mk.shape[1:], lambda bi, j: (bi, 0, 0))
    return pl.pallas_call(
        functools.partial(_mem_attend_kernel, head_rows),
        grid=(b, s // tq),
        in_specs=[pl.BlockSpec((1, tq, w), lambda bi, j: (bi, j, 0)), kv, kv,
                  pl.BlockSpec((1, HEAD_DIM_M), lambda bi, j: (0, 0))],
        out_specs=pl.BlockSpec((1, tq, w), lambda bi, j: (bi, j, 0)),
        out_shape=jax.ShapeDtypeStruct((b, s, w), BF16),
        compiler_params=_cparams(("parallel", "parallel")),
        name="memory_attend",
    )(q_m, mk, mv, q_norm_m.reshape(1, HEAD_DIM_M))


def _merge_kernel(x_ref, oa_ref, ob_ref, om_ref, gl_ref, wb_ref, wo_ref, ng_ref, rw_ref, rb_ref,
                  x1_ref, h_ref, e_ref, gate_ref, rank_ref, cnt_ref, base_sc):
    d = x_ref.shape[1]
    gl = gl_ref[...]
    merged = jnp.zeros(x_ref.shape, F32)
    for n, o_ref in enumerate((oa_ref, ob_ref, om_ref)):
        y = _dg(o_ref[...], wb_ref[n], 1, 0)
        merged = merged + jax.nn.sigmoid(gl[:, n * d:(n + 1) * d].astype(F32)) * y
    x1 = x_ref[...] + _dg(merged.astype(BF16), wo_ref[...], 1, 0)
    x1_ref[...] = x1
    hn = _rms(x1, ng_ref[...])
    h_ref[...] = hn
    logits = _dot3(hn, rw_ref[...]) + rb_ref[...]
    tm, ne = logits.shape
    e_idx = lax.broadcasted_iota(I32, (tm, ne), 1)
    vals, idxs = [], []
    g = logits
    for _ in range(TOP_K):
        mx = jnp.max(g, axis=1, keepdims=True)
        first = jnp.min(jnp.where(g == mx, e_idx, ne), axis=1, keepdims=True)
        vals.append(mx)
        idxs.append(first)
        g = jnp.where(e_idx == first, -jnp.inf, g)
    top = jnp.concatenate(vals, axis=1)
    pe = jnp.exp(top - vals[0])
    gate_ref[...] = pe / jnp.sum(pe, axis=1, keepdims=True)
    e_ref[...] = jnp.concatenate(idxs, axis=1)

    @pl.when(pl.program_id(0) == 0)
    def _():
        base_sc[...] = jnp.zeros_like(base_sc)

    ohs = [jnp.where(e_idx == idx, 1.0, 0.0) for idx in idxs]
    cnt = ohs[0] + ohs[1] + ohs[2] + ohs[3]
    ri = lax.broadcasted_iota(I32, (tm, tm), 0)
    ci = lax.broadcasted_iota(I32, (tm, tm), 1)
    tri = jnp.where(ri > ci, 1.0, 0.0).astype(BF16)
    tot = _dg(tri, cnt.astype(BF16), 1, 0) + base_sc[...]
    rank_ref[...] = jnp.concatenate([jnp.sum(oh * tot, axis=1, keepdims=True) for oh in ohs],
                                    axis=1).astype(I32)
    base_sc[...] = base_sc[...] + jnp.sum(cnt, axis=0, keepdims=True)
    cnt_ref[...] = base_sc[...].astype(I32)


def _merge_and_route(x2, o_a, o_b, o_m, gl, prm, tm):
    n, d = x2.shape
    const2 = lambda i: (0, 0)
    tok = lambda wd: pl.BlockSpec((tm, wd), lambda i: (i, 0))
    return pl.pallas_call(
        _merge_kernel,
        grid=(n // tm,),
        in_specs=[tok(d), tok(512), tok(512), tok(512), tok(3 * d),
                  pl.BlockSpec((3, 512, d), lambda i: (0, 0, 0)), pl.BlockSpec((d, d), const2),
                  pl.BlockSpec((1, d), const2), pl.BlockSpec((d, N_EXPERTS), const2),
                  pl.BlockSpec((1, N_EXPERTS), const2)],
        out_specs=[tok(d), tok(d), tok(TOP_K), tok(TOP_K), tok(TOP_K), pl.BlockSpec((1, N_EXPERTS), const2)],
        out_shape=[jax.ShapeDtypeStruct((n, d), F32), jax.ShapeDtypeStruct((n, d), F32),
                   jax.ShapeDtypeStruct((n, TOP_K), I32), jax.ShapeDtypeStruct((n, TOP_K), F32),
                   jax.ShapeDtypeStruct((n, TOP_K), I32), jax.ShapeDtypeStruct((1, N_EXPERTS), I32)],
        scratch_shapes=[pltpu.VMEM((1, N_EXPERTS), F32)],
        compiler_params=_cparams(("arbitrary",)),
        name="merge_route",
    )(x2, o_a, o_b, o_m, gl, prm['w_branch'].astype(BF16), prm['w_out'].astype(BF16),
      prm['norm_ffn'].reshape(1, d), prm['router_w'], prm['router_b'].reshape(1, N_EXPERTS))


def _onehots(e):
    tm = e.shape[0]
    e_idx = lax.broadcasted_iota(I32, (tm, N_EXPERTS), 1)
    return [jnp.where(e[:, k:k + 1] == e_idx, 1.0, 0.0) for k in range(TOP_K)]


def _moe_rows_kernel(e_ref, rank_ref, start_ref, row_ref):
    ohs = _onehots(e_ref[...])
    st = start_ref[...].astype(F32)
    base = jnp.concatenate([jnp.sum(oh * st, axis=1, keepdims=True) for oh in ohs], axis=1)
    row_ref[...] = rank_ref[...] + base.astype(I32)


def _moe_rows(top_e, rank, starts, tm):
    n = top_e.shape[0]
    tm = MOE_ROWS_TILE if n % MOE_ROWS_TILE == 0 else tm
    tok = pl.BlockSpec((tm, TOP_K), lambda i: (i, 0))
    return pl.pallas_call(
        _moe_rows_kernel,
        grid=(n // tm,),
        in_specs=[tok, tok, pl.BlockSpec((1, N_EXPERTS), lambda i: (0, 0))],
        out_specs=tok,
        out_shape=jax.ShapeDtypeStruct((n, TOP_K), I32),
        compiler_params=_cparams(("parallel",)),
        name="moe_rows",
    )(top_e, rank, starts)


def _dispatch_kernel(row_ref, h_ref, xs_in_ref, xs_ref, sem):
    del xs_in_ref
    groups = h_ref.shape[0]

    def start(g, carry):
        for u in range(SUBLANES):
            for k in range(TOP_K):
                r = row_ref[(g * SUBLANES + u) * TOP_K + k]
                pltpu.make_async_copy(h_ref.at[g, pl.ds(u, 1)], xs_ref.at[pl.ds(r, 1)], sem).start()
        return carry

    lax.fori_loop(0, groups, start, 0)
    all_rows = xs_ref.at[pl.ds(0, groups * SUBLANES * TOP_K)]
    pltpu.make_async_copy(all_rows, all_rows, sem).wait()


def _zero_rows_kernel(o_ref):
    o_ref[...] = jnp.zeros_like(o_ref)


def _zero_rows(n_rows, d):
    tile = next(t for t in (1024, 512, 256, MOE_ROWS_SMALL) if n_rows % t == 0)
    return pl.pallas_call(
        _zero_rows_kernel,
        grid=(n_rows // tile,),
        out_specs=pl.BlockSpec((tile, d), lambda i: (i, 0)),
        out_shape=jax.ShapeDtypeStruct((n_rows, d), F32),
        compiler_params=_cparams(("parallel",)),
        name="zero_rows",
    )()


def _moe_dispatch(h, row_flat, n_rows, tm):
    n, d = h.shape
    xs0 = _zero_rows(n_rows, d)
    return pl.pallas_call(
        _dispatch_kernel,
        grid=(n // tm,),
        in_specs=[pl.BlockSpec((tm * TOP_K,), lambda i: (i,), memory_space=pltpu.SMEM),
                  pl.BlockSpec((tm // SUBLANES, SUBLANES, d), lambda i: (i, 0, 0)),
                  pl.BlockSpec(memory_space=pl.ANY)],
        out_specs=pl.BlockSpec(memory_space=pl.ANY),
        out_shape=jax.ShapeDtypeStruct((n_rows, d), F32),
        scratch_shapes=[pltpu.SemaphoreType.DMA(())],
        input_output_aliases={2: 0},
        compiler_params=_cparams(("arbitrary",)),
        name="moe_dispatch",
    )(row_flat, h.reshape(n // SUBLANES, SUBLANES, d), xs0)


def _swiglu(u, d_ff):
    u_glu = jnp.minimum(u[:, :d_ff], SWIGLU_LIMIT)
    u_lin = jnp.clip(u[:, d_ff:], -SWIGLU_LIMIT, SWIGLU_LIMIT)
    return u_glu * jax.nn.sigmoid(SWIGLU_ALPHA * u_glu) * (u_lin + 1.0)


def _moe_ffn_kernel(be_ref, nu_ref, xs_ref, w1_ref, b1_ref, w2_ref, b2_ref, y_ref, w1_sc, w2_sc):
    i = pl.program_id(0)
    prev = be_ref[jnp.maximum(i - 1, 0)]
    first = (i == 0) | (be_ref[i] != prev)

    @pl.when(first)
    def _():
        w1_sc[...] = w1_ref[...].astype(BF16)
        w2_sc[...] = w2_ref[...].astype(BF16)

    @pl.when(i < nu_ref[0])
    def _():
        u = _dg(xs_ref[...].astype(BF16), w1_sc[...], 1, 0) + b1_ref[...]
        act = _swiglu(u, w2_ref.shape[0])
        y_ref[...] = _dg(act.astype(BF16), w2_sc[...], 1, 0) + b2_ref[...]

    @pl.when(i >= nu_ref[0])
    def _():
        y_ref[...] = jnp.zeros_like(y_ref)


def _moe_ffn(xs, blk_e, n_used, w1, b1, w2, b2, rows):
    n_rows, d = xs.shape
    ne, _, f2 = w1.shape
    d_ff = w2.shape[1]
    nblk = n_rows // rows
    return pl.pallas_call(
        _moe_ffn_kernel,
        grid_spec=pltpu.PrefetchScalarGridSpec(
            num_scalar_prefetch=2,
            grid=(nblk,),
            in_specs=[pl.BlockSpec((rows, d), lambda i, be, nu: (i, 0)),
                      pl.BlockSpec((None, d, f2), lambda i, be, nu: (be[i], 0, 0)),
                      pl.BlockSpec((None, 1, f2), lambda i, be, nu: (be[i], 0, 0)),
                      pl.BlockSpec((None, d_ff, d), lambda i, be, nu: (be[i], 0, 0)),
                      pl.BlockSpec((None, 1, d), lambda i, be, nu: (be[i], 0, 0))],
            out_specs=pl.BlockSpec((rows, d), lambda i, be, nu: (i, 0)),
            scratch_shapes=[pltpu.VMEM((d, f2), BF16), pltpu.VMEM((d_ff, d), BF16)],
        ),
        out_shape=jax.ShapeDtypeStruct((n_rows, d), F32),
        compiler_params=_cparams(("arbitrary",)),
        name="moe_ffn",
    )(blk_e, n_used, xs, w1, b1.reshape(ne, 1, f2), w2, b2.reshape(ne, 1, d))


def _combine_kernel(row_ref, yb_ref, x1_ref, gate_ref, y_ref, buf, sem):
    tm = x1_ref.shape[0]

    half = tm // 2
    per_tile = 2 * TOP_K

    def start(g, carry):
        for c in range(2):
            for k in range(TOP_K):
                r = row_ref[(c * half + g) * TOP_K + k]
                pltpu.make_async_copy(yb_ref.at[pl.ds(r, 1)], buf.at[g, pl.ds(TOP_K * c + k, 1)], sem).start()
        return carry

    lax.fori_loop(0, half, start, 0)
    pltpu.make_async_copy(buf, buf, sem).wait()
    gates = gate_ref[...]
    for c in range(2):
        tok = slice(c * half, (c + 1) * half)
        acc = buf[:, TOP_K * c, :] * gates[tok, 0:1]
        for k in range(1, TOP_K):
            acc = acc + buf[:, TOP_K * c + k, :] * gates[tok, k:k + 1]
        y_ref[tok, :] = x1_ref[tok, :] + acc


def _moe_combine(yb, row_flat, x1, gates, tm):
    n, d = x1.shape
    return pl.pallas_call(
        _combine_kernel,
        grid=(n // tm,),
        in_specs=[pl.BlockSpec((tm * TOP_K,), lambda i: (i,), memory_space=pltpu.SMEM),
                  pl.BlockSpec(memory_space=pl.ANY),
                  pl.BlockSpec((tm, d), lambda i: (i, 0)),
                  pl.BlockSpec((tm, TOP_K), lambda i: (i, 0))],
        out_specs=pl.BlockSpec((tm, d), lambda i: (i, 0)),
        out_shape=jax.ShapeDtypeStruct((n, d), F32),
        scratch_shapes=[pltpu.VMEM((tm // 2, 2 * TOP_K, d), F32), pltpu.SemaphoreType.DMA(())],
        compiler_params=_cparams(("arbitrary",)),
        name="moe_combine",
    )(row_flat, yb, x1, gates)


def _moe_block(x1, hn, top_e, gates, rank, counts, prm, tm):
    n, d = x1.shape
    rows = MOE_ROWS if n * TOP_K >= N_EXPERTS * MOE_ROWS else MOE_ROWS_SMALL
    counts = counts[0]
    padded = (counts + rows - 1) // rows * rows
    p_end = jnp.cumsum(padded)
    starts = (p_end - padded).astype(I32)
    nblk = -(-(n * TOP_K + N_EXPERTS * (rows - 1)) // rows)
    n_used = (p_end[-1] // rows).astype(I32)
    blk_i = jnp.minimum(jnp.arange(nblk, dtype=I32), n_used - 1)
    n_before = jnp.sum((p_end[None, :] <= (blk_i * rows)[:, None]).astype(I32), axis=1)
    blk_e = jnp.minimum(n_before, N_EXPERTS - 1).astype(I32)
    row = _moe_rows(top_e, rank, starts[None, :], tm)
    row_flat = row.reshape(-1)
    xs = _moe_dispatch(hn, row_flat, nblk * rows, tm)
    yb = _moe_ffn(xs, blk_e, n_used[None], prm['moe_w1'], prm['moe_b1'], prm['moe_w2'], prm['moe_b2'], rows)
    return _moe_combine(yb, row_flat, x1, gates, tm)


def _split_w_in(w_in):
    d = w_in.shape[0]
    a = 3 * WIDTH_A
    b = a + RWKV_COLS
    c = b + WIDTH_M
    wb = w_in.astype(BF16)
    return [wb[:, :a], wb[:, a:b], wb[:, b:c], wb[:, c:]]


def _token_mix_tail(x2, o_a, o_b, o_m, gl, prm, tm):
    x1, hn, top_e, gates, rank, counts = _merge_and_route(x2, o_a, o_b, o_m, gl, prm, tm)
    return _moe_block(x1, hn, top_e, gates, rank, counts, prm, tm)


def _layer_prompt(x, mem, prm, ws):
    b, s, d = x.shape
    tm = 256
    x2 = x.reshape(b * s, d)
    pos = jnp.arange(s, dtype=I32)
    p_rw, q_m, gl, q_s, k_t, v_t, kaug, vaug, kmean = _projection_and_moba_prep(
        x2, prm['norm_mix'].reshape(1, d), ws, b, pos, prm['q_norm_a'], prm['k_norm_a'])
    k_o, v_o = jnp.swapaxes(k_t, 2, 3), jnp.swapaxes(v_t, 2, 3)
    o_a = _moba_flash(q_s, kaug, vaug, kmean).reshape(b * s, WIDTH_A)

    p3 = p_rw.reshape(b, s, RWKV_COLS)
    s0_t = jnp.zeros((b, N_HEADS_B, HEAD_DIM_B, HEAD_DIM_B), F32)
    o_b, st = _rwkv_mix(p3, jnp.zeros((b, 1, RWKV_COLS), F32), s0_t, prm, RWKV_CHUNK, RWKV_CHUNKS_PER_STEP,
                        RWKV_CHUNK * RWKV_CHUNKS_PER_STEP)
    wkv = jnp.swapaxes(st, 2, 3)

    mk, mv = _memory_kv(mem, prm['norm_mem'], prm['w_mem_kv'], prm['k_norm_m'])
    o_m = _memory_attend(q_m.reshape(b, s, WIDTH_M), mk, mv, prm['q_norm_m'], 512)

    y = _token_mix_tail(x2, o_a, o_b.reshape(b * s, WIDTH_B), o_m.reshape(b * s, WIDTH_M), gl, prm, tm)
    m = mem.shape[1]
    return (y.reshape(b, s, d), k_o, v_o, wkv, p3[:, -1],
            mk.reshape(b, m, N_HEADS_M, HEAD_DIM_M), mv.reshape(b, m, N_HEADS_M, HEAD_DIM_M))


def _layer_sample(x, cache_k, cache_v, mem_k, mem_v, wkv0, shift0, page_table, layer, prm, ws):
    db, t, d = x.shape
    tp = SAMPLE_T_PAD
    past_len = page_table.shape[1] * PAGE_SIZE
    assert past_len % MOBA_BLOCK == 0 and t <= tp
    ppb = MOBA_BLOCK // PAGE_SIZE
    n = db * tp
    x2 = jnp.pad(x, ((0, 0), (0, tp - t), (0, 0))).reshape(n, d)
    qkv, p_rw, q_m, gl = _in_projection(x2, prm['norm_mix'].reshape(1, d), ws, n)
    pos = past_len + jnp.arange(tp, dtype=I32)
    q_s, k_o, v_o = _moba_prep(qkv.reshape(db, tp, -1), pos, prm['q_norm_a'], prm['k_norm_a'], tp)

    cache_kt, cache_vt = jnp.swapaxes(cache_k, 3, 4), jnp.swapaxes(cache_v, 3, 4)
    kmean_t = _page_means(cache_kt, page_table, layer)
    idx = _sample_select(q_s, kmean_t)
    idx = jnp.transpose(idx[:, :, :t], (0, 2, 1, 3))
    logical = idx[..., None] * ppb + jnp.arange(ppb, dtype=I32)
    phys = page_table[jnp.arange(db)[:, None, None, None, None], logical].reshape(-1).astype(I32)
    o_a = _sample_attend(q_s, k_o, v_o, cache_kt, cache_vt, phys, t, layer)
    o_a = jnp.pad(o_a.reshape(db, t, WIDTH_A), ((0, 0), (0, tp - t), (0, 0))).reshape(n, WIDTH_A).astype(BF16)

    p3 = p_rw.reshape(db, tp, RWKV_COLS)
    o_b, st = _rwkv_mix(p3, shift0[:, None, :], jnp.swapaxes(wkv0, 2, 3), prm, tp, 1, t)
    wkv = jnp.swapaxes(st, 2, 3)

    m = mem_k.shape[1]
    o_m = _memory_attend(q_m.reshape(db, tp, WIDTH_M), mem_k.reshape(db, m * N_HEADS_M, HEAD_DIM_M),
                         mem_v.reshape(db, m * N_HEADS_M, HEAD_DIM_M), prm['q_norm_m'], tp)

    y = _token_mix_tail(x2, o_a, o_b.reshape(n, WIDTH_B), o_m.reshape(n, WIDTH_M), gl, prm, n)
    return (y.reshape(db, tp, d)[:, :t], k_o[:, :, :t], v_o[:, :, :t], wkv, p3[:, t - 1])


def kernel(x_prompt, x_sample, mem_prompt, cache_k, cache_v, cache_mem_k, cache_mem_v, state_wkv, state_shift,
           page_table, norm_mix, norm_mem, norm_ffn, w_in, q_norm_a, k_norm_a, q_norm_m, k_norm_m, w_mem_kv,
           rw_mu, rw_w0, rw_decay_up, rw_a0, rw_a_up, rw_g_up, rw_k_k, rw_k_a, rw_r_k, ln_x_w, ln_x_b,
           w_branch, w_out, router_w, router_b, moe_w1, moe_b1, moe_w2, moe_b2):
    depth = w_in.shape[0]
    xp, xs = x_prompt, x_sample
    outs_p = [[] for _ in range(6)]
    outs_s = [[] for _ in range(4)]
    for l in range(depth):
        prm = dict(norm_mix=norm_mix[l], norm_mem=norm_mem[l], norm_ffn=norm_ffn[l], w_in=w_in[l],
                   q_norm_a=q_norm_a[l], k_norm_a=k_norm_a[l], q_norm_m=q_norm_m[l], k_norm_m=k_norm_m[l],
                   w_mem_kv=w_mem_kv[l], rw_mu=rw_mu[l], rw_w0=rw_w0[l], rw_decay_up=rw_decay_up[l],
                   rw_a0=rw_a0[l], rw_a_up=rw_a_up[l], rw_g_up=rw_g_up[l], rw_k_k=rw_k_k[l], rw_k_a=rw_k_a[l],
                   rw_r_k=rw_r_k[l].reshape(-1), ln_x_w=ln_x_w[l], ln_x_b=ln_x_b[l], w_branch=w_branch[l],
                   w_out=w_out[l], router_w=router_w[l], router_b=router_b[l], moe_w1=moe_w1[l],
                   moe_b1=moe_b1[l], moe_w2=moe_w2[l], moe_b2=moe_b2[l])
        ws = _split_w_in(prm['w_in'])
        xp, *rest_p = _layer_prompt(xp, mem_prompt, prm, ws)
        for acc, val in zip(outs_p, rest_p):
            acc.append(val)
        xs, *rest_s = _layer_sample(xs, cache_k, cache_v, cache_mem_k[l], cache_mem_v[l], state_wkv[l],
                                    state_shift[l], page_table, l, prm, ws)
        for acc, val in zip(outs_s, rest_s):
            acc.append(val)
    kp, vp, wkvp, shp, mkp, mvp = [jnp.stack(a) for a in outs_p]
    ksm, vsm, wkvs, shs = [jnp.stack(a) for a in outs_s]
    return (xp, xs, kp, vp, wkvp, shp, mkp, mvp, ksm, vsm, wkvs, shs)
```
